```python
import math
import jax, jax.numpy as jnp
from jax import lax
import numpy as np

D_MODEL = 1024
BATCH = 8
SEQ = 2048
DEPTH = 1

MEM_LEN = 256
EPS = 1e-6
NEG_INF = -1e30
BIG = 1e30

NSA_HEADS = 16
NSA_GROUPS = 4
NSA_HPG = NSA_HEADS // NSA_GROUPS
NSA_DH = 64
NSA_SCALE = NSA_DH ** -0.5
CMP_BLOCK = 32
CMP_STRIDE = 16
CMP_HIDDEN = 128
SEL_BLOCK = 64
SEL_TOPK = 16
N_LOCAL_FORCED = 2
WINDOW = 512
SEL_QBLOCK = 32
WIN_QBLOCK = 128

ML_HEADS = 4
ML_DH = 128
ML_CHUNK = 64
CONV_WIDTH = 4

MEM_HEADS = 4
MEM_DH = 128
MEM_SCALE = MEM_DH ** -0.5

D_FF = -(-8 * D_MODEL // (3 * 256)) * 256

N_BRANCH = 3
NSA_Q = NSA_HEADS * NSA_DH
NSA_KV = NSA_GROUPS * NSA_DH
ML_W = ML_HEADS * ML_DH
MEM_W = MEM_HEADS * MEM_DH
IN_SPLITS = (NSA_Q, 6 * NSA_KV, 3 * NSA_HEADS, 3 * ML_W, 2 * ML_HEADS, ML_W, MEM_W, N_BRANCH * D_MODEL)
IN_WIDTH = NSA_Q + 6 * NSA_KV + 3 * NSA_HEADS + 3 * ML_W + 2 * ML_HEADS + ML_W + MEM_W + N_BRANCH * D_MODEL

kernel_name = "hybrid_nsa_mlstm_memxattn_block"


def rmsnorm(x, g):
    xf = x.astype(jnp.float32)
    y = xf * lax.rsqrt(jnp.mean(xf * xf, axis=-1, keepdims=True) + EPS)
    return (y * g.astype(jnp.float32)).astype(x.dtype)


def masked_softmax(s, mask):
    s = jnp.where(mask, s.astype(jnp.float32), NEG_INF)
    return jax.nn.softmax(s, axis=-1) * mask


def compress_blocks(kv, pe, w1, w2):
    S = kv.shape[1]
    n_cmp = (S - CMP_BLOCK) // CMP_STRIDE + 1
    idx = jnp.arange(n_cmp)[:, None] * CMP_STRIDE + jnp.arange(CMP_BLOCK)[None, :]
    blocks = kv[:, idx] + pe[None, None, :, None, :]
    hid = jax.nn.silu(jnp.einsum('bnlgd,ldh->bngh', blocks, w1))
    return jnp.einsum('bngh,hd->bngd', hid, w2)


def nsa_compressed(q, k, v, pe_k, w1_k, w2_k, pe_v, w1_v, w2_v):
    S = q.shape[1]
    kc = compress_blocks(k, pe_k, w1_k, w2_k)
    vc = compress_blocks(v, pe_v, w1_v, w2_v)
    n_cmp = kc.shape[1]
    s = jnp.einsum('bsghd,bngd->bghsn', q, kc) * NSA_SCALE
    t = jnp.arange(S)
    end = jnp.arange(n_cmp) * CMP_STRIDE + CMP_BLOCK - 1
    p = masked_softmax(s, end[None, :] <= t[:, None])
    o = jnp.einsum('bghsn,bngd->bsghd', p.astype(vc.dtype), vc)
    return o, p


def cmp_to_sel_map(n_cmp, n_sel):
    c0 = jnp.arange(n_cmp) * CMP_STRIDE
    s0 = jnp.arange(n_sel) * SEL_BLOCK
    ov = jnp.minimum(c0[:, None] + CMP_BLOCK, s0[None, :] + SEL_BLOCK) - jnp.maximum(c0[:, None], s0[None, :])
    return jnp.clip(ov, 0, None).astype(jnp.float32) / CMP_BLOCK


def nsa_select(p_cmp, S):
    n_cmp = p_cmp.shape[-1]
    n_sel = S // SEL_BLOCK
    imp = jnp.einsum('bghsn,nj->bgsj', p_cmp, cmp_to_sel_map(n_cmp, n_sel))
    qblk = jnp.arange(S) // SEL_BLOCK
    j = jnp.arange(n_sel)
    rel = qblk[:, None] - j[None, :]
    causal = rel >= 0
    forced = causal & ((j[None, :] == 0) | (rel < N_LOCAL_FORCED))
    score = jnp.where(forced, BIG, jnp.where(causal, imp, NEG_INF))
    top_s, top_i = lax.top_k(score, min(SEL_TOPK, n_sel))
    return top_i, top_s > 0.5 * NEG_INF


def nsa_selected(q, k, v, top_i, top_ok):
    B, S, G, HPG, dh = q.shape
    n_sel = S // SEL_BLOCK
    nk = top_i.shape[-1]
    nq = S // SEL_QBLOCK
    kb = k.reshape(B, n_sel, SEL_BLOCK, G, dh).transpose(0, 3, 1, 2, 4)
    vb = v.reshape(B, n_sel, SEL_BLOCK, G, dh).transpose(0, 3, 1, 2, 4)
    qb = jnp.moveaxis(q.reshape(B, nq, SEL_QBLOCK, G, HPG, dh), 1, 0)
    ib = jnp.moveaxis(top_i.reshape(B, G, nq, SEL_QBLOCK, nk), 2, 0)
    mb = jnp.moveaxis(top_ok.reshape(B, G, nq, SEL_QBLOCK, nk), 2, 0)
    bi = jnp.arange(B)[:, None, None, None]
    gi = jnp.arange(G)[None, :, None, None]

    def body(args):
        qi, ii, mi, blk = args
        kg = kb[bi, gi, ii]
        vg = vb[bi, gi, ii]
        tpos = blk * SEL_QBLOCK + jnp.arange(SEL_QBLOCK)
        kpos = ii[..., None] * SEL_BLOCK + jnp.arange(SEL_BLOCK)
        mask = mi[..., None] & (kpos <= tpos[None, None, :, None, None])
        s = jnp.einsum('bqghd,bgqnld->bghqnl', qi, kg) * NSA_SCALE
        p = masked_softmax(s.reshape(B, G, HPG, SEL_QBLOCK, nk * SEL_BLOCK),
                           mask.reshape(B, G, 1, SEL_QBLOCK, nk * SEL_BLOCK))
        p = p.reshape(B, G, HPG, SEL_QBLOCK, nk, SEL_BLOCK).astype(vg.dtype)
        return jnp.einsum('bghqnl,bgqnld->bqghd', p, vg)

    out = lax.map(body, (qb, ib, mb, jnp.arange(nq)))
    return jnp.moveaxis(out, 0, 1).reshape(B, S, G, HPG, dh)


def nsa_window(q, k, v):
    B, S, G, HPG, dh = q.shape
    nq = S // WIN_QBLOCK
    span = WIN_QBLOCK + WINDOW
    kp = jnp.pad(k, ((0, 0), (WINDOW, 0), (0, 0), (0, 0)))
    vp = jnp.pad(v, ((0, 0), (WINDOW, 0), (0, 0), (0, 0)))
    qb = jnp.moveaxis(q.reshape(B, nq, WIN_QBLOCK, G, HPG, dh), 1, 0)

    def body(args):
        qi, blk = args
        start = blk * WIN_QBLOCK
        ks = lax.dynamic_slice_in_dim(kp, start, span, axis=1)
        vs = lax.dynamic_slice_in_dim(vp, start, span, axis=1)
        tpos = start + jnp.arange(WIN_QBLOCK)
        kpos = start - WINDOW + jnp.arange(span)
        d = tpos[:, None] - kpos[None, :]
        mask = (d >= 0) & (d < WINDOW) & (kpos[None, :] >= 0)
        s = jnp.einsum('bqghd,bkgd->bghqk', qi, ks) * NSA_SCALE
        p = masked_softmax(s, mask).astype(vs.dtype)
        return jnp.einsum('bghqk,bkgd->bqghd', p, vs)

    out = lax.map(body, (qb, jnp.arange(nq)))
    return jnp.moveaxis(out, 0, 1).reshape(B, S, G, HPG, dh)


def causal_conv(x, w, b):
    C = x.shape[-1]
    y = lax.conv_general_dilated(x, w[:, None, :], window_strides=(1,), padding=[(CONV_WIDTH - 1, 0)],
                                 dimension_numbers=('NWC', 'WIO', 'NWC'), feature_group_count=C)
    return y + b


def mlstm_chunkwise(q, k, v, i_pre, logf):
    B, S, H, dh = q.shape
    L = ML_CHUNK
    nc = S // L
    q = q.astype(jnp.float32)
    k = k.astype(jnp.float32) * (dh ** -0.5)
    v = v.astype(jnp.float32)

    def chunks(a):
        return jnp.moveaxis(a.reshape(B, nc, L, *a.shape[2:]), 1, 0)

    tril = jnp.tril(jnp.ones((L, L), dtype=bool))

    def step(carry, xs):
        C, n, m = carry
        qj, kj, vj, ij, fj = xs
        qh = qj.transpose(0, 2, 1, 3)
        kh = kj.transpose(0, 2, 1, 3)
        vh = vj.transpose(0, 2, 1, 3)
        b = jnp.cumsum(fj, axis=1).transpose(0, 2, 1)
        ih = ij.transpose(0, 2, 1)
        dlog = jnp.where(tril, b[..., :, None] - b[..., None, :] + ih[..., None, :], -jnp.inf)
        inter = b + m[..., None]
        m_t = jnp.maximum(inter, jnp.max(dlog, axis=-1))
        dw = jnp.exp(dlog - m_t[..., None])
        iw = jnp.exp(inter - m_t)
        sqk = jnp.einsum('bhld,bhsd->bhls', qh, kh) * dw
        num = jnp.einsum('bhls,bhsd->bhld', sqk, vh) + iw[..., None] * jnp.einsum('bhed,bhld->bhle', C, qh)
        den = jnp.sum(sqk, axis=-1) + iw * jnp.einsum('bhd,bhld->bhl', n, qh)
        h = num / jnp.maximum(jnp.abs(den), jnp.exp(-m_t))[..., None]
        b_end = b[..., -1]
        wlog = b_end[..., None] - b + ih
        m_new = jnp.maximum(b_end + m, jnp.max(wlog, axis=-1))
        ws = jnp.exp(wlog - m_new[..., None])
        decay = jnp.exp(b_end + m - m_new)
        C_new = decay[..., None, None] * C + jnp.einsum('bhs,bhse,bhsd->bhed', ws, vh, kh)
        n_new = decay[..., None] * n + jnp.einsum('bhs,bhsd->bhd', ws, kh)
        return (C_new, n_new, m_new), h

    init = (jnp.zeros((B, H, dh, dh), jnp.float32), jnp.zeros((B, H, dh), jnp.float32), jnp.zeros((B, H), jnp.float32))
    _, hs = lax.scan(step, init, (chunks(q), chunks(k), chunks(v), chunks(i_pre), chunks(logf)))
    return hs.transpose(1, 0, 3, 2, 4).reshape(B, S, H, dh)


def memory_xattn(q, mem, g, w_kv):
    B, S, _ = q.shape
    M = mem.shape[1]
    kv = rmsnorm(mem, g) @ w_kv
    k = kv[..., :MEM_W].reshape(B, M, MEM_HEADS, MEM_DH)
    v = kv[..., MEM_W:].reshape(B, M, MEM_HEADS, MEM_DH)
    qh = q.reshape(B, S, MEM_HEADS, MEM_DH)
    s = jnp.einsum('bshd,bmhd->bhsm', qh, k).astype(jnp.float32) * MEM_SCALE
    p = jax.nn.softmax(s, axis=-1).astype(v.dtype)
    return jnp.einsum('bhsm,bmhd->bshd', p, v).reshape(B, S, MEM_W)


def setup_inputs(seed: int = 0) -> dict:
    key = jax.random.key(seed)
    ks = jax.random.split(key, 32)
    L = DEPTH

    def nrm(k, shape, scale):
        return jax.random.normal(k, shape, jnp.float32) * scale

    def gain(k, shape):
        return 1.0 + 0.05 * jax.random.normal(k, shape, jnp.float32)

    gate_b = jnp.concatenate([
        0.1 * jax.random.normal(ks[10], (L, ML_HEADS), jnp.float32),
        jnp.linspace(3.0, 6.0, ML_HEADS, dtype=jnp.float32)[None, :] + 0.1 * jax.random.normal(ks[11], (L, ML_HEADS), jnp.float32),
    ], axis=-1)
    return {
        "x": nrm(ks[0], (BATCH, SEQ, D_MODEL), 1.0),
        "mem": nrm(ks[1], (BATCH, MEM_LEN, D_MODEL), 1.0),
        "g_pre_mix": gain(ks[2], (L, D_MODEL)),
        "w_in": nrm(ks[3], (L, D_MODEL, IN_WIDTH), D_MODEL ** -0.5),
        "cmp_pe_k": nrm(ks[4], (L, CMP_BLOCK, NSA_DH), 0.1),
        "cmp_w1_k": nrm(ks[5], (L, CMP_BLOCK, NSA_DH, CMP_HIDDEN), (CMP_BLOCK * NSA_DH) ** -0.5),
        "cmp_w2_k": nrm(ks[6], (L, CMP_HIDDEN, NSA_DH), CMP_HIDDEN ** -0.5),
        "cmp_pe_v": nrm(ks[7], (L, CMP_BLOCK, NSA_DH), 0.1),
        "cmp_w1_v": nrm(ks[8], (L, CMP_BLOCK, NSA_DH, CMP_HIDDEN), (CMP_BLOCK * NSA_DH) ** -0.5),
        "cmp_w2_v": nrm(ks[9], (L, CMP_HIDDEN, NSA_DH), CMP_HIDDEN ** -0.5),
        "ml_conv_w": nrm(ks[12], (L, CONV_WIDTH, 2 * ML_W), CONV_WIDTH ** -0.5),
        "ml_conv_b": nrm(ks[13], (L, 2 * ML_W), 0.02),
        "ml_gate_b": gate_b,
        "ml_head_g": gain(ks[14], (L, ML_W)),
        "g_mem": gain(ks[15], (L, D_MODEL)),
        "w_mem_kv": nrm(ks[16], (L, D_MODEL, 2 * MEM_W), D_MODEL ** -0.5),
        "w_proj_nsa": nrm(ks[17], (L, NSA_Q, D_MODEL), NSA_Q ** -0.5),
        "w_proj_ml": nrm(ks[18], (L, ML_W, D_MODEL), ML_W ** -0.5),
        "w_proj_mem": nrm(ks[19], (L, MEM_W, D_MODEL), MEM_W ** -0.5),
        "w_out": nrm(ks[20], (L, D_MODEL, D_MODEL), D_MODEL ** -0.5),
        "g_post_mix": gain(ks[21], (L, D_MODEL)),
        "g_pre_ffn": gain(ks[22], (L, D_MODEL)),
        "w_ffn_in": nrm(ks[23], (L, D_MODEL, 2 * D_FF), D_MODEL ** -0.5),
        "w_ffn_down": nrm(ks[24], (L, D_FF, D_MODEL), D_FF ** -0.5),
        "g_post_ffn": gain(ks[25], (L, D_MODEL)),
    }


def reference(x, mem, g_pre_mix, w_in, cmp_pe_k, cmp_w1_k, cmp_w2_k, cmp_pe_v, cmp_w1_v, cmp_w2_v,
              ml_conv_w, ml_conv_b, ml_gate_b, ml_head_g, g_mem, w_mem_kv, w_proj_nsa, w_proj_ml,
              w_proj_mem, w_out, g_post_mix, g_pre_ffn, w_ffn_in, w_ffn_down, g_post_ffn):
    B, S, _ = x.shape
    offsets = np.cumsum(IN_SPLITS)[:-1].tolist()
    for l in range(DEPTH):
        h = rmsnorm(x, g_pre_mix[l])
        z = h @ w_in[l]
        q_nsa, kv_nsa, g_nsa, qkv_ml, if_ml, o_ml, q_mem, g_merge = jnp.split(z, offsets, axis=-1)

        q = q_nsa.reshape(B, S, NSA_GROUPS, NSA_HPG, NSA_DH)
        kv = kv_nsa.reshape(B, S, 6, NSA_GROUPS, NSA_DH)
        o_cmp, p_cmp = nsa_compressed(q, kv[:, :, 0], kv[:, :, 1], cmp_pe_k[l], cmp_w1_k[l], cmp_w2_k[l],
                                      cmp_pe_v[l], cmp_w1_v[l], cmp_w2_v[l])
        top_i, top_ok = nsa_select(p_cmp, S)
        o_slc = nsa_selected(q, kv[:, :, 2], kv[:, :, 3], top_i, top_ok)
        o_win = nsa_window(q, kv[:, :, 4], kv[:, :, 5])
        gb = jax.nn.sigmoid(g_nsa.reshape(B, S, NSA_GROUPS, NSA_HPG, 3))
        y_nsa = (gb[..., 0:1] * o_cmp + gb[..., 1:2] * o_slc + gb[..., 2:3] * o_win).reshape(B, S, NSA_Q)

        qk = jax.nn.silu(causal_conv(qkv_ml[..., :2 * ML_W], ml_conv_w[l], ml_conv_b[l]))
        q_m = qk[..., :ML_W].reshape(B, S, ML_HEADS, ML_DH)
        k_m = qk[..., ML_W:].reshape(B, S, ML_HEADS, ML_DH)
        v_m = qkv_ml[..., 2 * ML_W:].reshape(B, S, ML_HEADS, ML_DH)
        gates = if_ml.astype(jnp.float32) + ml_gate_b[l].astype(jnp.float32)
        i_pre = gates[..., :ML_HEADS]
        logf = jax.nn.log_sigmoid(gates[..., ML_HEADS:])
        h_ml = mlstm_chunkwise(q_m, k_m, v_m, i_pre, logf)
        h_ml = h_ml * lax.rsqrt(jnp.mean(h_ml * h_ml, axis=-1, keepdims=True) + EPS)
        h_ml = h_ml.reshape(B, S, ML_W) * ml_head_g[l].astype(jnp.float32)
        y_ml = (jax.nn.sigmoid(o_ml.astype(jnp.float32)) * h_ml).astype(x.dtype)

        y_mem = memory_xattn(q_mem, mem, g_mem[l], w_mem_kv[l])

        gm = jax.nn.sigmoid(g_merge.reshape(B, S, N_BRANCH, D_MODEL))
        y = (gm[:, :, 0] * (y_nsa @ w_proj_nsa[l]) + gm[:, :, 1] * (y_ml @ w_proj_ml[l])
             + gm[:, :, 2] * (y_mem @ w_proj_mem[l]))
        x = x + rmsnorm(y @ w_out[l], g_post_mix[l])

        h = rmsnorm(x, g_pre_ffn[l])
        gu = h @ w_ffn_in[l]
        gate, up = gu[..., :D_FF], gu[..., D_FF:]
        x = x + rmsnorm((jax.nn.silu(gate) * up) @ w_ffn_down[l], g_post_ffn[l])
    return x
```

```python
import functools

import jax
import jax.numpy as jnp
from jax import lax
from jax.experimental import pallas as pl
from jax.experimental.pallas import tpu as pltpu

F32 = jnp.float32
BF16 = jnp.bfloat16

D_MODEL = 1024
EPS = 1e-6
NEG_INF = -1e30
BIG = 1e30

NSA_HEADS = 16
NSA_GROUPS = 4
NSA_HPG = NSA_HEADS // NSA_GROUPS
NSA_DH = 64
NSA_SCALE = NSA_DH ** -0.5
CMP_BLOCK = 32
CMP_STRIDE = 16
CMP_HIDDEN = 128
SEL_BLOCK = 64
SEL_TOPK = 16
N_LOCAL_FORCED = 2
WINDOW = 512

ML_HEADS = 4
ML_DH = 128
ML_CHUNK = 64
CONV_WIDTH = 4

MEM_HEADS = 4
MEM_DH = 128
MEM_SCALE = MEM_DH ** -0.5

D_FF = -(-8 * D_MODEL // (3 * 256)) * 256

NSA_Q = NSA_HEADS * NSA_DH
NSA_KV = NSA_GROUPS * NSA_DH
ML_W = ML_HEADS * ML_DH
MEM_W = MEM_HEADS * MEM_DH

_OFF_Q = 0
_OFF_KV = _OFF_Q + NSA_Q
_OFF_GNSA = _OFF_KV + 6 * NSA_KV
_OFF_QKVML = _OFF_GNSA + 3 * NSA_HEADS
_OFF_IF = _OFF_QKVML + 3 * ML_W
_OFF_OML = _OFF_IF + 2 * ML_HEADS
_OFF_QMEM = _OFF_OML + ML_W
_OFF_GMERGE = _OFF_QMEM + MEM_W
_IN_WIDTH = _OFF_GMERGE + 3 * D_MODEL

ZB_Q = 0
ZB_KV = ZB_Q + NSA_Q
ZB_QKVML = ZB_KV + 6 * NSA_KV
ZB_OML = ZB_QKVML + 3 * ML_W
ZB_QMEM = ZB_OML + ML_W
ZB_GMERGE = ZB_QMEM + MEM_W
ZB_WIDTH = ZB_GMERGE + 3 * D_MODEL
ZG_WIDTH = 128
ZG_GNSA = 0
ZG_IF = 3 * NSA_HEADS

LANE = 128
VMEM_LIMIT = 48 * 1024 * 1024

_NT = (((1,), (1,)), ((), ()))
_TN = (((0,), (0,)), ((), ()))


def _cparams(*sem):
    return pltpu.CompilerParams(dimension_semantics=sem, vmem_limit_bytes=VMEM_LIMIT)


def _rms(x, g):
    return x * lax.rsqrt(jnp.mean(x * x, axis=-1, keepdims=True) + EPS) * g


def _sigmoid(x):
    return 1.0 / (1.0 + jnp.exp(-x))


def _silu(x):
    return x * _sigmoid(x)


def _log_sigmoid(x):
    return jnp.minimum(x, 0.0) - jnp.log(1.0 + jnp.exp(-jnp.abs(x)))


def _inproj_kernel(x_ref, g_ref, w_ref, wg_ref, zb_ref, zg_ref, h_ref):
    @pl.when(pl.program_id(1) == 0)
    def _():
        hb = _rms(x_ref[...], g_ref[...]).astype(BF16)
        h_ref[...] = hb
        zg_ref[...] = jnp.dot(hb, wg_ref[...], preferred_element_type=F32)

    zb_ref[...] = jnp.dot(h_ref[...], w_ref[...], preferred_element_type=F32).astype(BF16)


def _inproj(x2, g, wb, wg, tm=1024, tn=1024):
    t = x2.shape[0]
    return pl.pallas_call(
        _inproj_kernel,
        grid=(t // tm, ZB_WIDTH // tn),
        in_specs=[
            pl.BlockSpec((tm, D_MODEL), lambda i, j: (i, 0)),
            pl.BlockSpec((1, D_MODEL), lambda i, j: (0, 0)),
            pl.BlockSpec((D_MODEL, tn), lambda i, j: (0, j)),
            pl.BlockSpec((D_MODEL, ZG_WIDTH), lambda i, j: (0, 0)),
        ],
        out_specs=[
            pl.BlockSpec((tm, tn), lambda i, j: (i, j)),
            pl.BlockSpec((tm, ZG_WIDTH), lambda i, j: (i, 0)),
        ],
        out_shape=[
            jax.ShapeDtypeStruct((t, ZB_WIDTH), BF16),
            jax.ShapeDtypeStruct((t, ZG_WIDTH), F32),
        ],
        scratch_shapes=[pltpu.VMEM((tm, D_MODEL), BF16)],
        compiler_params=_cparams("parallel", "arbitrary"),
        name="inproj",
    )(x2, g, wb, wg)


def _compress_kernel(a_ref, pe_ref, w1_ref, w2_ref, out_ref):
    half = (CMP_BLOCK // 2) * NSA_DH
    for kind in range(2):
        w1a = w1_ref[kind, 0:half, :]
        w1b = w1_ref[kind, half:2 * half, :]
        pe = pe_ref[kind]
        c = (jnp.dot(pe[:, 0:half], w1a, preferred_element_type=F32)
             + jnp.dot(pe[:, half:2 * half], w1b, preferred_element_type=F32))[0:1, :]
        w2 = w2_ref[kind]
        for g in range(NSA_GROUPS):
            a = a_ref[kind * NSA_GROUPS + g]
            p = jnp.dot(a, w1a, preferred_element_type=F32)
            q = jnp.dot(a, w1b, preferred_element_type=F32)
            n_rows = q.shape[0]
            qs = pltpu.roll(q, shift=n_rows - 1, axis=0)
            hid = _silu(p + qs + c)
            o = jnp.dot(hid.astype(BF16), w2, preferred_element_type=F32)
            col = (kind * NSA_GROUPS + g) * NSA_DH
            out_ref[:, col:col + NSA_DH] = o


def _compress(a, pe, w1, w2):
    b, _, n_rows, half = a.shape
    return pl.pallas_call(
        _compress_kernel,
        grid=(b,),
        in_specs=[
            pl.BlockSpec((None, 2 * NSA_GROUPS, n_rows, half), lambda i: (i, 0, 0, 0)),
            pl.BlockSpec((2, 8, 2 * half), lambda i: (0, 0, 0)),
            pl.BlockSpec((2, 2 * half, CMP_HIDDEN), lambda i: (0, 0, 0)),
            pl.BlockSpec((2, CMP_HIDDEN, NSA_DH), lambda i: (0, 0, 0)),
        ],
        out_specs=pl.BlockSpec((None, n_rows, 2 * NSA_KV), lambda i: (i, 0, 0)),
        out_shape=jax.ShapeDtypeStruct((b, n_rows, 2 * NSA_KV), F32),
        compiler_params=_cparams("parallel"),
        name="compress",
    )(a, pe, w1, w2)


def _cmpsel_kernel(q_ref, kvc_ref, ocmp_ref, sel_ref, *, tq, n_sel, topk):
    nc = kvc_ref.shape[0]
    t0 = pl.program_id(1) * tq
    q = q_ref[...]

    n_i = lax.broadcasted_iota(jnp.int32, (tq, nc), 1)
    t_i = t0 + lax.broadcasted_iota(jnp.int32, (tq, nc), 0)
    mask = (n_i * CMP_STRIDE + (CMP_BLOCK - 1)) <= t_i
    n_t = lax.broadcasted_iota(jnp.int32, (nc, tq), 0)
    t_t = t0 + lax.broadcasted_iota(jnp.int32, (nc, tq), 1)
    mask_t = (n_t * CMP_STRIDE + (CMP_BLOCK - 1)) <= t_t

    j_m = lax.broadcasted_iota(jnp.int32, (n_sel, nc), 0) * SEL_BLOCK
    c_m = lax.broadcasted_iota(jnp.int32, (n_sel, nc), 1) * CMP_STRIDE
    ov = jnp.minimum(c_m + CMP_BLOCK, j_m + SEL_BLOCK) - jnp.maximum(c_m, j_m)
    map_t = jnp.maximum(ov, 0).astype(F32) * (1.0 / CMP_BLOCK)

    j_s = lax.broadcasted_iota(jnp.int32, (n_sel, tq), 0)
    qblk = (t0 + lax.broadcasted_iota(jnp.int32, (n_sel, tq), 1)) // SEL_BLOCK
    rel = qblk - j_s
    causal = rel >= 0
    forced = causal & ((j_s == 0) | (rel < N_LOCAL_FORCED))

    sel_rows = []
    for g in range(NSA_GROUPS):
        kc = kvc_ref[:, g * NSA_DH:(g + 1) * NSA_DH].astype(BF16)
        vc = kvc_ref[:, NSA_KV + g * NSA_DH:NSA_KV + (g + 1) * NSA_DH].astype(BF16)
        psum_t = jnp.zeros((nc, tq), F32)
        for h in range(NSA_HPG):
            hh = g * NSA_HPG + h
            qh = q[:, hh * NSA_DH:(hh + 1) * NSA_DH]
            s = lax.dot_general(qh, kc, _NT, preferred_element_type=F32)
            s = jnp.where(mask, s, NEG_INF)
            e = jnp.where(mask, jnp.exp(s - jnp.max(s, axis=-1, keepdims=True)), 0.0)
            l = jnp.sum(e, axis=-1, keepdims=True)
            p = e / jnp.where(l > 0.0, l, 1.0)
            o = jnp.dot(p.astype(BF16), vc, preferred_element_type=F32)
            ocmp_ref[:, hh * NSA_DH:(hh + 1) * NSA_DH] = o.astype(BF16)
            st = lax.dot_general(kc, qh, _NT, preferred_element_type=F32)
            st = jnp.where(mask_t, st, NEG_INF)
            et = jnp.where(mask_t, jnp.exp(st - jnp.max(st, axis=0, keepdims=True)), 0.0)
            lt = jnp.sum(et, axis=0, keepdims=True)
            psum_t = psum_t + et / jnp.where(lt > 0.0, lt, 1.0)
        imp_t = jnp.dot(map_t, psum_t, preferred_element_type=F32,
                        precision=lax.Precision.HIGHEST)
        score = jnp.where(forced, BIG, jnp.where(causal, imp_t, NEG_INF))
        rank = jnp.zeros((n_sel, tq), jnp.int32)
        for jp in range(n_sel):
            row = score[jp:jp + 1, :]
            before = (row > score) | ((row == score) & (j_s > jp))
            rank = rank + before.astype(jnp.int32)
        chosen = (rank < topk) & (score > 0.5 * NEG_INF)
        sel_rows.append(jnp.where(chosen, 1.0, 0.0).astype(BF16))
    sel_t = jnp.concatenate(sel_rows, axis=0)
    eye = (lax.broadcasted_iota(jnp.int32, (tq, tq), 0)
           == lax.broadcasted_iota(jnp.int32, (tq, tq), 1))
    eye = jnp.where(eye, 1.0, 0.0).astype(BF16)
    sel_ref[...] = lax.dot_general(eye, sel_t, _NT, preferred_element_type=F32).astype(BF16)


def _cmpsel(zb3, kvc, tq=256):
    b, s, _ = zb3.shape
    n_sel = s // SEL_BLOCK
    topk = min(SEL_TOPK, n_sel)
    nc = kvc.shape[1]
    kern = functools.partial(_cmpsel_kernel, tq=tq, n_sel=n_sel, topk=topk)
    return pl.pallas_call(
        kern,
        grid=(b, s // tq),
        in_specs=[
            pl.BlockSpec((None, tq, NSA_Q), lambda i, j: (i, j, ZB_Q // NSA_Q)),
            pl.BlockSpec((None, nc, 2 * NSA_KV), lambda i, j: (i, 0, 0)),
        ],
        out_specs=[
            pl.BlockSpec((None, tq, NSA_Q), lambda i, j: (i, j, 0)),
            pl.BlockSpec((None, tq, NSA_GROUPS * n_sel), lambda i, j: (i, j, 0)),
        ],
        out_shape=[
            jax.ShapeDtypeStruct((b, s, NSA_Q), BF16),
            jax.ShapeDtypeStruct((b, s, NSA_GROUPS * n_sel), BF16),
        ],
        compiler_params=_cparams("parallel", "parallel"),
        name="cmpsel",
    )(zb3, kvc)


def _nsa_kernel(q_ref, ks_ref, vs_ref, kw_ref, vw_ref, sel_ref, ocmp_ref, zg_ref, y_ref,
                *, tq, n_sel):
    tk = tq
    i = pl.program_id(1)
    t0 = i * tq
    tpos = t0 + lax.broadcasted_iota(jnp.int32, (tq, tk), 0)
    kloc = lax.broadcasted_iota(jnp.int32, (tq, tk), 1)
    blk_of_key = lax.broadcasted_iota(jnp.int32, (n_sel, tk), 1) // SEL_BLOCK
    blk_row = lax.broadcasted_iota(jnp.int32, (n_sel, tk), 0)
    gates = _sigmoid(zg_ref[...])
    sel = sel_ref[...]

    def attend(carry, qs, k, v, mask):
        out = []
        for h in range(NSA_HPG):
            m, l, acc = carry[h]
            s = lax.dot_general(qs[h], k, _NT, preferred_element_type=F32)
            s = jnp.where(mask, s, NEG_INF)
            m_new = jnp.maximum(m, jnp.max(s, axis=-1, keepdims=True))
            alpha = jnp.exp(m - m_new)
            p = jnp.where(mask, jnp.exp(s - m_new), 0.0)
            l = alpha * l + jnp.sum(p, axis=-1, keepdims=True)
            acc = alpha * acc + jnp.dot(p.astype(BF16), v, preferred_element_type=F32)
            out.append((m_new, l, acc))
        return tuple(out)

    def init():
        return tuple((jnp.full((tq, 1), NEG_INF, F32), jnp.zeros((tq, 1), F32),
                      jnp.zeros((tq, NSA_DH), F32)) for _ in range(NSA_HPG))

    for g in range(NSA_GROUPS):
        c0 = g * NSA_DH
        qs = [q_ref[:, (g * NSA_HPG + h) * NSA_DH:(g * NSA_HPG + h + 1) * NSA_DH]
              for h in range(NSA_HPG)]
        selg = sel[:, g * n_sel:(g + 1) * n_sel]

        def slc_body(kt, carry):
            r0 = pl.multiple_of(kt * tk, tk)
            k = ks_ref[pl.ds(r0, tk), c0:c0 + NSA_DH]
            v = vs_ref[pl.ds(r0, tk), c0:c0 + NSA_DH]
            expand = jnp.where(blk_row == kt * (tk // SEL_BLOCK) + blk_of_key, 1.0, 0.0).astype(BF16)
            picked = jnp.dot(selg, expand, preferred_element_type=F32)
            mask = (picked > 0.5) & ((r0 + kloc) <= tpos)
            return attend(carry, qs, k, v, mask)

        slc = lax.fori_loop(0, i + 1, slc_body, init())

        def win_body(kt, carry):
            r0 = pl.multiple_of(kt * tk, tk)
            k = kw_ref[pl.ds(r0, tk), c0:c0 + NSA_DH]
            v = vw_ref[pl.ds(r0, tk), c0:c0 + NSA_DH]
            d = tpos - (r0 + kloc)
            mask = (d >= 0) & (d < WINDOW)
            return attend(carry, qs, k, v, mask)

        first = jnp.maximum(i - (WINDOW // tk), 0)
        win = lax.fori_loop(first, i + 1, win_body, init())

        for h in range(NSA_HPG):
            hh = g * NSA_HPG + h
            o_slc = slc[h][2] / slc[h][1]
            o_win = win[h][2] / win[h][1]
            o_cmp = ocmp_ref[:, hh * NSA_DH:(hh + 1) * NSA_DH].astype(F32)
            gc = ZG_GNSA + 3 * hh
            y = (gates[:, gc:gc + 1] * o_cmp + gates[:, gc + 1:gc + 2] * o_slc
                 + gates[:, gc + 2:gc + 3] * o_win)
            y_ref[:, hh * NSA_DH:(hh + 1) * NSA_DH] = y.astype(BF16)


def _nsa(zb3, sel, ocmp, zg3, tq=256):
    b, s, _ = zb3.shape
    n_sel = s // SEL_BLOCK
    assert WINDOW % tq == 0 and tq % SEL_BLOCK == 0
    kern = functools.partial(_nsa_kernel, tq=tq, n_sel=n_sel)
    kvb = ZB_KV // NSA_KV

    def kv_spec(k):
        return pl.BlockSpec((None, s, NSA_KV), lambda bi, j, k=k: (bi, 0, kvb + k))

    return pl.pallas_call(
        kern,
        grid=(b, s // tq),
        in_specs=[
            pl.BlockSpec((None, tq, NSA_Q), lambda bi, j: (bi, j, ZB_Q // NSA_Q)),
            kv_spec(2), kv_spec(3), kv_spec(4), kv_spec(5),
            pl.BlockSpec((None, tq, NSA_GROUPS * n_sel), lambda bi, j: (bi, j, 0)),
            pl.BlockSpec((None, tq, NSA_Q), lambda bi, j: (bi, j, 0)),
            pl.BlockSpec((None, tq, ZG_WIDTH), lambda bi, j: (bi, j, 0)),
        ],
        out_specs=pl.BlockSpec((None, tq, NSA_Q), lambda bi, j: (bi, j, 0)),
        out_shape=jax.ShapeDtypeStruct((b, s, NSA_Q), BF16),
        compiler_params=_cparams("parallel", "arbitrary"),
        name="nsa",
    )(zb3, zb3, zb3, zb3, zb3, sel, ocmp, zg3)


def _mlstm_kernel(q_ref, k_ref, v_ref, o_ref, zg_ref, cw_ref, cb_ref, gb_ref, hg_ref, y_ref,
                  xbuf, c_st, n_st, m_st):
    L = ML_CHUNK
    pad = 8
    @pl.when(pl.program_id(1) == 0)
    def _():
        xbuf[0:pad, :] = jnp.zeros((pad, 2 * ML_W), F32)
        c_st[...] = jnp.zeros_like(c_st)
        n_st[...] = jnp.zeros_like(n_st)
        m_st[...] = jnp.zeros_like(m_st)

    xbuf[pad:pad + L, 0:ML_W] = q_ref[...].astype(F32)
    xbuf[pad:pad + L, ML_W:2 * ML_W] = k_ref[...].astype(F32)
    conv = jnp.zeros((L, 2 * ML_W), F32) + cb_ref[...]
    for j in range(CONV_WIDTH):
        conv = conv + xbuf[pl.ds(pad - (CONV_WIDTH - 1) + j, L), :] * cw_ref[j:j + 1, :]
    xbuf[0:pad, :] = xbuf[L:L + pad, :]
    qk = _silu(conv)

    gates = zg_ref[...] + gb_ref[...]
    logf = _log_sigmoid(gates)
    row = lax.broadcasted_iota(jnp.int32, (L, L), 0)
    col = lax.broadcasted_iota(jnp.int32, (L, L), 1)
    tril = row >= col
    eye = row == col
    bcum = jnp.dot(jnp.where(tril, 1.0, 0.0), logf, preferred_element_type=F32,
                   precision=lax.Precision.HIGHEST)

    for h in range(ML_HEADS):
        q = qk[:, h * ML_DH:(h + 1) * ML_DH]
        k = qk[:, ML_W + h * ML_DH:ML_W + (h + 1) * ML_DH] * (ML_DH ** -0.5)
        v = v_ref[:, h * ML_DH:(h + 1) * ML_DH].astype(F32)
        i_col = gates[:, ZG_IF + h:ZG_IF + h + 1]
        b_col = bcum[:, ZG_IF + ML_HEADS + h:ZG_IF + ML_HEADS + h + 1]
        r_col = i_col - b_col
        r_row = jnp.sum(jnp.where(eye, r_col, 0.0), axis=0, keepdims=True)
        dlog = jnp.where(tril, b_col + r_row, -jnp.inf)
        m_prev = m_st[h:h + 1, 0:1]
        inter = b_col + m_prev
        m_t = jnp.maximum(inter, jnp.max(dlog, axis=-1, keepdims=True))
        dw = jnp.exp(dlog - m_t)
        iw = jnp.exp(inter - m_t)
        c_old = c_st[h]
        n_old = n_st[h:h + 1, :]
        sqk = lax.dot_general(q, k, _NT, preferred_element_type=F32) * dw
        cq = lax.dot_general(q, c_old, _NT, preferred_element_type=F32)
        num = jnp.dot(sqk, v, preferred_element_type=F32) + iw * cq
        den = jnp.sum(sqk, axis=-1, keepdims=True) + iw * jnp.sum(q * n_old, axis=-1, keepdims=True)
        hs = num / jnp.maximum(jnp.abs(den), jnp.exp(-m_t))
        b_end = b_col[L - 1:L, :]
        wlog = b_end - b_col + i_col
        m_new = jnp.maximum(b_end + m_prev, jnp.max(wlog, axis=0, keepdims=True))
        ws = jnp.exp(wlog - m_new)
        decay = jnp.exp(b_end + m_prev - m_new)
        kw = k * ws
        c_st[h] = decay * c_old + lax.dot_general(v, kw, _TN, preferred_element_type=F32)
        n_st[h:h + 1, :] = decay * n_old + jnp.sum(kw, axis=0, keepdims=True)
        m_st[h:h + 1, :] = jnp.zeros((1, LANE), F32) + m_new
        hn = hs * lax.rsqrt(jnp.mean(hs * hs, axis=-1, keepdims=True) + EPS)
        hn = hn * hg_ref[:, h * ML_DH:(h + 1) * ML_DH]
        og = _sigmoid(o_ref[:, h * ML_DH:(h + 1) * ML_DH].astype(F32))
        y_ref[:, h * ML_DH:(h + 1) * ML_DH] = (og * hn).astype(BF16)


def _mlstm(zb3, zg3, conv_w, conv_b, gate_b, head_g):
    b, s, _ = zb3.shape
    L = ML_CHUNK
    qb = ZB_QKVML // ML_W

    def zspec(k):
        return pl.BlockSpec((None, L, ML_W), lambda bi, c, k=k: (bi, c, k))

    return pl.pallas_call(
        _mlstm_kernel,
        grid=(b, s // L),
        in_specs=[
            zspec(qb), zspec(qb + 1), zspec(qb + 2), zspec(ZB_OML // ML_W),
            pl.BlockSpec((None, L, ZG_WIDTH), lambda bi, c: (bi, c, 0)),
            pl.BlockSpec((CONV_WIDTH, 2 * ML_W), lambda bi, c: (0, 0)),
            pl.BlockSpec((1, 2 * ML_W), lambda bi, c: (0, 0)),
            pl.BlockSpec((1, ZG_WIDTH), lambda bi, c: (0, 0)),
            pl.BlockSpec((1, ML_W), lambda bi, c: (0, 0)),
        ],
        out_specs=pl.BlockSpec((None, L, ML_W), lambda bi, c: (bi, c, 0)),
        out_shape=jax.ShapeDtypeStruct((b, s, ML_W), BF16),
        scratch_shapes=[
            pltpu.VMEM((8 + L, 2 * ML_W), F32),
            pltpu.VMEM((ML_HEADS, ML_DH, ML_DH), F32),
            pltpu.VMEM((8, ML_DH), F32),
            pltpu.VMEM((8, LANE), F32),
        ],
        compiler_params=_cparams("parallel", "arbitrary"),
        name="mlstm",
    )(zb3, zb3, zb3, zb3, zg3, conv_w, conv_b, gate_b, head_g)


def _memkv_kernel(mem_ref, g_ref, w_ref, kv_ref):
    hb = _rms(mem_ref[...], g_ref[...]).astype(BF16)
    kv_ref[...] = jnp.dot(hb, w_ref[...], preferred_element_type=F32).astype(BF16)


def _memkv(mem, g, wb):
    b, m, _ = mem.shape
    return pl.pallas_call(
        _memkv_kernel,
        grid=(b,),
        in_specs=[
            pl.BlockSpec((None, m, D_MODEL), lambda i: (i, 0, 0)),
            pl.BlockSpec((1, D_MODEL), lambda i: (0, 0)),
            pl.BlockSpec((D_MODEL, 2 * MEM_W), lambda i: (0, 0)),
        ],
        out_specs=pl.BlockSpec((None, m, 2 * MEM_W), lambda i: (i, 0, 0)),
        out_shape=jax.ShapeDtypeStruct((b, m, 2 * MEM_W), BF16),
        compiler_params=_cparams("parallel"),
        name="memkv",
    )(mem, g, wb)


def _memattn_kernel(q_ref, kv_ref, y_ref):
    for h in range(MEM_HEADS):
        q = q_ref[:, h * MEM_DH:(h + 1) * MEM_DH]
        k = kv_ref[:, h * MEM_DH:(h + 1) * MEM_DH]
        v = kv_ref[:, MEM_W + h * MEM_DH:MEM_W + (h + 1) * MEM_DH]
        s = lax.dot_general(q, k, _NT, preferred_element_type=F32) * MEM_SCALE
        e = jnp.exp(s - jnp.max(s, axis=-1, keepdims=True))
        p = e / jnp.sum(e, axis=-1, keepdims=True)
        o = jnp.dot(p.astype(BF16), v, preferred_element_type=F32)
        y_ref[:, h * MEM_DH:(h + 1) * MEM_DH] = o.astype(BF16)


def _memattn(zb3, kvm, tq=512):
    b, s, _ = zb3.shape
    m = kvm.shape[1]
    return pl.pallas_call(
        _memattn_kernel,
        grid=(b, s // tq),
        in_specs=[
            pl.BlockSpec((None, tq, MEM_W), lambda i, j: (i, j, ZB_QMEM // MEM_W)),
            pl.BlockSpec((None, m, 2 * MEM_W), lambda i, j: (i, 0, 0)),
        ],
        out_specs=pl.BlockSpec((None, tq, MEM_W), lambda i, j: (i, j, 0)),
        out_shape=jax.ShapeDtypeStruct((b, s, MEM_W), BF16),
        compiler_params=_cparams("parallel", "parallel"),
        name="memattn",
    )(zb3, kvm)


def _merge_kernel(x_ref, yn_ref, yl_ref, ym_ref, g0_ref, g1_ref, g2_ref,
                  wn_ref, wl_ref, wm_ref, wo_ref, gp_ref, out_ref):
    y = _sigmoid(g0_ref[...].astype(F32)) * jnp.dot(yn_ref[...], wn_ref[...], preferred_element_type=F32)
    y = y + _sigmoid(g1_ref[...].astype(F32)) * jnp.dot(yl_ref[...], wl_ref[...], preferred_element_type=F32)
    y = y + _sigmoid(g2_ref[...].astype(F32)) * jnp.dot(ym_ref[...], wm_ref[...], preferred_element_type=F32)
    u = jnp.dot(y.astype(BF16), wo_ref[...], preferred_element_type=F32)
    out_ref[...] = x_ref[...] + _rms(u, gp_ref[...])


def _merge(x2, yn, yl, ym, zb, wn, wl, wm, wo, gp, tm=512):
    t = x2.shape[0]
    gm = ZB_GMERGE // D_MODEL

    def const(shape):
        return pl.BlockSpec(shape, lambda i: (0, 0))

    return pl.pallas_call(
        _merge_kernel,
        grid=(t // tm,),
        in_specs=[
            pl.BlockSpec((tm, D_MODEL), lambda i: (i, 0)),
            pl.BlockSpec((tm, NSA_Q), lambda i: (i, 0)),
            pl.BlockSpec((tm, ML_W), lambda i: (i, 0)),
            pl.BlockSpec((tm, MEM_W), lambda i: (i, 0)),
            pl.BlockSpec((tm, D_MODEL), lambda i: (i, gm)),
            pl.BlockSpec((tm, D_MODEL), lambda i: (i, gm + 1)),
            pl.BlockSpec((tm, D_MODEL), lambda i: (i, gm + 2)),
            const((NSA_Q, D_MODEL)), const((ML_W, D_MODEL)), const((MEM_W, D_MODEL)),
            const((D_MODEL, D_MODEL)), const((1, D_MODEL)),
        ],
        out_specs=pl.BlockSpec((tm, D_MODEL), lambda i: (i, 0)),
        out_shape=jax.ShapeDtypeStruct((t, D_MODEL), F32),
        compiler_params=_cparams("parallel"),
        name="merge",
    )(x2, yn, yl, ym, zb, zb, zb, wn, wl, wm, wo, gp)


def _ffn_kernel(x_ref, gpre_ref, wg_ref, wu_ref, wd_ref, gpost_ref, out_ref, h_ref, acc_ref):
    j = pl.program_id(1)

    @pl.when(j == 0)
    def _():
        h_ref[...] = _rms(x_ref[...], gpre_ref[...]).astype(BF16)
        acc_ref[...] = jnp.zeros_like(acc_ref)

    h = h_ref[...]
    gate = jnp.dot(h, wg_ref[...], preferred_element_type=F32)
    up = jnp.dot(h, wu_ref[...], preferred_element_type=F32)
    act = (_silu(gate) * up).astype(BF16)
    acc_ref[...] += jnp.dot(act, wd_ref[...], preferred_element_type=F32)

    @pl.when(j == pl.num_programs(1) - 1)
    def _():
        out_ref[...] = x_ref[...] + _rms(acc_ref[...], gpost_ref[...])


def _ffn(x2, gpre, w_in, w_down, gpost, tm=512, nf=2):
    t = x2.shape[0]
    tf = D_FF // nf
    assert tf % LANE == 0
    return pl.pallas_call(
        _ffn_kernel,
        grid=(t // tm, nf),
        in_specs=[
            pl.BlockSpec((tm, D_MODEL), lambda i, j: (i, 0)),
            pl.BlockSpec((1, D_MODEL), lambda i, j: (0, 0)),
            pl.BlockSpec((D_MODEL, tf), lambda i, j: (0, j)),
            pl.BlockSpec((D_MODEL, tf), lambda i, j: (0, nf + j)),
            pl.BlockSpec((tf, D_MODEL), lambda i, j: (j, 0)),
            pl.BlockSpec((1, D_MODEL), lambda i, j: (0, 0)),
        ],
        out_specs=pl.BlockSpec((tm, D_MODEL), lambda i, j: (i, 0)),
        out_shape=jax.ShapeDtypeStruct((t, D_MODEL), F32),
        scratch_shapes=[pltpu.VMEM((tm, D_MODEL), BF16), pltpu.VMEM((tm, D_MODEL), F32)],
        compiler_params=_cparams("parallel", "arbitrary"),
        name="ffn",
    )(x2, gpre, w_in, w_in, w_down, gpost)


def _layer(x, mem, p):
    b, s, _ = x.shape
    t = b * s
    x2 = x.reshape(t, D_MODEL)

    w_in = p["w_in"]
    wb = jnp.concatenate([
        w_in[:, _OFF_Q:_OFF_KV] * NSA_SCALE,
        w_in[:, _OFF_KV:_OFF_GNSA],
        w_in[:, _OFF_QKVML:_OFF_IF],
        w_in[:, _OFF_OML:_OFF_QMEM],
        w_in[:, _OFF_QMEM:_OFF_GMERGE],
        w_in[:, _OFF_GMERGE:_IN_WIDTH],
    ], axis=1).astype(BF16)
    wg = jnp.concatenate([
        w_in[:, _OFF_GNSA:_OFF_QKVML],
        w_in[:, _OFF_IF:_OFF_OML],
        jnp.zeros((D_MODEL, ZG_WIDTH - 3 * NSA_HEADS - 2 * ML_HEADS), F32),
    ], axis=1).astype(BF16)
    zb, zg = _inproj(x2, p["g_pre_mix"].reshape(1, D_MODEL), wb, wg)
    zb3 = zb.reshape(b, s, ZB_WIDTH)
    zg3 = zg.reshape(b, s, ZG_WIDTH)

    n_rows = s // CMP_STRIDE
    half = (CMP_BLOCK // 2) * NSA_DH
    a = zb3[:, :, ZB_KV:ZB_KV + 2 * NSA_KV].reshape(b, n_rows, CMP_STRIDE, 2, NSA_GROUPS, NSA_DH)
    a = a.transpose(0, 3, 4, 1, 2, 5).reshape(b, 2 * NSA_GROUPS, n_rows, half)
    pe = jnp.stack([p["cmp_pe_k"], p["cmp_pe_v"]]).reshape(2, 1, 2 * half)
    pe = jnp.pad(pe, ((0, 0), (0, 7), (0, 0))).astype(BF16)
    w1 = jnp.stack([p["cmp_w1_k"], p["cmp_w1_v"]]).reshape(2, 2 * half, CMP_HIDDEN).astype(BF16)
    w2 = jnp.stack([p["cmp_w2_k"], p["cmp_w2_v"]]).astype(BF16)
    kvc = _compress(a, pe, w1, w2)
    ocmp, sel = _cmpsel(zb3, kvc)
    y_nsa = _nsa(zb3, sel, ocmp, zg3)

    gate_b = jnp.zeros((1, ZG_WIDTH), F32).at[0, ZG_IF:ZG_IF + 2 * ML_HEADS].set(p["ml_gate_b"])
    y_ml = _mlstm(zb3, zg3, p["ml_conv_w"], p["ml_conv_b"].reshape(1, 2 * ML_W), gate_b,
                  p["ml_head_g"].reshape(1, ML_W))

    kvm = _memkv(mem, p["g_mem"].reshape(1, D_MODEL), p["w_mem_kv"].astype(BF16))
    y_mem = _memattn(zb3, kvm)

    x1 = _merge(x2, y_nsa.reshape(t, NSA_Q), y_ml.reshape(t, ML_W), y_mem.reshape(t, MEM_W), zb,
                p["w_proj_nsa"].astype(BF16), p["w_proj_ml"].astype(BF16),
                p["w_proj_mem"].astype(BF16), p["w_out"].astype(BF16),
                p["g_post_mix"].reshape(1, D_MODEL))
    x2o = _ffn(x1, p["g_pre_ffn"].reshape(1, D_MODEL), p["w_ffn_in"].astype(BF16),
               p["w_ffn_down"].astype(BF16), p["g_post_ffn"].reshape(1, D_MODEL))
    return x2o.reshape(b, s, D_MODEL)


def kernel(x, mem, g_pre_mix, w_in, cmp_pe_k, cmp_w1_k, cmp_w2_k, cmp_pe_v, cmp_w1_v, cmp_w2_v,
           ml_conv_w, ml_conv_b, ml_gate_b, ml_head_g, g_mem, w_mem_kv, w_proj_nsa, w_proj_ml,
           w_proj_mem, w_out, g_post_mix, g_pre_ffn, w_ffn_in, w_ffn_down, g_post_ffn):
    params = dict(
        g_pre_mix=g_pre_mix, w_in=w_in, cmp_pe_k=cmp_pe_k, cmp_w1_k=cmp_w1_k, cmp_w2_k=cmp_w2_k,
        cmp_pe_v=cmp_pe_v, cmp_w1_v=cmp_w1_v, cmp_w2_v=cmp_w2_v, ml_conv_w=ml_conv_w,
        ml_conv_b=ml_conv_b, ml_gate_b=ml_gate_b, ml_head_g=ml_head_g, g_mem=g_mem,
        w_mem_kv=w_mem_kv, w_proj_nsa=w_proj_nsa, w_proj_ml=w_proj_ml, w_proj_mem=w_proj_mem,
        w_out=w_out, g_post_mix=g_post_mix, g_pre_ffn=g_pre_ffn, w_ffn_in=w_ffn_in,
        w_ffn_down=w_ffn_down, g_post_ffn=g_post_ffn)
    depth = w_in.shape[0]
    for l in range(depth):
        x = _layer(x, mem, {k: v[l] for k, v in params.items()})
    return x
```

```python
import functools
import math

import jax
import jax.numpy as jnp
from jax import lax
from jax.experimental import pallas as pl
from jax.experimental.pallas import tpu as pltpu

F32 = jnp.float32
BF16 = jnp.bfloat16

D_MODEL = 1024
EPS = 1e-6
NEG_INF = -1e30
BIG = 1e30

NSA_HEADS = 16
NSA_GROUPS = 4
NSA_HPG = NSA_HEADS // NSA_GROUPS
NSA_DH = 64
NSA_SCALE = NSA_DH ** -0.5
CMP_BLOCK = 32
CMP_STRIDE = 16
CMP_HIDDEN = 128
SEL_BLOCK = 64
SEL_TOPK = 16
N_LOCAL_FORCED = 2
WINDOW = 512

ML_HEADS = 4
ML_DH = 128
ML_CHUNK = 64
CONV_WIDTH = 4

MEM_HEADS = 4
MEM_DH = 128
MEM_SCALE = MEM_DH ** -0.5

D_FF = -(-8 * D_MODEL // (3 * 256)) * 256

NSA_Q = NSA_HEADS * NSA_DH
NSA_KV = NSA_GROUPS * NSA_DH
ML_W = ML_HEADS * ML_DH
MEM_W = MEM_HEADS * MEM_DH

_OFF_Q = 0
_OFF_KV = _OFF_Q + NSA_Q
_OFF_GNSA = _OFF_KV + 6 * NSA_KV
_OFF_QKVML = _OFF_GNSA + 3 * NSA_HEADS
_OFF_IF = _OFF_QKVML + 3 * ML_W
_OFF_OML = _OFF_IF + 2 * ML_HEADS
_OFF_QMEM = _OFF_OML + ML_W
_OFF_GMERGE = _OFF_QMEM + MEM_W
_IN_WIDTH = _OFF_GMERGE + 3 * D_MODEL

ZB_GMERGE = 0
ZB_KVC = ZB_GMERGE + 3 * D_MODEL
ZB_KSLC = ZB_KVC + 2 * NSA_KV
ZB_KWIN = ZB_KSLC + NSA_KV
ZB_QKVML = ZB_KWIN + NSA_KV
ZB_OML = ZB_QKVML + 3 * ML_W
ZB_QMEM = ZB_OML + ML_W
ZB_WIDTH = ZB_QMEM + MEM_W
ZT_Q = 0
ZT_VSLC = ZT_Q + NSA_Q
ZT_VWIN = ZT_VSLC + NSA_KV
ZT_ROWS = ZT_VWIN + NSA_KV
ZG_WIDTH = 128
ZG_GNSA = 0
ZG_IF = 3 * NSA_HEADS

LANE = 128
VMEM_LIMIT = 48 * 1024 * 1024
LOG2E = math.log2(math.e)

_NT = (((1,), (1,)), ((), ()))
_TN = (((0,), (0,)), ((), ()))


def _cparams(*sem):
    return pltpu.CompilerParams(dimension_semantics=sem, vmem_limit_bytes=VMEM_LIMIT)


def _rms(x, g):
    return x * lax.rsqrt(jnp.mean(x * x, axis=-1, keepdims=True) + EPS) * g


def _sigmoid(x):
    return 1.0 / (1.0 + jnp.exp(-x))


def _silu(x):
    return x * _sigmoid(x)


def _log_sigmoid(x):
    return jnp.minimum(x, 0.0) - jnp.log(1.0 + jnp.exp(-jnp.abs(x)))


def _inproj_kernel(x_ref, g_ref, wt_ref, wn_ref, wg_ref, wgt_ref, zt_ref, zb_ref, zg_ref, zgt_ref,
                   h_ref, *, nt):
    j = pl.program_id(1)

    @pl.when(j == 0)
    def _():
        hb = _rms(x_ref[...], g_ref[...]).astype(BF16)
        h_ref[...] = hb
        zg_ref[...] = jnp.dot(hb, wg_ref[...], preferred_element_type=F32)
        zgt_ref[...] = lax.dot_general(wgt_ref[...], hb, _NT, preferred_element_type=F32)

    @pl.when(j < nt)
    def _():
        zt_ref[...] = lax.dot_general(wt_ref[...], h_ref[...], _NT,
                                      preferred_element_type=F32).astype(BF16)

    @pl.when(j >= nt)
    def _():
        zb_ref[...] = jnp.dot(h_ref[...], wn_ref[...], preferred_element_type=F32).astype(BF16)


def _inproj(x2, g, wt, wn, wg, wgt, b, s, tm=1024, nt=2, nn=4):
    t = x2.shape[0]
    tr = ZT_ROWS // nt
    tn = ZB_WIDTH // nn
    assert tr * nt == ZT_ROWS and tn * nn == ZB_WIDTH and tn % LANE == 0 and s % tm == 0
    spb = s // tm
    kern = functools.partial(_inproj_kernel, nt=nt)
    return pl.pallas_call(
        kern,
        grid=(t // tm, nt + nn),
        in_specs=[
            pl.BlockSpec((tm, D_MODEL), lambda i, j: (i, 0)),
            pl.BlockSpec((1, D_MODEL), lambda i, j: (0, 0)),
            pl.BlockSpec((tr, D_MODEL), lambda i, j: (jnp.minimum(j, nt - 1), 0)),
            pl.BlockSpec((D_MODEL, tn), lambda i, j: (0, jnp.maximum(j - nt, 0))),
            pl.BlockSpec((D_MODEL, ZG_WIDTH), lambda i, j: (0, 0)),
            pl.BlockSpec((ZG_WIDTH, D_MODEL), lambda i, j: (0, 0)),
        ],
        out_specs=[
            pl.BlockSpec((None, tr, tm), lambda i, j: (i // spb, jnp.minimum(j, nt - 1), i % spb)),
            pl.BlockSpec((tm, tn), lambda i, j: (i, jnp.maximum(j - nt, 0))),
            pl.BlockSpec((tm, ZG_WIDTH), lambda i, j: (i, 0)),
            pl.BlockSpec((None, ZG_WIDTH, tm), lambda i, j: (i // spb, 0, i % spb)),
        ],
        out_shape=[
            jax.ShapeDtypeStruct((b, ZT_ROWS, s), BF16),
            jax.ShapeDtypeStruct((t, ZB_WIDTH), BF16),
            jax.ShapeDtypeStruct((t, ZG_WIDTH), F32),
            jax.ShapeDtypeStruct((b, ZG_WIDTH, s), F32),
        ],
        scratch_shapes=[pltpu.VMEM((tm, D_MODEL), BF16)],
        compiler_params=_cparams("parallel", "arbitrary"),
        name="inproj",
    )(x2, g, wt, wn, wg, wgt)


def _compress_kernel(a_ref, pe_ref, w1_ref, w2k_ref, w2vt_ref, kc_ref, vct_ref):
    half = (CMP_BLOCK // 2) * NSA_DH
    for kind in range(2):
        w1a = w1_ref[kind, 0:half, :]
        w1b = w1_ref[kind, half:2 * half, :]
        pe = pe_ref[kind]
        c = (jnp.dot(pe[:, 0:half], w1a, preferred_element_type=F32)
             + jnp.dot(pe[:, half:2 * half], w1b, preferred_element_type=F32))[0:1, :]
        for g in range(NSA_GROUPS):
            a = a_ref[kind * NSA_GROUPS + g]
            p = jnp.dot(a, w1a, preferred_element_type=F32)
            q = jnp.dot(a, w1b, preferred_element_type=F32)
            n_rows = q.shape[0]
            qs = pltpu.roll(q, shift=n_rows - 1, axis=0)
            hid = _silu(p + qs + c).astype(BF16)
            if kind == 0:
                o = jnp.dot(hid, w2k_ref[...], preferred_element_type=F32)
                kc_ref[:, g * NSA_DH:(g + 1) * NSA_DH] = o.astype(BF16)
            else:
                ot = lax.dot_general(w2vt_ref[...], hid, _NT, preferred_element_type=F32)
                vct_ref[g * NSA_DH:(g + 1) * NSA_DH, :] = ot.astype(BF16)


def _compress(a, pe, w1, w2k, w2vt):
    b, _, n_rows, half = a.shape
    return pl.pallas_call(
        _compress_kernel,
        grid=(b,),
        in_specs=[
            pl.BlockSpec((None, 2 * NSA_GROUPS, n_rows, half), lambda i: (i, 0, 0, 0)),
            pl.BlockSpec((2, 8, 2 * half), lambda i: (0, 0, 0)),
            pl.BlockSpec((2, 2 * half, CMP_HIDDEN), lambda i: (0, 0, 0)),
            pl.BlockSpec((CMP_HIDDEN, NSA_DH), lambda i: (0, 0)),
            pl.BlockSpec((NSA_DH, CMP_HIDDEN), lambda i: (0, 0)),
        ],
        out_specs=[
            pl.BlockSpec((None, n_rows, NSA_KV), lambda i: (i, 0, 0)),
            pl.BlockSpec((None, NSA_KV, n_rows), lambda i: (i, 0, 0)),
        ],
        out_shape=[
            jax.ShapeDtypeStruct((b, n_rows, NSA_KV), BF16),
            jax.ShapeDtypeStruct((b, NSA_KV, n_rows), BF16),
        ],
        compiler_params=_cparams("parallel"),
        name="compress",
    )(a, pe, w1, w2k, w2vt)


def _pad_query(qt, g):
    z = jnp.zeros_like(qt)
    return jnp.concatenate([qt, z] if g % 2 == 0 else [z, qt], axis=0)


def _cmpsel_kernel(qt_ref, kc_ref, vct_ref, ocmpt_ref, selb_ref, *, tq, n_sel, topk):
    nc = kc_ref.shape[0]
    t0 = pl.program_id(1) * tq

    n_t = lax.broadcasted_iota(jnp.int32, (nc, tq), 0)
    t_t = t0 + lax.broadcasted_iota(jnp.int32, (nc, tq), 1)
    mask_t = (n_t * CMP_STRIDE + (CMP_BLOCK - 1)) <= t_t

    j_m = lax.broadcasted_iota(jnp.int32, (n_sel, nc), 0) * SEL_BLOCK
    c_m = lax.broadcasted_iota(jnp.int32, (n_sel, nc), 1) * CMP_STRIDE
    ov = jnp.minimum(c_m + CMP_BLOCK, j_m + SEL_BLOCK) - jnp.maximum(c_m, j_m)
    map_t = jnp.maximum(ov, 0).astype(F32) * (1.0 / CMP_BLOCK)

    j_s = lax.broadcasted_iota(jnp.int32, (n_sel, tq), 0)
    qblk = (t0 + lax.broadcasted_iota(jnp.int32, (n_sel, tq), 1)) // SEL_BLOCK
    rel = qblk - j_s
    causal = rel >= 0
    forced = causal & ((j_s == 0) | (rel < N_LOCAL_FORCED))

    for g in range(NSA_GROUPS):
        pair = g // 2
        kcp = kc_ref[:, pair * LANE:(pair + 1) * LANE]
        vct = vct_ref[g * NSA_DH:(g + 1) * NSA_DH, :]
        psum_t = jnp.zeros((nc, tq), F32)
        sts = [jnp.dot(kcp, _pad_query(qt_ref[(g * NSA_HPG + h) * NSA_DH:(g * NSA_HPG + h + 1) * NSA_DH, :], g),
                       preferred_element_type=F32) for h in range(NSA_HPG)]
        for h in range(NSA_HPG):
            hh = g * NSA_HPG + h
            st = jnp.where(mask_t, sts[h], NEG_INF)
            et = jnp.where(mask_t, jnp.exp2(st - jnp.max(st, axis=0, keepdims=True)), 0.0)
            lt = jnp.sum(et, axis=0, keepdims=True)
            pt = et * (1.0 / jnp.where(lt > 0.0, lt, 1.0))
            psum_t = psum_t + pt
            ot = jnp.dot(vct, pt.astype(BF16), preferred_element_type=F32)
            ocmpt_ref[hh * NSA_DH:(hh + 1) * NSA_DH, :] = ot.astype(BF16)
        imp_t = jnp.dot(map_t, psum_t, preferred_element_type=F32,
                        precision=lax.Precision.HIGHEST)
        score = jnp.where(forced, BIG, jnp.where(causal, imp_t, NEG_INF))
        rank = jnp.zeros((n_sel, tq), jnp.int32)
        for jp in range(n_sel):
            row = score[jp:jp + 1, :]
            before = (row > score) | ((row == score) & (j_s > jp))
            rank = rank + before.astype(jnp.int32)
        chosen = (rank < topk) & (score > 0.5 * NEG_INF)
        selb_ref[g * n_sel:(g + 1) * n_sel, :] = jnp.where(chosen, 0.0, NEG_INF)


def _cmpsel(zt, kc, vct, tq=256):
    b, _, s = zt.shape
    n_sel = s // SEL_BLOCK
    topk = min(SEL_TOPK, n_sel)
    nc = kc.shape[1]
    kern = functools.partial(_cmpsel_kernel, tq=tq, n_sel=n_sel, topk=topk)
    return pl.pallas_call(
        kern,
        grid=(b, s // tq),
        in_specs=[
            pl.BlockSpec((None, NSA_Q, tq), lambda i, j: (i, ZT_Q // NSA_Q, j)),
            pl.BlockSpec((None, nc, NSA_KV), lambda i, j: (i, 0, 0)),
            pl.BlockSpec((None, NSA_KV, nc), lambda i, j: (i, 0, 0)),
        ],
        out_specs=[
            pl.BlockSpec((None, NSA_Q, tq), lambda i, j: (i, 0, j)),
            pl.BlockSpec((None, NSA_GROUPS * n_sel, tq), lambda i, j: (i, 0, j)),
        ],
        out_shape=[
            jax.ShapeDtypeStruct((b, NSA_Q, s), BF16),
            jax.ShapeDtypeStruct((b, NSA_GROUPS * n_sel, s), F32),
        ],
        compiler_params=_cparams("parallel", "parallel"),
        name="cmpsel",
    )(zt, kc, vct)


def _nsa_kernel(qt_ref, ks_ref, kw_ref, vst_ref, vwt_ref, selb_ref, ocmpt_ref, gt_ref, y_ref,
                yt_scr, *, tq, n_sel, gpt):
    tk = tq
    nw = WINDOW // tk
    sel_per_tile = tk // SEL_BLOCK
    den_rows = 16
    i = pl.program_id(1)
    d0 = (lax.broadcasted_iota(jnp.int32, (tk, tq), 1)
          - lax.broadcasted_iota(jnp.int32, (tk, tq), 0))
    caus = jnp.where(d0 >= 0, 0.0, NEG_INF)
    lowb = jnp.where(d0 < 0, 0.0, NEG_INF)
    blk_of_key = lax.broadcasted_iota(jnp.int32, (tk, n_sel), 0) // SEL_BLOCK
    blk_col = lax.broadcasted_iota(jnp.int32, (tk, n_sel), 1)
    ones_rows = jnp.ones((den_rows, tk), BF16)
    gates = _sigmoid(gt_ref[...])

    def init():
        return tuple((jnp.full((1, tq), NEG_INF, F32), jnp.zeros((NSA_DH + den_rows, tq), F32))
                     for _ in range(gpt * NSA_HPG))

    for g0 in range(0, NSA_GROUPS, gpt):
        groups = list(range(g0, g0 + gpt))
        qps = [[_pad_query(qt_ref[(g * NSA_HPG + h) * NSA_DH:(g * NSA_HPG + h + 1) * NSA_DH, :], g)
                for h in range(NSA_HPG)] for g in groups]
        selbs = [selb_ref[g * n_sel:(g + 1) * n_sel, :].astype(BF16) for g in groups]

        def step(carry, ktiles, vt_augs, biases):
            sts = []
            for gi, g in enumerate(groups):
                for h in range(NSA_HPG):
                    st = jnp.dot(ktiles[g // 2], qps[gi][h], preferred_element_type=F32)
                    if biases[gi] is not None:
                        st = st + biases[gi]
                    sts.append(st)
            out = []
            for idx, st in enumerate(sts):
                m, acc = carry[idx]
                m_new = jnp.maximum(m, jnp.max(st, axis=0, keepdims=True))
                alpha = jnp.exp2(m - m_new)
                p = jnp.exp2(st - m_new).astype(BF16)
                acc = alpha * acc + jnp.dot(vt_augs[idx // NSA_HPG], p, preferred_element_type=F32)
                out.append((m_new, acc))
            return tuple(out)

        def load(k_ref, vt_ref, kt):
            r0 = pl.multiple_of(kt * tk, tk)
            ktiles = {pair: k_ref[pl.ds(r0, tk), pair * LANE:(pair + 1) * LANE]
                      for pair in sorted({g // 2 for g in groups})}
            vt_augs = [jnp.concatenate([vt_ref[g * NSA_DH:(g + 1) * NSA_DH, pl.ds(r0, tk)], ones_rows],
                                       axis=0) for g in groups]
            return ktiles, vt_augs

        def sel_bias(kt, extra=None):
            expand = jnp.where(blk_col == kt * sel_per_tile + blk_of_key, 1.0, 0.0).astype(BF16)
            out = [jnp.dot(expand, sb, preferred_element_type=F32) for sb in selbs]
            return out if extra is None else [o + extra for o in out]

        def slc_body(kt, carry):
            ktiles, vt_augs = load(ks_ref, vst_ref, kt)
            return step(carry, ktiles, vt_augs, sel_bias(kt))

        slc = lax.fori_loop(0, i, slc_body, init())
        ktiles, vt_augs = load(ks_ref, vst_ref, i)
        slc = step(slc, ktiles, vt_augs, sel_bias(i, caus))

        def win_low(_, carry):
            ktiles, vt_augs = load(kw_ref, vwt_ref, i - nw)
            return step(carry, ktiles, vt_augs, [lowb] * gpt)

        def win_mid(kt, carry):
            ktiles, vt_augs = load(kw_ref, vwt_ref, kt)
            return step(carry, ktiles, vt_augs, [None] * gpt)

        win = lax.fori_loop(0, (i >= nw).astype(jnp.int32), win_low, init())
        win = lax.fori_loop(jnp.maximum(i - nw + 1, 0), i, win_mid, win)
        ktiles, vt_augs = load(kw_ref, vwt_ref, i)
        win = step(win, ktiles, vt_augs, [caus] * gpt)

        for gi, g in enumerate(groups):
            for h in range(NSA_HPG):
                hh = g * NSA_HPG + h
                acc_s = slc[gi * NSA_HPG + h][1]
                acc_w = win[gi * NSA_HPG + h][1]
                o_slc = acc_s[0:NSA_DH] * (1.0 / acc_s[NSA_DH:NSA_DH + 1])
                o_win = acc_w[0:NSA_DH] * (1.0 / acc_w[NSA_DH:NSA_DH + 1])
                o_cmp = ocmpt_ref[hh * NSA_DH:(hh + 1) * NSA_DH, :].astype(F32)
                gc = ZG_GNSA + 3 * hh
                yt_scr[hh * NSA_DH:(hh + 1) * NSA_DH, :] = (
                    gates[gc:gc + 1] * o_cmp + gates[gc + 1:gc + 2] * o_slc
                    + gates[gc + 2:gc + 3] * o_win)

    y_ref[...] = yt_scr[...].T.astype(BF16)


def _nsa(zt, zb3, selb, ocmpt, zgt, tq=256, gpt=4):
    b, _, s = zt.shape
    n_sel = s // SEL_BLOCK
    assert WINDOW % tq == 0 and tq % SEL_BLOCK == 0 and NSA_GROUPS % gpt == 0
    kern = functools.partial(_nsa_kernel, tq=tq, n_sel=n_sel, gpt=gpt)
    return pl.pallas_call(
        kern,
        grid=(b, s // tq),
        in_specs=[
            pl.BlockSpec((None, NSA_Q, tq), lambda bi, j: (bi, ZT_Q // NSA_Q, j)),
            pl.BlockSpec((None, s, NSA_KV), lambda bi, j: (bi, 0, ZB_KSLC // NSA_KV)),
            pl.BlockSpec((None, s, NSA_KV), lambda bi, j: (bi, 0, ZB_KWIN // NSA_KV)),
            pl.BlockSpec((None, NSA_KV, s), lambda bi, j: (bi, ZT_VSLC // NSA_KV, 0)),
            pl.BlockSpec((None, NSA_KV, s), lambda bi, j: (bi, ZT_VWIN // NSA_KV, 0)),
            pl.BlockSpec((None, NSA_GROUPS * n_sel, tq), lambda bi, j: (bi, 0, j)),
            pl.BlockSpec((None, NSA_Q, tq), lambda bi, j: (bi, 0, j)),
            pl.BlockSpec((None, ZG_WIDTH, tq), lambda bi, j: (bi, 0, j)),
        ],
        out_specs=pl.BlockSpec((None, tq, NSA_Q), lambda bi, j: (bi, j, 0)),
        out_shape=jax.ShapeDtypeStruct((b, s, NSA_Q), BF16),
        scratch_shapes=[pltpu.VMEM((NSA_Q, tq), F32)],
        compiler_params=_cparams("parallel", "arbitrary"),
        name="nsa",
    )(zt, zb3, zb3, zt, zt, selb, ocmpt, zgt)


def _mlstm_kernel(q_ref, k_ref, v_ref, o_ref, zg_ref, cw_ref, cb_ref, gb_ref, hg_ref, y_ref,
                  xbuf, c_st, n_st, m_st):
    L = ML_CHUNK
    pad = 8
    @pl.when(pl.program_id(1) == 0)
    def _():
        xbuf[0:pad, :] = jnp.zeros((pad, 2 * ML_W), F32)
        c_st[...] = jnp.zeros_like(c_st)
        n_st[...] = jnp.zeros_like(n_st)
        m_st[...] = jnp.zeros_like(m_st)

    xbuf[pad:pad + L, 0:ML_W] = q_ref[...].astype(F32)
    xbuf[pad:pad + L, ML_W:2 * ML_W] = k_ref[...].astype(F32)
    conv = jnp.zeros((L, 2 * ML_W), F32) + cb_ref[...]
    for j in range(CONV_WIDTH):
        conv = conv + xbuf[pl.ds(pad - (CONV_WIDTH - 1) + j, L), :] * cw_ref[j:j + 1, :]
    xbuf[0:pad, :] = xbuf[L:L + pad, :]
    qk = _silu(conv)

    gates = zg_ref[...] + gb_ref[...]
    logf = _log_sigmoid(gates)
    row = lax.broadcasted_iota(jnp.int32, (L, L), 0)
    col = lax.broadcasted_iota(jnp.int32, (L, L), 1)
    tril = row >= col
    eye = row == col
    bcum = jnp.dot(jnp.where(tril, 1.0, 0.0), logf, preferred_element_type=F32,
                   precision=lax.Precision.HIGHEST)

    for h in range(ML_HEADS):
        q = qk[:, h * ML_DH:(h + 1) * ML_DH]
        k = qk[:, ML_W + h * ML_DH:ML_W + (h + 1) * ML_DH] * (ML_DH ** -0.5)
        v = v_ref[:, h * ML_DH:(h + 1) * ML_DH].astype(F32)
        i_col = gates[:, ZG_IF + h:ZG_IF + h + 1]
        b_col = bcum[:, ZG_IF + ML_HEADS + h:ZG_IF + ML_HEADS + h + 1]
        r_col = i_col - b_col
        r_row = jnp.sum(jnp.where(eye, r_col, 0.0), axis=0, keepdims=True)
        dlog = jnp.where(tril, b_col + r_row, -jnp.inf)
        m_prev = m_st[h:h + 1, 0:1]
        inter = b_col + m_prev
        m_t = jnp.maximum(inter, jnp.max(dlog, axis=-1, keepdims=True))
        dw = jnp.exp(dlog - m_t)
        iw = jnp.exp(inter - m_t)
        c_old = c_st[h]
        n_old = n_st[h:h + 1, :]
        sqk = lax.dot_general(q, k, _NT, preferred_element_type=F32) * dw
        cq = lax.dot_general(q, c_old, _NT, preferred_element_type=F32)
        num = jnp.dot(sqk, v, preferred_element_type=F32) + iw * cq
        den = jnp.sum(sqk, axis=-1, keepdims=True) + iw * jnp.sum(q * n_old, axis=-1, keepdims=True)
        hs = num / jnp.maximum(jnp.abs(den), jnp.exp(-m_t))
        b_end = b_col[L - 1:L, :]
        wlog = b_end - b_col + i_col
        m_new = jnp.maximum(b_end + m_prev, jnp.max(wlog, axis=0, keepdims=True))
        ws = jnp.exp(wlog - m_new)
        decay = jnp.exp(b_end + m_prev - m_new)
        kw = k * ws
        c_st[h] = decay * c_old + lax.dot_general(v, kw, _TN, preferred_element_type=F32)
        n_st[h:h + 1, :] = decay * n_old + jnp.sum(kw, axis=0, keepdims=True)
        m_st[h:h + 1, :] = jnp.zeros((1, LANE), F32) + m_new
        hn = hs * lax.rsqrt(jnp.mean(hs * hs, axis=-1, keepdims=True) + EPS)
        hn = hn * hg_ref[:, h * ML_DH:(h + 1) * ML_DH]
        og = _sigmoid(o_ref[:, h * ML_DH:(h + 1) * ML_DH].astype(F32))
        y_ref[:, h * ML_DH:(h + 1) * ML_DH] = (og * hn).astype(BF16)


def _mlstm(zb3, zg3, conv_w, conv_b, gate_b, head_g):
    b, s, _ = zb3.shape
    L = ML_CHUNK
    qb = ZB_QKVML // ML_W

    def zspec(k):
        return pl.BlockSpec((None, L, ML_W), lambda bi, c, k=k: (bi, c, k))

    return pl.pallas_call(
        _mlstm_kernel,
        grid=(b, s // L),
        in_specs=[
            zspec(qb), zspec(qb + 1), zspec(qb + 2), zspec(ZB_OML // ML_W),
            pl.BlockSpec((None, L, ZG_WIDTH), lambda bi, c: (bi, c, 0)),
            pl.BlockSpec((CONV_WIDTH, 2 * ML_W), lambda bi, c: (0, 0)),
            pl.BlockSpec((1, 2 * ML_W), lambda bi, c: (0, 0)),
            pl.BlockSpec((1, ZG_WIDTH), lambda bi, c: (0, 0)),
            pl.BlockSpec((1, ML_W), lambda bi, c: (0, 0)),
        ],
        out_specs=pl.BlockSpec((None, L, ML_W), lambda bi, c: (bi, c, 0)),
        out_shape=jax.ShapeDtypeStruct((b, s, ML_W), BF16),
        scratch_shapes=[
            pltpu.VMEM((8 + L, 2 * ML_W), F32),
            pltpu.VMEM((ML_HEADS, ML_DH, ML_DH), F32),
            pltpu.VMEM((8, ML_DH), F32),
            pltpu.VMEM((8, LANE), F32),
        ],
        compiler_params=_cparams("parallel", "arbitrary"),
        name="mlstm",
    )(zb3, zb3, zb3, zb3, zg3, conv_w, conv_b, gate_b, head_g)


def _memkv_kernel(mem_ref, g_ref, w_ref, kv_ref):
    hb = _rms(mem_ref[...], g_ref[...]).astype(BF16)
    kv_ref[...] = jnp.dot(hb, w_ref[...], preferred_element_type=F32).astype(BF16)


def _memkv(mem, g, wb):
    b, m, _ = mem.shape
    return pl.pallas_call(
        _memkv_kernel,
        grid=(b,),
        in_specs=[
            pl.BlockSpec((None, m, D_MODEL), lambda i: (i, 0, 0)),
            pl.BlockSpec((1, D_MODEL), lambda i: (0, 0)),
            pl.BlockSpec((D_MODEL, 2 * MEM_W), lambda i: (0, 0)),
        ],
        out_specs=pl.BlockSpec((None, m, 2 * MEM_W), lambda i: (i, 0, 0)),
        out_shape=jax.ShapeDtypeStruct((b, m, 2 * MEM_W), BF16),
        compiler_params=_cparams("parallel"),
        name="memkv",
    )(mem, g, wb)


def _memattn_kernel(q_ref, kv_ref, y_ref):
    for h in range(MEM_HEADS):
        q = q_ref[:, h * MEM_DH:(h + 1) * MEM_DH]
        k = kv_ref[:, h * MEM_DH:(h + 1) * MEM_DH]
        v = kv_ref[:, MEM_W + h * MEM_DH:MEM_W + (h + 1) * MEM_DH]
        s = lax.dot_general(q, k, _NT, preferred_element_type=F32) * MEM_SCALE
        e = jnp.exp(s - jnp.max(s, axis=-1, keepdims=True))
        p = e / jnp.sum(e, axis=-1, keepdims=True)
        o = jnp.dot(p.astype(BF16), v, preferred_element_type=F32)
        y_ref[:, h * MEM_DH:(h + 1) * MEM_DH] = o.astype(BF16)


def _memattn(zb3, kvm, tq=512):
    b, s, _ = zb3.shape
    m = kvm.shape[1]
    return pl.pallas_call(
        _memattn_kernel,
        grid=(b, s // tq),
        in_specs=[
            pl.BlockSpec((None, tq, MEM_W), lambda i, j: (i, j, ZB_QMEM // MEM_W)),
            pl.BlockSpec((None, m, 2 * MEM_W), lambda i, j: (i, 0, 0)),
        ],
        out_specs=pl.BlockSpec((None, tq, MEM_W), lambda i, j: (i, j, 0)),
        out_shape=jax.ShapeDtypeStruct((b, s, MEM_W), BF16),
        compiler_params=_cparams("parallel", "parallel"),
        name="memattn",
    )(zb3, kvm)


def _merge_kernel(x_ref, yn_ref, yl_ref, ym_ref, g0_ref, g1_ref, g2_ref,
                  wn_ref, wl_ref, wm_ref, wo_ref, gp_ref, out_ref):
    y = _sigmoid(g0_ref[...].astype(F32)) * jnp.dot(yn_ref[...], wn_ref[...], preferred_element_type=F32)
    y = y + _sigmoid(g1_ref[...].astype(F32)) * jnp.dot(yl_ref[...], wl_ref[...], preferred_element_type=F32)
    y = y + _sigmoid(g2_ref[...].astype(F32)) * jnp.dot(ym_ref[...], wm_ref[...], preferred_element_type=F32)
    u = jnp.dot(y.astype(BF16), wo_ref[...], preferred_element_type=F32)
    out_ref[...] = x_ref[...] + _rms(u, gp_ref[...])


def _merge(x2, yn, yl, ym, zb, wn, wl, wm, wo, gp, tm=512):
    t = x2.shape[0]
    gm = ZB_GMERGE // D_MODEL

    def const(shape):
        return pl.BlockSpec(shape, lambda i: (0, 0))

    return pl.pallas_call(
        _merge_kernel,
        grid=(t // tm,),
        in_specs=[
            pl.BlockSpec((tm, D_MODEL), lambda i: (i, 0)),
            pl.BlockSpec((tm, NSA_Q), lambda i: (i, 0)),
            pl.BlockSpec((tm, ML_W), lambda i: (i, 0)),
            pl.BlockSpec((tm, MEM_W), lambda i: (i, 0)),
            pl.BlockSpec((tm, D_MODEL), lambda i: (i, gm)),
            pl.BlockSpec((tm, D_MODEL), lambda i: (i, gm + 1)),
            pl.BlockSpec((tm, D_MODEL), lambda i: (i, gm + 2)),
            const((NSA_Q, D_MODEL)), const((ML_W, D_MODEL)), const((MEM_W, D_MODEL)),
            const((D_MODEL, D_MODEL)), const((1, D_MODEL)),
        ],
        out_specs=pl.BlockSpec((tm, D_MODEL), lambda i: (i, 0)),
        out_shape=jax.ShapeDtypeStruct((t, D_MODEL), F32),
        compiler_params=_cparams("parallel"),
        name="merge",
    )(x2, yn, yl, ym, zb, zb, zb, wn, wl, wm, wo, gp)


def _ffn_kernel(x_ref, gpre_ref, wg_ref, wu_ref, wd_ref, gpost_ref, out_ref, h_ref, acc_ref):
    j = pl.program_id(1)

    @pl.when(j == 0)
    def _():
        h_ref[...] = _rms(x_ref[...], gpre_ref[...]).astype(BF16)
        acc_ref[...] = jnp.zeros_like(acc_ref)

    h = h_ref[...]
    gate = jnp.dot(h, wg_ref[...], preferred_element_type=F32)
    up = jnp.dot(h, wu_ref[...], preferred_element_type=F32)
    act = (_silu(gate) * up).astype(BF16)
    acc_ref[...] += jnp.dot(act, wd_ref[...], preferred_element_type=F32)

    @pl.when(j == pl.num_programs(1) - 1)
    def _():
        out_ref[...] = x_ref[...] + _rms(acc_ref[...], gpost_ref[...])


def _ffn(x2, gpre, w_in, w_down, gpost, tm=512, nf=2):
    t = x2.shape[0]
    tf = D_FF // nf
    assert tf % LANE == 0
    return pl.pallas_call(
        _ffn_kernel,
        grid=(t // tm, nf),
        in_specs=[
            pl.BlockSpec((tm, D_MODEL), lambda i, j: (i, 0)),
            pl.BlockSpec((1, D_MODEL), lambda i, j: (0, 0)),
            pl.BlockSpec((D_MODEL, tf), lambda i, j: (0, j)),
            pl.BlockSpec((D_MODEL, tf), lambda i, j: (0, nf + j)),
            pl.BlockSpec((tf, D_MODEL), lambda i, j: (j, 0)),
            pl.BlockSpec((1, D_MODEL), lambda i, j: (0, 0)),
        ],
        out_specs=pl.BlockSpec((tm, D_MODEL), lambda i, j: (i, 0)),
        out_shape=jax.ShapeDtypeStruct((t, D_MODEL), F32),
        scratch_shapes=[pltpu.VMEM((tm, D_MODEL), BF16), pltpu.VMEM((tm, D_MODEL), F32)],
        compiler_params=_cparams("parallel", "arbitrary"),
        name="ffn",
    )(x2, gpre, w_in, w_in, w_down, gpost)


def _layer(x, mem, p):
    b, s, _ = x.shape
    t = b * s
    x2 = x.reshape(t, D_MODEL)

    w_in = p["w_in"]
    kv0 = _OFF_KV

    def kv_cols(k):
        return w_in[:, kv0 + k * NSA_KV:kv0 + (k + 1) * NSA_KV]

    wn = jnp.concatenate([
        w_in[:, _OFF_GMERGE:_IN_WIDTH],
        kv_cols(0), kv_cols(1), kv_cols(2), kv_cols(4),
        w_in[:, _OFF_QKVML:_OFF_IF],
        w_in[:, _OFF_OML:_OFF_QMEM],
        w_in[:, _OFF_QMEM:_OFF_GMERGE],
    ], axis=1).astype(BF16)
    wt = jnp.concatenate([
        w_in[:, _OFF_Q:_OFF_KV] * (NSA_SCALE * LOG2E),
        kv_cols(3), kv_cols(5),
    ], axis=1).T.astype(BF16)
    wg = jnp.concatenate([
        w_in[:, _OFF_GNSA:_OFF_QKVML],
        w_in[:, _OFF_IF:_OFF_OML],
        jnp.zeros((D_MODEL, ZG_WIDTH - 3 * NSA_HEADS - 2 * ML_HEADS), F32),
    ], axis=1).astype(BF16)
    zt, zb, zg, zgt = _inproj(x2, p["g_pre_mix"].reshape(1, D_MODEL), wt, wn, wg, wg.T, b, s)
    zb3 = zb.reshape(b, s, ZB_WIDTH)
    zg3 = zg.reshape(b, s, ZG_WIDTH)

    n_rows = s // CMP_STRIDE
    half = (CMP_BLOCK // 2) * NSA_DH
    a = zb3[:, :, ZB_KVC:ZB_KVC + 2 * NSA_KV].reshape(b, n_rows, CMP_STRIDE, 2, NSA_GROUPS, NSA_DH)
    a = a.transpose(0, 3, 4, 1, 2, 5).reshape(b, 2 * NSA_GROUPS, n_rows, half)
    pe = jnp.stack([p["cmp_pe_k"], p["cmp_pe_v"]]).reshape(2, 1, 2 * half)
    pe = jnp.pad(pe, ((0, 0), (0, 7), (0, 0))).astype(BF16)
    w1 = jnp.stack([p["cmp_w1_k"], p["cmp_w1_v"]]).reshape(2, 2 * half, CMP_HIDDEN).astype(BF16)
    kc, vct = _compress(a, pe, w1, p["cmp_w2_k"].astype(BF16), p["cmp_w2_v"].T.astype(BF16))
    ocmpt, selb = _cmpsel(zt, kc, vct)
    y_nsa = _nsa(zt, zb3, selb, ocmpt, zgt)

    gate_b = jnp.zeros((1, ZG_WIDTH), F32).at[0, ZG_IF:ZG_IF + 2 * ML_HEADS].set(p["ml_gate_b"])
    y_ml = _mlstm(zb3, zg3, p["ml_conv_w"], p["ml_conv_b"].reshape(1, 2 * ML_W), gate_b,
                  p["ml_head_g"].reshape(1, ML_W))

    kvm = _memkv(mem, p["g_mem"].reshape(1, D_MODEL), p["w_mem_kv"].astype(BF16))
    y_mem = _memattn(zb3, kvm)

    x1 = _merge(x2, y_nsa.reshape(t, NSA_Q), y_ml.reshape(t, ML_W), y_mem.reshape(t, MEM_W), zb,
                p["w_proj_nsa"].astype(BF16), p["w_proj_ml"].astype(BF16),
                p["w_proj_mem"].astype(BF16), p["w_out"].astype(BF16),
                p["g_post_mix"].reshape(1, D_MODEL))
    x2o = _ffn(x1, p["g_pre_ffn"].reshape(1, D_MODEL), p["w_ffn_in"].astype(BF16),
               p["w_ffn_down"].astype(BF16), p["g_post_ffn"].reshape(1, D_MODEL))
    return x2o.reshape(b, s, D_MODEL)


def kernel(x, mem, g_pre_mix, w_in, cmp_pe_k, cmp_w1_k, cmp_w2_k, cmp_pe_v, cmp_w1_v, cmp_w2_v,
           ml_conv_w, ml_conv_b, ml_gate_b, ml_head_g, g_mem, w_mem_kv, w_proj_nsa, w_proj_ml,
           w_proj_mem, w_out, g_post_mix, g_pre_ffn, w_ffn_in, w_ffn_down, g_post_ffn):
    params = dict(
        g_pre_mix=g_pre_mix, w_in=w_in, cmp_pe_k=cmp_pe_k, cmp_w1_k=cmp_w1_k, cmp_w2_k=cmp_w2_k,
        cmp_pe_v=cmp_pe_v, cmp_w1_v=cmp_w1_v, cmp_w2_v=cmp_w2_v, ml_conv_w=ml_conv_w,
        ml_conv_b=ml_conv_b, ml_gate_b=ml_gate_b, ml_head_g=ml_head_g, g_mem=g_mem,
        w_mem_kv=w_mem_kv, w_proj_nsa=w_proj_nsa, w_proj_ml=w_proj_ml, w_proj_mem=w_proj_mem,
        w_out=w_out, g_post_mix=g_post_mix, g_pre_ffn=g_pre_ffn, w_ffn_in=w_ffn_in,
        w_ffn_down=w_ffn_down, g_post_ffn=g_post_ffn)
    depth = w_in.shape[0]
    for l in range(depth):
        x = _layer(x, mem, {k: v[l] for k, v in params.items()})
    return x
```

```python
import functools
import math

import jax
import jax.numpy as jnp
from jax import lax
from jax.experimental import pallas as pl
from jax.experimental.pallas import tpu as pltpu

F32 = jnp.float32
BF16 = jnp.bfloat16

D_MODEL = 1024
EPS = 1e-6
NEG_INF = -1e30
BIG = 1e30

NSA_HEADS = 16
NSA_GROUPS = 4
NSA_HPG = NSA_HEADS // NSA_GROUPS
NSA_DH = 64
NSA_SCALE = NSA_DH ** -0.5
CMP_BLOCK = 32
CMP_STRIDE = 16
CMP_HIDDEN = 128
SEL_BLOCK = 64
SEL_TOPK = 16
N_LOCAL_FORCED = 2
WINDOW = 512

ML_HEADS = 4
ML_DH = 128
ML_CHUNK = 64
CONV_WIDTH = 4

MEM_HEADS = 4
MEM_DH = 128
MEM_SCALE = MEM_DH ** -0.5

D_FF = -(-8 * D_MODEL // (3 * 256)) * 256

NSA_Q = NSA_HEADS * NSA_DH
NSA_KV = NSA_GROUPS * NSA_DH
ML_W = ML_HEADS * ML_DH
MEM_W = MEM_HEADS * MEM_DH

_OFF_Q = 0
_OFF_KV = _OFF_Q + NSA_Q
_OFF_GNSA = _OFF_KV + 6 * NSA_KV
_OFF_QKVML = _OFF_GNSA + 3 * NSA_HEADS
_OFF_IF = _OFF_QKVML + 3 * ML_W
_OFF_OML = _OFF_IF + 2 * ML_HEADS
_OFF_QMEM = _OFF_OML + ML_W
_OFF_GMERGE = _OFF_QMEM + MEM_W
_IN_WIDTH = _OFF_GMERGE + 3 * D_MODEL

ZB_GMERGE = 0
ZB_KVC = ZB_GMERGE + 3 * D_MODEL
ZB_KSLC = ZB_KVC + 2 * NSA_KV
ZB_KWIN = ZB_KSLC + NSA_KV
ZB_QKVML = ZB_KWIN + NSA_KV
ZB_OML = ZB_QKVML + 3 * ML_W
ZB_QMEM = ZB_OML + ML_W
ZB_WIDTH = ZB_QMEM + MEM_W
ZT_Q = 0
ZT_VSLC = ZT_Q + NSA_Q
ZT_VWIN = ZT_VSLC + NSA_KV
ZT_ROWS = ZT_VWIN + NSA_KV
ZG_WIDTH = 128
ZG_GNSA = 0
ZG_IF = 3 * NSA_HEADS

LANE = 128
VMEM_LIMIT = 48 * 1024 * 1024
LOG2E = math.log2(math.e)

_NT = (((1,), (1,)), ((), ()))
_TN = (((0,), (0,)), ((), ()))


def _cparams(*sem):
    return pltpu.CompilerParams(dimension_semantics=sem, vmem_limit_bytes=VMEM_LIMIT)


def _rms(x, g):
    return x * lax.rsqrt(jnp.mean(x * x, axis=-1, keepdims=True) + EPS) * g


def _sigmoid(x):
    return 1.0 / (1.0 + jnp.exp(-x))


def _silu(x):
    return x * _sigmoid(x)


def _log_sigmoid(x):
    return jnp.minimum(x, 0.0) - jnp.log(1.0 + jnp.exp(-jnp.abs(x)))


def _inproj_kernel(x_ref, g_ref, wt_ref, wn_ref, wg_ref, wgt_ref, zt_ref, zb_ref, zg_ref, zgt_ref,
                   h_ref, *, nt):
    j = pl.program_id(1)

    @pl.when(j == 0)
    def _():
        hb = _rms(x_ref[...], g_ref[...]).astype(BF16)
        h_ref[...] = hb
        zg_ref[...] = jnp.dot(hb, wg_ref[...], preferred_element_type=F32)
        zgt_ref[...] = lax.dot_general(wgt_ref[...], hb, _NT, preferred_element_type=F32)

    @pl.when(j < nt)
    def _():
        zt_ref[...] = lax.dot_general(wt_ref[...], h_ref[...], _NT,
                                      preferred_element_type=F32).astype(BF16)

    @pl.when(j >= nt)
    def _():
        zb_ref[...] = jnp.dot(h_ref[...], wn_ref[...], preferred_element_type=F32).astype(BF16)


def _inproj(x2, g, wt, wn, wg, wgt, b, s, tm=1024, nt=2, nn=4):
    t = x2.shape[0]
    tr = ZT_ROWS // nt
    tn = ZB_WIDTH // nn
    assert tr * nt == ZT_ROWS and tn * nn == ZB_WIDTH and tn % LANE == 0 and s % tm == 0
    spb = s // tm
    kern = functools.partial(_inproj_kernel, nt=nt)
    return pl.pallas_call(
        kern,
        grid=(t // tm, nt + nn),
        in_specs=[
            pl.BlockSpec((tm, D_MODEL), lambda i, j: (i, 0)),
            pl.BlockSpec((1, D_MODEL), lambda i, j: (0, 0)),
            pl.BlockSpec((tr, D_MODEL), lambda i, j: (jnp.minimum(j, nt - 1), 0)),
            pl.BlockSpec((D_MODEL, tn), lambda i, j: (0, jnp.maximum(j - nt, 0))),
            pl.BlockSpec((D_MODEL, ZG_WIDTH), lambda i, j: (0, 0)),
            pl.BlockSpec((ZG_WIDTH, D_MODEL), lambda i, j: (0, 0)),
        ],
        out_specs=[
            pl.BlockSpec((None, tr, tm), lambda i, j: (i // spb, jnp.minimum(j, nt - 1), i % spb)),
            pl.BlockSpec((tm, tn), lambda i, j: (i, jnp.maximum(j - nt, 0))),
            pl.BlockSpec((tm, ZG_WIDTH), lambda i, j: (i, 0)),
            pl.BlockSpec((None, ZG_WIDTH, tm), lambda i, j: (i // spb, 0, i % spb)),
        ],
        out_shape=[
            jax.ShapeDtypeStruct((b, ZT_ROWS, s), BF16),
            jax.ShapeDtypeStruct((t, ZB_WIDTH), BF16),
            jax.ShapeDtypeStruct((t, ZG_WIDTH), F32),
            jax.ShapeDtypeStruct((b, ZG_WIDTH, s), F32),
        ],
        scratch_shapes=[pltpu.VMEM((tm, D_MODEL), BF16)],
        compiler_params=_cparams("parallel", "arbitrary"),
        name="inproj",
    )(x2, g, wt, wn, wg, wgt)


def _compress_kernel(a_ref, pe_ref, w1_ref, w2k_ref, w2vt_ref, kc_ref, vct_ref):
    half = (CMP_BLOCK // 2) * NSA_DH
    for kind in range(2):
        w1a = w1_ref[kind, 0:half, :]
        w1b = w1_ref[kind, half:2 * half, :]
        pe = pe_ref[kind]
        c = (jnp.dot(pe[:, 0:half], w1a, preferred_element_type=F32)
             + jnp.dot(pe[:, half:2 * half], w1b, preferred_element_type=F32))[0:1, :]
        for g in range(NSA_GROUPS):
            a = a_ref[kind * NSA_GROUPS + g]
            p = jnp.dot(a, w1a, preferred_element_type=F32)
            q = jnp.dot(a, w1b, preferred_element_type=F32)
            n_rows = q.shape[0]
            qs = pltpu.roll(q, shift=n_rows - 1, axis=0)
            hid = _silu(p + qs + c).astype(BF16)
            if kind == 0:
                o = jnp.dot(hid, w2k_ref[...], preferred_element_type=F32)
                kc_ref[:, g * NSA_DH:(g + 1) * NSA_DH] = o.astype(BF16)
            else:
                ot = lax.dot_general(w2vt_ref[...], hid, _NT, preferred_element_type=F32)
                vct_ref[g * NSA_DH:(g + 1) * NSA_DH, :] = ot.astype(BF16)


def _compress(a, pe, w1, w2k, w2vt):
    b, _, n_rows, half = a.shape
    return pl.pallas_call(
        _compress_kernel,
        grid=(b,),
        in_specs=[
            pl.BlockSpec((None, 2 * NSA_GROUPS, n_rows, half), lambda i: (i, 0, 0, 0)),
            pl.BlockSpec((2, 8, 2 * half), lambda i: (0, 0, 0)),
            pl.BlockSpec((2, 2 * half, CMP_HIDDEN), lambda i: (0, 0, 0)),
            pl.BlockSpec((CMP_HIDDEN, NSA_DH), lambda i: (0, 0)),
            pl.BlockSpec((NSA_DH, CMP_HIDDEN), lambda i: (0, 0)),
        ],
        out_specs=[
            pl.BlockSpec((None, n_rows, NSA_KV), lambda i: (i, 0, 0)),
            pl.BlockSpec((None, NSA_KV, n_rows), lambda i: (i, 0, 0)),
        ],
        out_shape=[
            jax.ShapeDtypeStruct((b, n_rows, NSA_KV), BF16),
            jax.ShapeDtypeStruct((b, NSA_KV, n_rows), BF16),
        ],
        compiler_params=_cparams("parallel"),
        name="compress",
    )(a, pe, w1, w2k, w2vt)


def _pad_query(qt, g):
    z = jnp.zeros_like(qt)
    return jnp.concatenate([qt, z] if g % 2 == 0 else [z, qt], axis=0)


def _cmpsel_kernel(qt_ref, kc_ref, vct_ref, ocmpt_ref, selb_ref, *, tq, n_sel, topk):
    nc = kc_ref.shape[0]
    t0 = pl.program_id(1) * tq

    n_t = lax.broadcasted_iota(jnp.int32, (nc, tq), 0)
    t_t = t0 + lax.broadcasted_iota(jnp.int32, (nc, tq), 1)
    mask_t = (n_t * CMP_STRIDE + (CMP_BLOCK - 1)) <= t_t

    j_m = lax.broadcasted_iota(jnp.int32, (n_sel, nc), 0) * SEL_BLOCK
    c_m = lax.broadcasted_iota(jnp.int32, (n_sel, nc), 1) * CMP_STRIDE
    ov = jnp.minimum(c_m + CMP_BLOCK, j_m + SEL_BLOCK) - jnp.maximum(c_m, j_m)
    map_t = jnp.maximum(ov, 0).astype(F32) * (1.0 / CMP_BLOCK)

    j_s = lax.broadcasted_iota(jnp.int32, (n_sel, tq), 0)
    qblk = (t0 + lax.broadcasted_iota(jnp.int32, (n_sel, tq), 1)) // SEL_BLOCK
    rel = qblk - j_s
    causal = rel >= 0
    forced = causal & ((j_s == 0) | (rel < N_LOCAL_FORCED))

    for g in range(NSA_GROUPS):
        pair = g // 2
        kcp = kc_ref[:, pair * LANE:(pair + 1) * LANE]
        vct = vct_ref[g * NSA_DH:(g + 1) * NSA_DH, :]
        psum_t = jnp.zeros((nc, tq), F32)
        sts = [jnp.dot(kcp, _pad_query(qt_ref[(g * NSA_HPG + h) * NSA_DH:(g * NSA_HPG + h + 1) * NSA_DH, :], g),
                       preferred_element_type=F32) for h in range(NSA_HPG)]
        for h in range(NSA_HPG):
            hh = g * NSA_HPG + h
            st = jnp.where(mask_t, sts[h], NEG_INF)
            et = jnp.where(mask_t, jnp.exp2(st - jnp.max(st, axis=0, keepdims=True)), 0.0)
            lt = jnp.sum(et, axis=0, keepdims=True)
            pt = et * (1.0 / jnp.where(lt > 0.0, lt, 1.0))
            psum_t = psum_t + pt
            ot = jnp.dot(vct, pt.astype(BF16), preferred_element_type=F32)
            ocmpt_ref[hh * NSA_DH:(hh + 1) * NSA_DH, :] = ot.astype(BF16)
        imp_t = jnp.dot(map_t, psum_t, preferred_element_type=F32,
                        precision=lax.Precision.HIGHEST)
        score = jnp.where(forced, BIG, jnp.where(causal, imp_t, NEG_INF))
        rank = jnp.zeros((n_sel, tq), jnp.int32)
        for jp in range(n_sel):
            row = score[jp:jp + 1, :]
            before = (row > score) | ((row == score) & (j_s > jp))
            rank = rank + before.astype(jnp.int32)
        chosen = (rank < topk) & (score > 0.5 * NEG_INF)
        selb_ref[g * n_sel:(g + 1) * n_sel, :] = jnp.where(chosen, 0.0, NEG_INF)


def _cmpsel(zt, kc, vct, tq=256):
    b, _, s = zt.shape
    n_sel = s // SEL_BLOCK
    topk = min(SEL_TOPK, n_sel)
    nc = kc.shape[1]
    kern = functools.partial(_cmpsel_kernel, tq=tq, n_sel=n_sel, topk=topk)
    return pl.pallas_call(
        kern,
        grid=(b, s // tq),
        in_specs=[
            pl.BlockSpec((None, NSA_Q, tq), lambda i, j: (i, ZT_Q // NSA_Q, j)),
            pl.BlockSpec((None, nc, NSA_KV), lambda i, j: (i, 0, 0)),
            pl.BlockSpec((None, NSA_KV, nc), lambda i, j: (i, 0, 0)),
        ],
        out_specs=[
            pl.BlockSpec((None, NSA_Q, tq), lambda i, j: (i, 0, j)),
            pl.BlockSpec((None, NSA_GROUPS * n_sel, tq), lambda i, j: (i, 0, j)),
        ],
        out_shape=[
            jax.ShapeDtypeStruct((b, NSA_Q, s), BF16),
            jax.ShapeDtypeStruct((b, NSA_GROUPS * n_sel, s), F32),
        ],
        compiler_params=_cparams("parallel", "parallel"),
        name="cmpsel",
    )(zt, kc, vct)


def _nsa_kernel(qt_ref, ks_ref, kw_ref, vst_ref, vwt_ref, selb_ref, ocmpt_ref, gt_ref, y_ref,
                yt_scr, qp_scr, m_scr, acc_scr, st_scr, mx_scr, bias_scr, *, tq, n_sel, look):
    tk = tq
    nw = WINDOW // tk
    sel_per_tile = tk // SEL_BLOCK
    den_rows = 16
    i = pl.program_id(1)
    d0 = (lax.broadcasted_iota(jnp.int32, (tk, tq), 1)
          - lax.broadcasted_iota(jnp.int32, (tk, tq), 0))
    caus = jnp.where(d0 >= 0, 0.0, NEG_INF)
    lowb = jnp.where(d0 < 0, 0.0, NEG_INF)
    blk_of_key = lax.broadcasted_iota(jnp.int32, (tk, n_sel), 0) // SEL_BLOCK
    blk_col = lax.broadcasted_iota(jnp.int32, (tk, n_sel), 1)
    ones_rows = jnp.ones((den_rows, tk), BF16)
    gates = _sigmoid(gt_ref[...])

    for hh in range(NSA_HEADS):
        qp_scr[hh] = _pad_query(qt_ref[hh * NSA_DH:(hh + 1) * NSA_DH, :], hh // NSA_HPG)
    m_scr[...] = jnp.full(m_scr.shape, NEG_INF, F32)
    acc_scr[...] = jnp.zeros(acc_scr.shape, F32)

    def step(br, k_ref, vt_ref, kt, sel_on, mask):
        r0 = pl.multiple_of(kt * tk, tk)
        ktiles = [k_ref[pl.ds(r0, tk), pair * LANE:(pair + 1) * LANE] for pair in range(NSA_GROUPS // 2)]
        vt_augs = [jnp.concatenate([vt_ref[g * NSA_DH:(g + 1) * NSA_DH, pl.ds(r0, tk)], ones_rows], axis=0)
                   for g in range(NSA_GROUPS)]
        if sel_on:
            expand = jnp.where(blk_col == kt * sel_per_tile + blk_of_key, 1.0, 0.0).astype(BF16)
        for g in range(NSA_GROUPS):
            if sel_on:
                bias = jnp.dot(expand, selb_ref[g * n_sel:(g + 1) * n_sel, :].astype(BF16),
                               preferred_element_type=F32)
                bias_scr[g] = bias if mask is None else bias + mask
            elif mask is not None and g == 0:
                bias_scr[0] = mask

        def qk(hh):
            g = hh // NSA_HPG
            st = jnp.dot(ktiles[g // 2], qp_scr[hh], preferred_element_type=F32)
            if sel_on:
                st = st + bias_scr[g]
            elif mask is not None:
                st = st + bias_scr[0]
            st_scr[hh] = st
            mx_scr[hh] = jnp.max(st, axis=0, keepdims=True)

        def softmax_pv(hh):
            vt_aug = vt_augs[hh // NSA_HPG]
            m = m_scr[br, hh]
            m_new = jnp.maximum(m, mx_scr[hh])
            alpha = jnp.exp2(m - m_new)
            p = jnp.exp2(st_scr[hh] - m_new).astype(BF16)
            acc_scr[br, hh] = alpha * acc_scr[br, hh] + jnp.dot(vt_aug, p, preferred_element_type=F32)
            m_scr[br, hh] = m_new

        for hh in range(look):
            qk(hh)
        for hh in range(NSA_HEADS):
            if hh + look < NSA_HEADS:
                qk(hh + look)
            softmax_pv(hh)

    def slc_body(kt, c):
        step(0, ks_ref, vst_ref, kt, True, None)
        return c

    lax.fori_loop(0, i, slc_body, 0)
    step(0, ks_ref, vst_ref, i, True, caus)

    def win_low(_, c):
        step(1, kw_ref, vwt_ref, i - nw, False, lowb)
        return c

    def win_mid(kt, c):
        step(1, kw_ref, vwt_ref, kt, False, None)
        return c

    lax.fori_loop(0, (i >= nw).astype(jnp.int32), win_low, 0)
    lax.fori_loop(jnp.maximum(i - nw + 1, 0), i, win_mid, 0)
    step(1, kw_ref, vwt_ref, i, False, caus)

    for hh in range(NSA_HEADS):
        acc_s = acc_scr[0, hh]
        acc_w = acc_scr[1, hh]
        o_slc = acc_s[0:NSA_DH] * (1.0 / acc_s[NSA_DH:NSA_DH + 1])
        o_win = acc_w[0:NSA_DH] * (1.0 / acc_w[NSA_DH:NSA_DH + 1])
        o_cmp = ocmpt_ref[hh * NSA_DH:(hh + 1) * NSA_DH, :].astype(F32)
        gc = ZG_GNSA + 3 * hh
        yt_scr[hh * NSA_DH:(hh + 1) * NSA_DH, :] = (
            gates[gc:gc + 1] * o_cmp + gates[gc + 1:gc + 2] * o_slc + gates[gc + 2:gc + 3] * o_win)

    y_ref[...] = yt_scr[...].T.astype(BF16)


def _nsa(zt, zb3, selb, ocmpt, zgt, tq=256, look=5):
    b, _, s = zt.shape
    n_sel = s // SEL_BLOCK
    tk = tq
    den_rows = 16
    assert WINDOW % tq == 0 and tq % SEL_BLOCK == 0
    kern = functools.partial(_nsa_kernel, tq=tq, n_sel=n_sel, look=look)
    return pl.pallas_call(
        kern,
        grid=(b, s // tq),
        in_specs=[
            pl.BlockSpec((None, NSA_Q, tq), lambda bi, j: (bi, ZT_Q // NSA_Q, j)),
            pl.BlockSpec((None, s, NSA_KV), lambda bi, j: (bi, 0, ZB_KSLC // NSA_KV)),
            pl.BlockSpec((None, s, NSA_KV), lambda bi, j: (bi, 0, ZB_KWIN // NSA_KV)),
            pl.BlockSpec((None, NSA_KV, s), lambda bi, j: (bi, ZT_VSLC // NSA_KV, 0)),
            pl.BlockSpec((None, NSA_KV, s), lambda bi, j: (bi, ZT_VWIN // NSA_KV, 0)),
            pl.BlockSpec((None, NSA_GROUPS * n_sel, tq), lambda bi, j: (bi, 0, j)),
            pl.BlockSpec((None, NSA_Q, tq), lambda bi, j: (bi, 0, j)),
            pl.BlockSpec((None, ZG_WIDTH, tq), lambda bi, j: (bi, 0, j)),
        ],
        out_specs=pl.BlockSpec((None, tq, NSA_Q), lambda bi, j: (bi, j, 0)),
        out_shape=jax.ShapeDtypeStruct((b, s, NSA_Q), BF16),
        scratch_shapes=[
            pltpu.VMEM((NSA_Q, tq), F32),
            pltpu.VMEM((NSA_HEADS, 2 * NSA_DH, tq), BF16),
            pltpu.VMEM((2, NSA_HEADS, 1, tq), F32),
            pltpu.VMEM((2, NSA_HEADS, NSA_DH + den_rows, tq), F32),
            pltpu.VMEM((NSA_HEADS, tk, tq), F32),
            pltpu.VMEM((NSA_HEADS, 1, tq), F32),
            pltpu.VMEM((NSA_GROUPS, tk, tq), F32),
        ],
        compiler_params=_cparams("parallel", "arbitrary"),
        name="nsa",
    )(zt, zb3, zb3, zt, zt, selb, ocmpt, zgt)


def _mlstm_kernel(q_ref, k_ref, v_ref, o_ref, zg_ref, cw_ref, cb_ref, gb_ref, hg_ref, y_ref,
                  xbuf, c_st, n_st, m_st, *, nb):
    L = ML_CHUNK
    pad = 8
    @pl.when(pl.program_id(1) == 0)
    def _():
        xbuf[:, 0:pad, :] = jnp.zeros((nb, pad, 2 * ML_W), F32)
        c_st[...] = jnp.zeros_like(c_st)
        n_st[...] = jnp.zeros_like(n_st)
        m_st[...] = jnp.zeros_like(m_st)

    row = lax.broadcasted_iota(jnp.int32, (L, L), 0)
    col = lax.broadcasted_iota(jnp.int32, (L, L), 1)
    tril = row >= col
    tril_f = jnp.where(tril, 1.0, 0.0)
    streams = [(bb, h) for bb in range(nb) for h in range(ML_HEADS)]

    qks = []
    for bb in range(nb):
        xbuf[bb, pad:pad + L, 0:ML_W] = q_ref[bb].astype(F32)
        xbuf[bb, pad:pad + L, ML_W:2 * ML_W] = k_ref[bb].astype(F32)
        conv = jnp.zeros((L, 2 * ML_W), F32) + cb_ref[...]
        for j in range(CONV_WIDTH):
            conv = conv + xbuf[bb, pl.ds(pad - (CONV_WIDTH - 1) + j, L), :] * cw_ref[j:j + 1, :]
        xbuf[bb, 0:pad, :] = xbuf[bb, L:L + pad, :]
        qks.append(_silu(conv))

    qs = {(bb, h): qks[bb][:, h * ML_DH:(h + 1) * ML_DH] for bb, h in streams}
    ks = {(bb, h): qks[bb][:, ML_W + h * ML_DH:ML_W + (h + 1) * ML_DH] * (ML_DH ** -0.5)
          for bb, h in streams}
    qbs = {s: qs[s].astype(BF16) for s in streams}
    vbs = {(bb, h): v_ref[bb, :, h * ML_DH:(h + 1) * ML_DH] for bb, h in streams}
    c_olds = {(bb, h): c_st[bb * ML_HEADS + h] for bb, h in streams}
    n_olds = {(bb, h): n_st[bb, h:h + 1, :] for bb, h in streams}
    s_qk = {s: lax.dot_general(qbs[s], ks[s].astype(BF16), _NT, preferred_element_type=F32)
            for s in streams}
    cqs = {s: lax.dot_general(qbs[s], c_olds[s].astype(BF16), _NT, preferred_element_type=F32)
           for s in streams}

    g = []
    for bb in range(nb):
        gates = zg_ref[bb] + gb_ref[...]
        bcum = jnp.dot(tril_f, _log_sigmoid(gates), preferred_element_type=F32,
                       precision=lax.Precision.HIGHEST)
        b_al = pltpu.roll(bcum, shift=ZG_WIDTH - ML_HEADS, axis=1)
        m_prev = m_st[bb, 0:1, :]
        b_end = b_al[L - 1:L, :]
        inter = b_al + m_prev
        wlog = b_end - b_al + gates
        m_new = jnp.maximum(b_end + m_prev, jnp.max(wlog, axis=0, keepdims=True))
        ws = jnp.exp(wlog - m_new)
        decay = jnp.exp(b_end + m_prev - m_new)
        m_st[bb, 0:1, :] = m_new
        r_t = (gates - b_al).T
        g.append((b_al, inter, ws, decay, r_t))

    for bb, h in streams:
        c = ZG_IF + h
        _, _, ws, decay, _ = g[bb]
        kw = ks[bb, h] * ws[:, c:c + 1]
        upd = lax.dot_general(vbs[bb, h], kw.astype(BF16), _TN, preferred_element_type=F32)
        c_st[bb * ML_HEADS + h] = decay[:, c:c + 1] * c_olds[bb, h] + upd
        n_st[bb, h:h + 1, :] = decay[:, c:c + 1] * n_olds[bb, h] + jnp.sum(kw, axis=0, keepdims=True)

    for bb, h in streams:
        c = ZG_IF + h
        b_al, inter, _, _, r_t = g[bb]
        dlog = jnp.where(tril, b_al[:, c:c + 1] + r_t[c:c + 1, :], -jnp.inf)
        inter_c = inter[:, c:c + 1]
        m_t = jnp.maximum(inter_c, jnp.max(dlog, axis=-1, keepdims=True))
        dw = jnp.exp(dlog - m_t)
        iw = jnp.exp(inter_c - m_t)
        sqk = s_qk[bb, h] * dw
        num = jnp.dot(sqk.astype(BF16), vbs[bb, h], preferred_element_type=F32) + iw * cqs[bb, h]
        den = (jnp.sum(sqk, axis=-1, keepdims=True)
               + iw * jnp.sum(qs[bb, h] * n_olds[bb, h], axis=-1, keepdims=True))
        hs = num / jnp.maximum(jnp.abs(den), jnp.exp(-m_t))
        hn = hs * lax.rsqrt(jnp.mean(hs * hs, axis=-1, keepdims=True) + EPS)
        hn = hn * hg_ref[:, h * ML_DH:(h + 1) * ML_DH]
        og = _sigmoid(o_ref[bb, :, h * ML_DH:(h + 1) * ML_DH].astype(F32))
        y_ref[bb, :, h * ML_DH:(h + 1) * ML_DH] = (og * hn).astype(BF16)


def _mlstm(zb3, zg3, conv_w, conv_b, gate_b, head_g, nb=1):
    b, s, _ = zb3.shape
    L = ML_CHUNK
    qb = ZB_QKVML // ML_W
    assert b % nb == 0

    def zspec(k):
        return pl.BlockSpec((nb, L, ML_W), lambda bi, c, k=k: (bi, c, k))

    return pl.pallas_call(
        functools.partial(_mlstm_kernel, nb=nb),
        grid=(b // nb, s // L),
        in_specs=[
            zspec(qb), zspec(qb + 1), zspec(qb + 2), zspec(ZB_OML // ML_W),
            pl.BlockSpec((nb, L, ZG_WIDTH), lambda bi, c: (bi, c, 0)),
            pl.BlockSpec((CONV_WIDTH, 2 * ML_W), lambda bi, c: (0, 0)),
            pl.BlockSpec((1, 2 * ML_W), lambda bi, c: (0, 0)),
            pl.BlockSpec((1, ZG_WIDTH), lambda bi, c: (0, 0)),
            pl.BlockSpec((1, ML_W), lambda bi, c: (0, 0)),
        ],
        out_specs=pl.BlockSpec((nb, L, ML_W), lambda bi, c: (bi, c, 0)),
        out_shape=jax.ShapeDtypeStruct((b, s, ML_W), BF16),
        scratch_shapes=[
            pltpu.VMEM((nb, 8 + L, 2 * ML_W), F32),
            pltpu.VMEM((nb * ML_HEADS, ML_DH, ML_DH), F32),
            pltpu.VMEM((nb, 8, ML_DH), F32),
            pltpu.VMEM((nb, 8, LANE), F32),
        ],
        compiler_params=_cparams("parallel", "arbitrary"),
        name="mlstm",
    )(zb3, zb3, zb3, zb3, zg3, conv_w, conv_b, gate_b, head_g)


def _memkv_kernel(mem_ref, g_ref, w_ref, kv_ref):
    hb = _rms(mem_ref[...], g_ref[...]).astype(BF16)
    kv_ref[...] = jnp.dot(hb, w_ref[...], preferred_element_type=F32).astype(BF16)


def _memkv(mem, g, wb):
    b, m, _ = mem.shape
    return pl.pallas_call(
        _memkv_kernel,
        grid=(b,),
        in_specs=[
            pl.BlockSpec((None, m, D_MODEL), lambda i: (i, 0, 0)),
            pl.BlockSpec((1, D_MODEL), lambda i: (0, 0)),
            pl.BlockSpec((D_MODEL, 2 * MEM_W), lambda i: (0, 0)),
        ],
        out_specs=pl.BlockSpec((None, m, 2 * MEM_W), lambda i: (i, 0, 0)),
        out_shape=jax.ShapeDtypeStruct((b, m, 2 * MEM_W), BF16),
        compiler_params=_cparams("parallel"),
        name="memkv",
    )(mem, g, wb)


def _memattn_kernel(q_ref, kv_ref, y_ref):
    for h in range(MEM_HEADS):
        q = q_ref[:, h * MEM_DH:(h + 1) * MEM_DH]
        k = kv_ref[:, h * MEM_DH:(h + 1) * MEM_DH]
        v = kv_ref[:, MEM_W + h * MEM_DH:MEM_W + (h + 1) * MEM_DH]
        s = lax.dot_general(q, k, _NT, preferred_element_type=F32) * MEM_SCALE
        e = jnp.exp(s - jnp.max(s, axis=-1, keepdims=True))
        p = e / jnp.sum(e, axis=-1, keepdims=True)
        o = jnp.dot(p.astype(BF16), v, preferred_element_type=F32)
        y_ref[:, h * MEM_DH:(h + 1) * MEM_DH] = o.astype(BF16)


def _memattn(zb3, kvm, tq=512):
    b, s, _ = zb3.shape
    m = kvm.shape[1]
    return pl.pallas_call(
        _memattn_kernel,
        grid=(b, s // tq),
        in_specs=[
            pl.BlockSpec((None, tq, MEM_W), lambda i, j: (i, j, ZB_QMEM // MEM_W)),
            pl.BlockSpec((None, m, 2 * MEM_W), lambda i, j: (i, 0, 0)),
        ],
        out_specs=pl.BlockSpec((None, tq, MEM_W), lambda i, j: (i, j, 0)),
        out_shape=jax.ShapeDtypeStruct((b, s, MEM_W), BF16),
        compiler_params=_cparams("parallel", "parallel"),
        name="memattn",
    )(zb3, kvm)


def _merge_kernel(x_ref, yn_ref, yl_ref, ym_ref, g0_ref, g1_ref, g2_ref,
                  wn_ref, wl_ref, wm_ref, wo_ref, gp_ref, out_ref):
    y = _sigmoid(g0_ref[...].astype(F32)) * jnp.dot(yn_ref[...], wn_ref[...], preferred_element_type=F32)
    y = y + _sigmoid(g1_ref[...].astype(F32)) * jnp.dot(yl_ref[...], wl_ref[...], preferred_element_type=F32)
    y = y + _sigmoid(g2_ref[...].astype(F32)) * jnp.dot(ym_ref[...], wm_ref[...], preferred_element_type=F32)
    u = jnp.dot(y.astype(BF16), wo_ref[...], preferred_element_type=F32)
    out_ref[...] = x_ref[...] + _rms(u, gp_ref[...])


def _merge(x2, yn, yl, ym, zb, wn, wl, wm, wo, gp, tm=512):
    t = x2.shape[0]
    gm = ZB_GMERGE // D_MODEL

    def const(shape):
        return pl.BlockSpec(shape, lambda i: (0, 0))

    return pl.pallas_call(
        _merge_kernel,
        grid=(t // tm,),
        in_specs=[
            pl.BlockSpec((tm, D_MODEL), lambda i: (i, 0)),
            pl.BlockSpec((tm, NSA_Q), lambda i: (i, 0)),
            pl.BlockSpec((tm, ML_W), lambda i: (i, 0)),
            pl.BlockSpec((tm, MEM_W), lambda i: (i, 0)),
            pl.BlockSpec((tm, D_MODEL), lambda i: (i, gm)),
            pl.BlockSpec((tm, D_MODEL), lambda i: (i, gm + 1)),
            pl.BlockSpec((tm, D_MODEL), lambda i: (i, gm + 2)),
            const((NSA_Q, D_MODEL)), const((ML_W, D_MODEL)), const((MEM_W, D_MODEL)),
            const((D_MODEL, D_MODEL)), const((1, D_MODEL)),
        ],
        out_specs=pl.BlockSpec((tm, D_MODEL), lambda i: (i, 0)),
        out_shape=jax.ShapeDtypeStruct((t, D_MODEL), F32),
        compiler_params=_cparams("parallel"),
        name="merge",
    )(x2, yn, yl, ym, zb, zb, zb, wn, wl, wm, wo, gp)


def _ffn_kernel(x_ref, gpre_ref, wg_ref, wu_ref, wd_ref, gpost_ref, out_ref, h_ref, acc_ref):
    j = pl.program_id(1)

    @pl.when(j == 0)
    def _():
        h_ref[...] = _rms(x_ref[...], gpre_ref[...]).astype(BF16)
        acc_ref[...] = jnp.zeros_like(acc_ref)

    h = h_ref[...]
    gate = jnp.dot(h, wg_ref[...], preferred_element_type=F32)
    up = jnp.dot(h, wu_ref[...], preferred_element_type=F32)
    act = (_silu(gate) * up).astype(BF16)
    acc_ref[...] += jnp.dot(act, wd_ref[...], preferred_element_type=F32)

    @pl.when(j == pl.num_programs(1) - 1)
    def _():
        out_ref[...] = x_ref[...] + _rms(acc_ref[...], gpost_ref[...])


def _ffn(x2, gpre, w_in, w_down, gpost, tm=1024, nf=2):
    t = x2.shape[0]
    tf = D_FF // nf
    assert tf % LANE == 0
    return pl.pallas_call(
        _ffn_kernel,
        grid=(t // tm, nf),
        in_specs=[
            pl.BlockSpec((tm, D_MODEL), lambda i, j: (i, 0)),
            pl.BlockSpec((1, D_MODEL), lambda i, j: (0, 0)),
            pl.BlockSpec((D_MODEL, tf), lambda i, j: (0, j)),
            pl.BlockSpec((D_MODEL, tf), lambda i, j: (0, nf + j)),
            pl.BlockSpec((tf, D_MODEL), lambda i, j: (j, 0)),
            pl.BlockSpec((1, D_MODEL), lambda i, j: (0, 0)),
        ],
        out_specs=pl.BlockSpec((tm, D_MODEL), lambda i, j: (i, 0)),
        out_shape=jax.ShapeDtypeStruct((t, D_MODEL), F32),
        scratch_shapes=[pltpu.VMEM((tm, D_MODEL), BF16), pltpu.VMEM((tm, D_MODEL), F32)],
        compiler_params=_cparams("parallel", "arbitrary"),
        name="ffn",
    )(x2, gpre, w_in, w_in, w_down, gpost)


def _layer(x, mem, p):
    b, s, _ = x.shape
    t = b * s
    x2 = x.reshape(t, D_MODEL)

    w_in = p["w_in"]
    kv0 = _OFF_KV

    def kv_cols(k):
        return w_in[:, kv0 + k * NSA_KV:kv0 + (k + 1) * NSA_KV]

    wn = jnp.concatenate([
        w_in[:, _OFF_GMERGE:_IN_WIDTH],
        kv_cols(0), kv_cols(1), kv_cols(2), kv_cols(4),
        w_in[:, _OFF_QKVML:_OFF_IF],
        w_in[:, _OFF_OML:_OFF_QMEM],
        w_in[:, _OFF_QMEM:_OFF_GMERGE],
    ], axis=1).astype(BF16)
    wt = jnp.concatenate([
        w_in[:, _OFF_Q:_OFF_KV] * (NSA_SCALE * LOG2E),
        kv_cols(3), kv_cols(5),
    ], axis=1).T.astype(BF16)
    wg = jnp.concatenate([
        w_in[:, _OFF_GNSA:_OFF_QKVML],
        w_in[:, _OFF_IF:_OFF_OML],
        jnp.zeros((D_MODEL, ZG_WIDTH - 3 * NSA_HEADS - 2 * ML_HEADS), F32),
    ], axis=1).astype(BF16)
    zt, zb, zg, zgt = _inproj(x2, p["g_pre_mix"].reshape(1, D_MODEL), wt, wn, wg, wg.T, b, s)
    zb3 = zb.reshape(b, s, ZB_WIDTH)
    zg3 = zg.reshape(b, s, ZG_WIDTH)

    n_rows = s // CMP_STRIDE
    half = (CMP_BLOCK // 2) * NSA_DH
    a = zb3[:, :, ZB_KVC:ZB_KVC + 2 * NSA_KV].reshape(b, n_rows, CMP_STRIDE, 2, NSA_GROUPS, NSA_DH)
    a = a.transpose(0, 3, 4, 1, 2, 5).reshape(b, 2 * NSA_GROUPS, n_rows, half)
    pe = jnp.stack([p["cmp_pe_k"], p["cmp_pe_v"]]).reshape(2, 1, 2 * half)
    pe = jnp.pad(pe, ((0, 0), (0, 7), (0, 0))).astype(BF16)
    w1 = jnp.stack([p["cmp_w1_k"], p["cmp_w1_v"]]).reshape(2, 2 * half, CMP_HIDDEN).astype(BF16)
    kc, vct = _compress(a, pe, w1, p["cmp_w2_k"].astype(BF16), p["cmp_w2_v"].T.astype(BF16))
    ocmpt, selb = _cmpsel(zt, kc, vct)
    y_nsa = _nsa(zt, zb3, selb, ocmpt, zgt)

    gate_b = jnp.zeros((1, ZG_WIDTH), F32).at[0, ZG_IF:ZG_IF + 2 * ML_HEADS].set(p["ml_gate_b"])
    y_ml = _mlstm(zb3, zg3, p["ml_conv_w"], p["ml_conv_b"].reshape(1, 2 * ML_W), gate_b,
                  p["ml_head_g"].reshape(1, ML_W))

    kvm = _memkv(mem, p["g_mem"].reshape(1, D_MODEL), p["w_mem_kv"].astype(BF16))
    y_mem = _memattn(zb3, kvm)

    x1 = _merge(x2, y_nsa.reshape(t, NSA_Q), y_ml.reshape(t, ML_W), y_mem.reshape(t, MEM_W), zb,
                p["w_proj_nsa"].astype(BF16), p["w_proj_ml"].astype(BF16),
                p["w_proj_mem"].astype(BF16), p["w_out"].astype(BF16),
                p["g_post_mix"].reshape(1, D_MODEL))
    x2o = _ffn(x1, p["g_pre_ffn"].reshape(1, D_MODEL), p["w_ffn_in"].astype(BF16),
               p["w_ffn_down"].astype(BF16), p["g_post_ffn"].reshape(1, D_MODEL))
    return x2o.reshape(b, s, D_MODEL)


def kernel(x, mem, g_pre_mix, w_in, cmp_pe_k, cmp_w1_k, cmp_w2_k, cmp_pe_v, cmp_w1_v, cmp_w2_v,
           ml_conv_w, ml_conv_b, ml_gate_b, ml_head_g, g_mem, w_mem_kv, w_proj_nsa, w_proj_ml,
           w_proj_mem, w_out, g_post_mix, g_pre_ffn, w_ffn_in, w_ffn_down, g_post_ffn):
    params = dict(
        g_pre_mix=g_pre_mix, w_in=w_in, cmp_pe_k=cmp_pe_k, cmp_w1_k=cmp_w1_k, cmp_w2_k=cmp_w2_k,
        cmp_pe_v=cmp_pe_v, cmp_w1_v=cmp_w1_v, cmp_w2_v=cmp_w2_v, ml_conv_w=ml_conv_w,
        ml_conv_b=ml_conv_b, ml_gate_b=ml_gate_b, ml_head_g=ml_head_g, g_mem=g_mem,
        w_mem_kv=w_mem_kv, w_proj_nsa=w_proj_nsa, w_proj_ml=w_proj_ml, w_proj_mem=w_proj_mem,
        w_out=w_out, g_post_mix=g_post_mix, g_pre_ffn=g_pre_ffn, w_ffn_in=w_ffn_in,
        w_ffn_down=w_ffn_down, g_post_ffn=g_post_ffn)
    depth = w_in.shape[0]
    for l in range(depth):
        x = _layer(x, mem, {k: v[l] for k, v in params.items()})
    return x
```

```python
import functools
import math

import jax
import jax.numpy as jnp
from jax import lax
from jax.experimental import pallas as pl
from jax.experimental.pallas import tpu as pltpu

F32 = jnp.float32
BF16 = jnp.bfloat16

D_MODEL = 1024
EPS = 1e-6
NEG_INF = -1e30
BIG = 1e30

NSA_HEADS = 16
NSA_GROUPS = 4
NSA_HPG = NSA_HEADS // NSA_GROUPS
NSA_DH = 64
NSA_SCALE = NSA_DH ** -0.5
CMP_BLOCK = 32
CMP_STRIDE = 16
CMP_HIDDEN = 128
SEL_BLOCK = 64
SEL_TOPK = 16
N_LOCAL_FORCED = 2
WINDOW = 512

ML_HEADS = 4
ML_DH = 128
ML_CHUNK = 128
CONV_WIDTH = 4

MEM_HEADS = 4
MEM_DH = 128
MEM_SCALE = MEM_DH ** -0.5

D_FF = -(-8 * D_MODEL // (3 * 256)) * 256

NSA_Q = NSA_HEADS * NSA_DH
NSA_KV = NSA_GROUPS * NSA_DH
ML_W = ML_HEADS * ML_DH
MEM_W = MEM_HEADS * MEM_DH

_OFF_Q = 0
_OFF_KV = _OFF_Q + NSA_Q
_OFF_GNSA = _OFF_KV + 6 * NSA_KV
_OFF_QKVML = _OFF_GNSA + 3 * NSA_HEADS
_OFF_IF = _OFF_QKVML + 3 * ML_W
_OFF_OML = _OFF_IF + 2 * ML_HEADS
_OFF_QMEM = _OFF_OML + ML_W
_OFF_GMERGE = _OFF_QMEM + MEM_W
_IN_WIDTH = _OFF_GMERGE + 3 * D_MODEL

ZB_GMERGE = 0
ZB_KSLC = ZB_GMERGE + 3 * D_MODEL
ZB_KWIN = ZB_KSLC + NSA_KV
ZB_QKVML = ZB_KWIN + NSA_KV
ZB_OML = ZB_QKVML + 3 * ML_W
ZB_QMEM = ZB_OML + ML_W
ZB_WIDTH = ZB_QMEM + MEM_W
KVC_WIDTH = 2 * NSA_KV
ZT_Q = 0
ZT_VSLC = ZT_Q + NSA_Q
ZT_VWIN = ZT_VSLC + NSA_KV
ZT_ROWS = ZT_VWIN + NSA_KV
ZG_WIDTH = 128
ZG_GNSA = 0
ZG_IF = 3 * NSA_HEADS

LANE = 128
SLAB_ROWS = 8
VMEM_LIMIT = 48 * 1024 * 1024
LOG2E = math.log2(math.e)

_NT = (((1,), (1,)), ((), ()))
_TN = (((0,), (0,)), ((), ()))


def _cparams(*sem):
    return pltpu.CompilerParams(dimension_semantics=sem, vmem_limit_bytes=VMEM_LIMIT)


def _rms(x, g):
    return x * lax.rsqrt(jnp.mean(x * x, axis=-1, keepdims=True) + EPS) * g


def _sigmoid(x):
    return 1.0 / (1.0 + jnp.exp(-x))


def _silu(x):
    return x * _sigmoid(x)


def _log_sigmoid(x):
    return jnp.minimum(x, 0.0) - jnp.log(1.0 + jnp.exp(-jnp.abs(x)))


def _inproj_kernel(x_ref, g_ref, wt_ref, wn_ref, wg_ref, wgt_ref, wc_ref,
                   zt_ref, zb_ref, zg_ref, zgt_ref, kvc_ref, h_ref, *, nt):
    j = pl.program_id(1)

    @pl.when(j == 0)
    def _():
        hb = _rms(x_ref[...], g_ref[...]).astype(BF16)
        h_ref[...] = hb
        zg_ref[...] = jnp.dot(hb, wg_ref[...], preferred_element_type=F32)
        zgt_ref[...] = lax.dot_general(wgt_ref[...], hb, _NT, preferred_element_type=F32)
        kvc_ref[...] = jnp.dot(hb, wc_ref[...], preferred_element_type=F32).astype(BF16)

    @pl.when(j < nt)
    def _():
        zt_ref[...] = lax.dot_general(wt_ref[...], h_ref[...], _NT,
                                      preferred_element_type=F32).astype(BF16)

    @pl.when(j >= nt)
    def _():
        zb_ref[...] = jnp.dot(h_ref[...], wn_ref[...], preferred_element_type=F32).astype(BF16)


def _inproj(x2, g, wt, wn, wg, wgt, wc, b, s, tm=1024, nt=2, nn=4):
    t = x2.shape[0]
    tr = ZT_ROWS // nt
    tn = ZB_WIDTH // nn
    assert tr * nt == ZT_ROWS and tn * nn == ZB_WIDTH and tn % LANE == 0 and s % tm == 0
    spb = s // tm
    kern = functools.partial(_inproj_kernel, nt=nt)
    return pl.pallas_call(
        kern,
        grid=(t // tm, nt + nn),
        in_specs=[
            pl.BlockSpec((tm, D_MODEL), lambda i, j: (i, 0)),
            pl.BlockSpec((1, D_MODEL), lambda i, j: (0, 0)),
            pl.BlockSpec((tr, D_MODEL), lambda i, j: (jnp.minimum(j, nt - 1), 0)),
            pl.BlockSpec((D_MODEL, tn), lambda i, j: (0, jnp.maximum(j - nt, 0))),
            pl.BlockSpec((D_MODEL, ZG_WIDTH), lambda i, j: (0, 0)),
            pl.BlockSpec((ZG_WIDTH, D_MODEL), lambda i, j: (0, 0)),
            pl.BlockSpec((D_MODEL, KVC_WIDTH), lambda i, j: (0, 0)),
        ],
        out_specs=[
            pl.BlockSpec((None, tr, tm), lambda i, j: (i // spb, jnp.minimum(j, nt - 1), i % spb)),
            pl.BlockSpec((tm, tn), lambda i, j: (i, jnp.maximum(j - nt, 0))),
            pl.BlockSpec((tm, ZG_WIDTH), lambda i, j: (i, 0)),
            pl.BlockSpec((None, ZG_WIDTH, tm), lambda i, j: (i // spb, 0, i % spb)),
            pl.BlockSpec((tm, KVC_WIDTH), lambda i, j: (i, 0)),
        ],
        out_shape=[
            jax.ShapeDtypeStruct((b, ZT_ROWS, s), BF16),
            jax.ShapeDtypeStruct((t, ZB_WIDTH), BF16),
            jax.ShapeDtypeStruct((t, ZG_WIDTH), F32),
            jax.ShapeDtypeStruct((b, ZG_WIDTH, s), F32),
            jax.ShapeDtypeStruct((t, KVC_WIDTH), BF16),
        ],
        scratch_shapes=[pltpu.VMEM((tm, D_MODEL), BF16)],
        compiler_params=_cparams("parallel", "arbitrary"),
        name="inproj",
    )(x2, g, wt, wn, wg, wgt, wc)


def _compress_kernel(a_ref, pe_ref, w1_ref, wp_ref, w2k_ref, w2vt_ref, kc_ref, vct_ref):
    n_rows = a_ref.shape[0]
    for kind in range(2):
        pe = pe_ref[kind]
        c = jnp.dot(pe, w1_ref[kind], preferred_element_type=F32)[0:1, :]
        for pair in range(NSA_GROUPS // 2):
            acc = jnp.zeros((n_rows, 4 * CMP_HIDDEN), F32)
            for l in range(CMP_STRIDE):
                c0 = l * KVC_WIDTH + kind * NSA_KV + pair * LANE
                acc = acc + jnp.dot(a_ref[:, c0:c0 + LANE], wp_ref[kind, l],
                                    preferred_element_type=F32)
            for gi in range(2):
                g = 2 * pair + gi
                p = acc[:, gi * CMP_HIDDEN:(gi + 1) * CMP_HIDDEN]
                q = acc[:, (2 + gi) * CMP_HIDDEN:(3 + gi) * CMP_HIDDEN]
                qs = pltpu.roll(q, shift=n_rows - 1, axis=0)
                hid = _silu(p + qs + c).astype(BF16)
                if kind == 0:
                    o = jnp.dot(hid, w2k_ref[...], preferred_element_type=F32)
                    kc_ref[:, g * NSA_DH:(g + 1) * NSA_DH] = o.astype(BF16)
                else:
                    ot = lax.dot_general(w2vt_ref[...], hid, _NT, preferred_element_type=F32)
                    vct_ref[g * NSA_DH:(g + 1) * NSA_DH, :] = ot.astype(BF16)


def _compress(a, pe, w1, wp, w2k, w2vt):
    b, n_rows, width = a.shape
    half = (CMP_BLOCK // 2) * NSA_DH
    return pl.pallas_call(
        _compress_kernel,
        grid=(b,),
        in_specs=[
            pl.BlockSpec((None, n_rows, width), lambda i: (i, 0, 0)),
            pl.BlockSpec((2, 8, 2 * half), lambda i: (0, 0, 0)),
            pl.BlockSpec((2, 2 * half, CMP_HIDDEN), lambda i: (0, 0, 0)),
            pl.BlockSpec((2, CMP_STRIDE, LANE, 4 * CMP_HIDDEN), lambda i: (0, 0, 0, 0)),
            pl.BlockSpec((CMP_HIDDEN, NSA_DH), lambda i: (0, 0)),
            pl.BlockSpec((NSA_DH, CMP_HIDDEN), lambda i: (0, 0)),
        ],
        out_specs=[
            pl.BlockSpec((None, n_rows, NSA_KV), lambda i: (i, 0, 0)),
            pl.BlockSpec((None, NSA_KV, n_rows), lambda i: (i, 0, 0)),
        ],
        out_shape=[
            jax.ShapeDtypeStruct((b, n_rows, NSA_KV), BF16),
            jax.ShapeDtypeStruct((b, NSA_KV, n_rows), BF16),
        ],
        compiler_params=_cparams("parallel"),
        name="compress",
    )(a, pe, w1, wp, w2k, w2vt)


def _pad_query(qt, g):
    z = jnp.zeros_like(qt)
    return jnp.concatenate([qt, z] if g % 2 == 0 else [z, qt], axis=0)


def _cmpsel_kernel(qt_ref, kc_ref, vct_ref, ocmpt_ref, selb_ref, *, tq, n_sel, topk):
    nc = kc_ref.shape[0]
    t0 = pl.program_id(1) * tq

    n_t = lax.broadcasted_iota(jnp.int32, (nc, tq), 0)
    t_t = t0 + lax.broadcasted_iota(jnp.int32, (nc, tq), 1)
    mask_t = (n_t * CMP_STRIDE + (CMP_BLOCK - 1)) <= t_t

    j_m = lax.broadcasted_iota(jnp.int32, (n_sel, nc), 0) * SEL_BLOCK
    c_m = lax.broadcasted_iota(jnp.int32, (n_sel, nc), 1) * CMP_STRIDE
    ov = jnp.minimum(c_m + CMP_BLOCK, j_m + SEL_BLOCK) - jnp.maximum(c_m, j_m)
    map_t = jnp.maximum(ov, 0).astype(F32) * (1.0 / CMP_BLOCK)

    j_s = lax.broadcasted_iota(jnp.int32, (n_sel, tq), 0)
    qblk = (t0 + lax.broadcasted_iota(jnp.int32, (n_sel, tq), 1)) // SEL_BLOCK
    rel = qblk - j_s
    causal = rel >= 0
    forced = causal & ((j_s == 0) | (rel < N_LOCAL_FORCED))

    for g in range(NSA_GROUPS):
        pair = g // 2
        kcp = kc_ref[:, pair * LANE:(pair + 1) * LANE]
        vct = vct_ref[g * NSA_DH:(g + 1) * NSA_DH, :]
        psum_t = jnp.zeros((nc, tq), F32)
        sts = [jnp.dot(kcp, _pad_query(qt_ref[(g * NSA_HPG + h) * NSA_DH:(g * NSA_HPG + h + 1) * NSA_DH, :], g),
                       preferred_element_type=F32) for h in range(NSA_HPG)]
        for h in range(NSA_HPG):
            hh = g * NSA_HPG + h
            st = jnp.where(mask_t, sts[h], NEG_INF)
            et = jnp.where(mask_t, jnp.exp2(st - jnp.max(st, axis=0, keepdims=True)), 0.0)
            lt = jnp.sum(et, axis=0, keepdims=True)
            pt = et * (1.0 / jnp.where(lt > 0.0, lt, 1.0))
            psum_t = psum_t + pt
            ot = jnp.dot(vct, pt.astype(BF16), preferred_element_type=F32)
            ocmpt_ref[hh * NSA_DH:(hh + 1) * NSA_DH, :] = ot.astype(BF16)
        imp_t = jnp.dot(map_t, psum_t, preferred_element_type=F32,
                        precision=lax.Precision.HIGHEST)
        score = jnp.where(forced, BIG, jnp.where(causal, imp_t, NEG_INF))
        rank = jnp.zeros((n_sel, tq), jnp.int32)
        for jp in range(n_sel):
            row = score[jp:jp + 1, :]
            before = (row > score) | ((row == score) & (j_s > jp))
            rank = rank + before.astype(jnp.int32)
        chosen = (rank < topk) & (score > 0.5 * NEG_INF)
        selb_ref[g * n_sel:(g + 1) * n_sel, :] = jnp.where(chosen, 0.0, NEG_INF)


def _cmpsel(zt, kc, vct, tq=256):
    b, _, s = zt.shape
    n_sel = s // SEL_BLOCK
    topk = min(SEL_TOPK, n_sel)
    nc = kc.shape[1]
    kern = functools.partial(_cmpsel_kernel, tq=tq, n_sel=n_sel, topk=topk)
    return pl.pallas_call(
        kern,
        grid=(b, s // tq),
        in_specs=[
            pl.BlockSpec((None, NSA_Q, tq), lambda i, j: (i, ZT_Q // NSA_Q, j)),
            pl.BlockSpec((None, nc, NSA_KV), lambda i, j: (i, 0, 0)),
            pl.BlockSpec((None, NSA_KV, nc), lambda i, j: (i, 0, 0)),
        ],
        out_specs=[
            pl.BlockSpec((None, NSA_Q, tq), lambda i, j: (i, 0, j)),
            pl.BlockSpec((None, NSA_GROUPS * n_sel, tq), lambda i, j: (i, 0, j)),
        ],
        out_shape=[
            jax.ShapeDtypeStruct((b, NSA_Q, s), BF16),
            jax.ShapeDtypeStruct((b, NSA_GROUPS * n_sel, s), F32),
        ],
        compiler_params=_cparams("parallel", "parallel"),
        name="cmpsel",
    )(zt, kc, vct)


def _nsa_kernel(qt_ref, ks_ref, kw_ref, vst_ref, vwt_ref, selb_ref, ocmpt_ref, gt_ref, y_ref,
                yt_scr, qp_scr, m_scr, acc_scr, st_scr, mx_scr, bias_scr, *, tq, n_sel, look):
    tk = tq
    nw = WINDOW // tk
    sel_per_tile = tk // SEL_BLOCK
    den_rows = 16
    i = pl.program_id(1)
    d0 = (lax.broadcasted_iota(jnp.int32, (tk, tq), 1)
          - lax.broadcasted_iota(jnp.int32, (tk, tq), 0))
    caus = jnp.where(d0 >= 0, 0.0, NEG_INF)
    lowb = jnp.where(d0 < 0, 0.0, NEG_INF)
    ones_rows = jnp.ones((den_rows, tk), BF16)
    gates = _sigmoid(gt_ref[...])

    for hh in range(NSA_HEADS):
        qp_scr[hh] = _pad_query(qt_ref[hh * NSA_DH:(hh + 1) * NSA_DH, :], hh // NSA_HPG)
    m_scr[...] = jnp.full(m_scr.shape, NEG_INF, F32)
    acc_scr[...] = jnp.zeros(acc_scr.shape, F32)

    def step(br, k_ref, vt_ref, kt, sel_on, mask):
        r0 = pl.multiple_of(kt * tk, tk)
        ktiles = [k_ref[pl.ds(r0, tk), pair * LANE:(pair + 1) * LANE] for pair in range(NSA_GROUPS // 2)]
        vt_augs = [jnp.concatenate([vt_ref[g * NSA_DH:(g + 1) * NSA_DH, pl.ds(r0, tk)], ones_rows], axis=0)
                   for g in range(NSA_GROUPS)]
        for g in range(NSA_GROUPS):
            if sel_on:
                row0 = g * n_sel + kt * sel_per_tile
                slab = selb_ref[pl.ds(pl.multiple_of((row0 // SLAB_ROWS) * SLAB_ROWS, SLAB_ROWS), SLAB_ROWS), :]
                for jb in range(sel_per_tile):
                    row = slab[jb:jb + 1, :]
                    for part in range(1, SLAB_ROWS // sel_per_tile):
                        cand = slab[part * sel_per_tile + jb:part * sel_per_tile + jb + 1, :]
                        row = jnp.where(row0 % SLAB_ROWS == part * sel_per_tile, cand, row)
                    rows = jnp.broadcast_to(row, (SEL_BLOCK, tq))
                    if mask is not None:
                        rows = rows + mask[jb * SEL_BLOCK:(jb + 1) * SEL_BLOCK, :]
                    bias_scr[g, jb * SEL_BLOCK:(jb + 1) * SEL_BLOCK, :] = rows
            elif mask is not None and g == 0:
                bias_scr[0] = mask

        def qk(hh):
            g = hh // NSA_HPG
            st = jnp.dot(ktiles[g // 2], qp_scr[hh], preferred_element_type=F32)
            if sel_on:
                st = st + bias_scr[g]
            elif mask is not None:
                st = st + bias_scr[0]
            st_scr[hh] = st
            mx_scr[hh] = jnp.max(st, axis=0, keepdims=True)

        def softmax_pv(hh):
            vt_aug = vt_augs[hh // NSA_HPG]
            m = m_scr[br, hh]
            m_new = jnp.maximum(m, mx_scr[hh])
            alpha = jnp.exp2(m - m_new)
            p = jnp.exp2(st_scr[hh] - m_new).astype(BF16)
            acc_scr[br, hh] = alpha * acc_scr[br, hh] + jnp.dot(vt_aug, p, preferred_element_type=F32)
            m_scr[br, hh] = m_new

        for hh in range(look):
            qk(hh)
        for hh in range(NSA_HEADS):
            if hh + look < NSA_HEADS:
                qk(hh + look)
            softmax_pv(hh)

    def slc_body(kt, c):
        step(0, ks_ref, vst_ref, kt, True, None)
        return c

    lax.fori_loop(0, i, slc_body, 0)
    step(0, ks_ref, vst_ref, i, True, caus)

    def win_low(_, c):
        step(1, kw_ref, vwt_ref, i - nw, False, lowb)
        return c

    def win_mid(kt, c):
        step(1, kw_ref, vwt_ref, kt, False, None)
        return c

    lax.fori_loop(0, (i >= nw).astype(jnp.int32), win_low, 0)
    lax.fori_loop(jnp.maximum(i - nw + 1, 0), i, win_mid, 0)
    step(1, kw_ref, vwt_ref, i, False, caus)

    for hh in range(NSA_HEADS):
        acc_s = acc_scr[0, hh]
        acc_w = acc_scr[1, hh]
        o_slc = acc_s[0:NSA_DH] * (1.0 / acc_s[NSA_DH:NSA_DH + 1])
        o_win = acc_w[0:NSA_DH] * (1.0 / acc_w[NSA_DH:NSA_DH + 1])
        o_cmp = ocmpt_ref[hh * NSA_DH:(hh + 1) * NSA_DH, :].astype(F32)
        gc = ZG_GNSA + 3 * hh
        yt_scr[hh * NSA_DH:(hh + 1) * NSA_DH, :] = (
            gates[gc:gc + 1] * o_cmp + gates[gc + 1:gc + 2] * o_slc + gates[gc + 2:gc + 3] * o_win)

    y_ref[...] = yt_scr[...].T.astype(BF16)


def _nsa(zt, zb3, selb, ocmpt, zgt, tq=256, look=5):
    b, _, s = zt.shape
    n_sel = s // SEL_BLOCK
    tk = tq
    den_rows = 16
    assert WINDOW % tq == 0 and tq % SEL_BLOCK == 0 and SLAB_ROWS % (tk // SEL_BLOCK) == 0
    kern = functools.partial(_nsa_kernel, tq=tq, n_sel=n_sel, look=look)
    return pl.pallas_call(
        kern,
        grid=(b, s // tq),
        in_specs=[
            pl.BlockSpec((None, NSA_Q, tq), lambda bi, j: (bi, ZT_Q // NSA_Q, j)),
            pl.BlockSpec((None, s, NSA_KV), lambda bi, j: (bi, 0, ZB_KSLC // NSA_KV)),
            pl.BlockSpec((None, s, NSA_KV), lambda bi, j: (bi, 0, ZB_KWIN // NSA_KV)),
            pl.BlockSpec((None, NSA_KV, s), lambda bi, j: (bi, ZT_VSLC // NSA_KV, 0)),
            pl.BlockSpec((None, NSA_KV, s), lambda bi, j: (bi, ZT_VWIN // NSA_KV, 0)),
            pl.BlockSpec((None, NSA_GROUPS * n_sel, tq), lambda bi, j: (bi, 0, j)),
            pl.BlockSpec((None, NSA_Q, tq), lambda bi, j: (bi, 0, j)),
            pl.BlockSpec((None, ZG_WIDTH, tq), lambda bi, j: (bi, 0, j)),
        ],
        out_specs=pl.BlockSpec((None, tq, NSA_Q), lambda bi, j: (bi, j, 0)),
        out_shape=jax.ShapeDtypeStruct((b, s, NSA_Q), BF16),
        scratch_shapes=[
            pltpu.VMEM((NSA_Q, tq), F32),
            pltpu.VMEM((NSA_HEADS, 2 * NSA_DH, tq), BF16),
            pltpu.VMEM((2, NSA_HEADS, 1, tq), F32),
            pltpu.VMEM((2, NSA_HEADS, NSA_DH + den_rows, tq), F32),
            pltpu.VMEM((NSA_HEADS, tk, tq), F32),
            pltpu.VMEM((NSA_HEADS, 1, tq), F32),
            pltpu.VMEM((NSA_GROUPS, tk, tq), F32),
        ],
        compiler_params=_cparams("parallel", "arbitrary"),
        name="nsa",
    )(zt, zb3, zb3, zt, zt, selb, ocmpt, zgt)


def _mlstm_kernel(q_ref, k_ref, v_ref, o_ref, zg_ref, cw_ref, cb_ref, gb_ref, hg_ref, y_ref,
                  xbuf, c_st, n_st, m_st, *, nb):
    L = ML_CHUNK
    pad = 8
    @pl.when(pl.program_id(1) == 0)
    def _():
        xbuf[:, 0:pad, :] = jnp.zeros((nb, pad, 2 * ML_W), F32)
        c_st[...] = jnp.zeros_like(c_st)
        n_st[...] = jnp.zeros_like(n_st)
        m_st[...] = jnp.zeros_like(m_st)

    row = lax.broadcasted_iota(jnp.int32, (L, L), 0)
    col = lax.broadcasted_iota(jnp.int32, (L, L), 1)
    tril = row >= col
    tril_f = jnp.where(tril, 1.0, 0.0)
    streams = [(bb, h) for bb in range(nb) for h in range(ML_HEADS)]

    qks = []
    for bb in range(nb):
        xbuf[bb, pad:pad + L, 0:ML_W] = q_ref[bb].astype(F32)
        xbuf[bb, pad:pad + L, ML_W:2 * ML_W] = k_ref[bb].astype(F32)
        conv = jnp.zeros((L, 2 * ML_W), F32) + cb_ref[...]
        for j in range(CONV_WIDTH):
            conv = conv + xbuf[bb, pl.ds(pad - (CONV_WIDTH - 1) + j, L), :] * cw_ref[j:j + 1, :]
        xbuf[bb, 0:pad, :] = xbuf[bb, L:L + pad, :]
        qks.append(_silu(conv))

    qs = {(bb, h): qks[bb][:, h * ML_DH:(h + 1) * ML_DH] for bb, h in streams}
    ks = {(bb, h): qks[bb][:, ML_W + h * ML_DH:ML_W + (h + 1) * ML_DH] * (ML_DH ** -0.5)
          for bb, h in streams}
    qbs = {s: qs[s].astype(BF16) for s in streams}
    vbs = {(bb, h): v_ref[bb, :, h * ML_DH:(h + 1) * ML_DH] for bb, h in streams}
    c_olds = {(bb, h): c_st[bb * ML_HEADS + h] for bb, h in streams}
    n_olds = {(bb, h): n_st[bb, h:h + 1, :] for bb, h in streams}
    s_qk = {s: lax.dot_general(qbs[s], ks[s].astype(BF16), _NT, preferred_element_type=F32)
            for s in streams}
    cqs = {s: lax.dot_general(qbs[s], c_olds[s].astype(BF16), _NT, preferred_element_type=F32)
           for s in streams}

    g = []
    for bb in range(nb):
        gates = zg_ref[bb] + gb_ref[...]
        bcum = jnp.dot(tril_f, _log_sigmoid(gates), preferred_element_type=F32,
                       precision=lax.Precision.HIGHEST)
        b_al = pltpu.roll(bcum, shift=ZG_WIDTH - ML_HEADS, axis=1)
        m_prev = m_st[bb, 0:1, :]
        b_end = b_al[L - 1:L, :]
        inter = b_al + m_prev
        wlog = b_end - b_al + gates
        m_new = jnp.maximum(b_end + m_prev, jnp.max(wlog, axis=0, keepdims=True))
        ws = jnp.exp(wlog - m_new)
        decay = jnp.exp(b_end + m_prev - m_new)
        m_st[bb, 0:1, :] = m_new
        r_t = (gates - b_al).T
        g.append((b_al, inter, ws, decay, r_t))

    for bb, h in streams:
        c = ZG_IF + h
        _, _, ws, decay, _ = g[bb]
        kw = ks[bb, h] * ws[:, c:c + 1]
        upd = lax.dot_general(vbs[bb, h], kw.astype(BF16), _TN, preferred_element_type=F32)
        c_st[bb * ML_HEADS + h] = decay[:, c:c + 1] * c_olds[bb, h] + upd
        n_st[bb, h:h + 1, :] = decay[:, c:c + 1] * n_olds[bb, h] + jnp.sum(kw, axis=0, keepdims=True)

    for bb, h in streams:
        c = ZG_IF + h
        b_al, inter, _, _, r_t = g[bb]
        dlog = jnp.where(tril, b_al[:, c:c + 1] + r_t[c:c + 1, :], -jnp.inf)
        inter_c = inter[:, c:c + 1]
        m_t = jnp.maximum(inter_c, jnp.max(dlog, axis=-1, keepdims=True))
        dw = jnp.exp(dlog - m_t)
        iw = jnp.exp(inter_c - m_t)
        sqk = s_qk[bb, h] * dw
        num = jnp.dot(sqk.astype(BF16), vbs[bb, h], preferred_element_type=F32) + iw * cqs[bb, h]
        den = (jnp.sum(sqk, axis=-1, keepdims=True)
               + iw * jnp.sum(qs[bb, h] * n_olds[bb, h], axis=-1, keepdims=True))
        hs = num / jnp.maximum(jnp.abs(den), jnp.exp(-m_t))
        hn = hs * lax.rsqrt(jnp.mean(hs * hs, axis=-1, keepdims=True) + EPS)
        hn = hn * hg_ref[:, h * ML_DH:(h + 1) * ML_DH]
        og = _sigmoid(o_ref[bb, :, h * ML_DH:(h + 1) * ML_DH].astype(F32))
        y_ref[bb, :, h * ML_DH:(h + 1) * ML_DH] = (og * hn).astype(BF16)


def _mlstm(zb3, zg3, conv_w, conv_b, gate_b, head_g, nb=1):
    b, s, _ = zb3.shape
    L = ML_CHUNK
    qb = ZB_QKVML // ML_W
    assert b % nb == 0

    def zspec(k):
        return pl.BlockSpec((nb, L, ML_W), lambda bi, c, k=k: (bi, c, k))

    return pl.pallas_call(
        functools.partial(_mlstm_kernel, nb=nb),
        grid=(b // nb, s // L),
        in_specs=[
            zspec(qb), zspec(qb + 1), zspec(qb + 2), zspec(ZB_OML // ML_W),
            pl.BlockSpec((nb, L, ZG_WIDTH), lambda bi, c: (bi, c, 0)),
            pl.BlockSpec((CONV_WIDTH, 2 * ML_W), lambda bi, c: (0, 0)),
            pl.BlockSpec((1, 2 * ML_W), lambda bi, c: (0, 0)),
            pl.BlockSpec((1, ZG_WIDTH), lambda bi, c: (0, 0)),
            pl.BlockSpec((1, ML_W), lambda bi, c: (0, 0)),
        ],
        out_specs=pl.BlockSpec((nb, L, ML_W), lambda bi, c: (bi, c, 0)),
        out_shape=jax.ShapeDtypeStruct((b, s, ML_W), BF16),
        scratch_shapes=[
            pltpu.VMEM((nb, 8 + L, 2 * ML_W), F32),
            pltpu.VMEM((nb * ML_HEADS, ML_DH, ML_DH), F32),
            pltpu.VMEM((nb, 8, ML_DH), F32),
            pltpu.VMEM((nb, 8, LANE), F32),
        ],
        compiler_params=_cparams("parallel", "arbitrary"),
        name="mlstm",
    )(zb3, zb3, zb3, zb3, zg3, conv_w, conv_b, gate_b, head_g)


def _memkv_kernel(mem_ref, g_ref, w_ref, kv_ref):
    hb = _rms(mem_ref[...], g_ref[...]).astype(BF16)
    kv_ref[...] = jnp.dot(hb, w_ref[...], preferred_element_type=F32).astype(BF16)


def _memkv(mem, g, wb):
    b, m, _ = mem.shape
    return pl.pallas_call(
        _memkv_kernel,
        grid=(b,),
        in_specs=[
            pl.BlockSpec((None, m, D_MODEL), lambda i: (i, 0, 0)),
            pl.BlockSpec((1, D_MODEL), lambda i: (0, 0)),
            pl.BlockSpec((D_MODEL, 2 * MEM_W), lambda i: (0, 0)),
        ],
        out_specs=pl.BlockSpec((None, m, 2 * MEM_W), lambda i: (i, 0, 0)),
        out_shape=jax.ShapeDtypeStruct((b, m, 2 * MEM_W), BF16),
        compiler_params=_cparams("parallel"),
        name="memkv",
    )(mem, g, wb)


def _memattn_kernel(q_ref, kv_ref, y_ref):
    for h in range(MEM_HEADS):
        q = q_ref[:, h * MEM_DH:(h + 1) * MEM_DH]
        k = kv_ref[:, h * MEM_DH:(h + 1) * MEM_DH]
        v = kv_ref[:, MEM_W + h * MEM_DH:MEM_W + (h + 1) * MEM_DH]
        s = lax.dot_general(q, k, _NT, preferred_element_type=F32) * MEM_SCALE
        e = jnp.exp(s - jnp.max(s, axis=-1, keepdims=True))
        p = e / jnp.sum(e, axis=-1, keepdims=True)
        o = jnp.dot(p.astype(BF16), v, preferred_element_type=F32)
        y_ref[:, h * MEM_DH:(h + 1) * MEM_DH] = o.astype(BF16)


def _memattn(zb3, kvm, tq=512):
    b, s, _ = zb3.shape
    m = kvm.shape[1]
    return pl.pallas_call(
        _memattn_kernel,
        grid=(b, s // tq),
        in_specs=[
            pl.BlockSpec((None, tq, MEM_W), lambda i, j: (i, j, ZB_QMEM // MEM_W)),
            pl.BlockSpec((None, m, 2 * MEM_W), lambda i, j: (i, 0, 0)),
        ],
        out_specs=pl.BlockSpec((None, tq, MEM_W), lambda i, j: (i, j, 0)),
        out_shape=jax.ShapeDtypeStruct((b, s, MEM_W), BF16),
        compiler_params=_cparams("parallel", "parallel"),
        name="memattn",
    )(zb3, kvm)


def _merge_kernel(x_ref, yn_ref, yl_ref, ym_ref, g0_ref, g1_ref, g2_ref,
                  wn_ref, wl_ref, wm_ref, wo_ref, gp_ref, out_ref):
    y = _sigmoid(g0_ref[...].astype(F32)) * jnp.dot(yn_ref[...], wn_ref[...], preferred_element_type=F32)
    y = y + _sigmoid(g1_ref[...].astype(F32)) * jnp.dot(yl_ref[...], wl_ref[...], preferred_element_type=F32)
    y = y + _sigmoid(g2_ref[...].astype(F32)) * jnp.dot(ym_ref[...], wm_ref[...], preferred_element_type=F32)
    u = jnp.dot(y.astype(BF16), wo_ref[...], preferred_element_type=F32)
    out_ref[...] = x_ref[...] + _rms(u, gp_ref[...])


def _merge(x2, yn, yl, ym, zb, wn, wl, wm, wo, gp, tm=512):
    t = x2.shape[0]
    gm = ZB_GMERGE // D_MODEL

    def const(shape):
        return pl.BlockSpec(shape, lambda i: (0, 0))

    return pl.pallas_call(
        _merge_kernel,
        grid=(t // tm,),
        in_specs=[
            pl.BlockSpec((tm, D_MODEL), lambda i: (i, 0)),
            pl.BlockSpec((tm, NSA_Q), lambda i: (i, 0)),
            pl.BlockSpec((tm, ML_W), lambda i: (i, 0)),
            pl.BlockSpec((tm, MEM_W), lambda i: (i, 0)),
            pl.BlockSpec((tm, D_MODEL), lambda i: (i, gm)),
            pl.BlockSpec((tm, D_MODEL), lambda i: (i, gm + 1)),
            pl.BlockSpec((tm, D_MODEL), lambda i: (i, gm + 2)),
            const((NSA_Q, D_MODEL)), const((ML_W, D_MODEL)), const((MEM_W, D_MODEL)),
            const((D_MODEL, D_MODEL)), const((1, D_MODEL)),
        ],
        out_specs=pl.BlockSpec((tm, D_MODEL), lambda i: (i, 0)),
        out_shape=jax.ShapeDtypeStruct((t, D_MODEL), F32),
        compiler_params=_cparams("parallel"),
        name="merge",
    )(x2, yn, yl, ym, zb, zb, zb, wn, wl, wm, wo, gp)


def _ffn_kernel(x_ref, gpre_ref, wg_ref, wu_ref, wd_ref, gpost_ref, out_ref, h_ref, acc_ref):
    j = pl.program_id(1)

    @pl.when(j == 0)
    def _():
        h_ref[...] = _rms(x_ref[...], gpre_ref[...]).astype(BF16)
        acc_ref[...] = jnp.zeros_like(acc_ref)

    h = h_ref[...]
    gate = jnp.dot(h, wg_ref[...], preferred_element_type=F32)
    up = jnp.dot(h, wu_ref[...], preferred_element_type=F32)
    act = (_silu(gate) * up).astype(BF16)
    acc_ref[...] += jnp.dot(act, wd_ref[...], preferred_element_type=F32)

    @pl.when(j == pl.num_programs(1) - 1)
    def _():
        out_ref[...] = x_ref[...] + _rms(acc_ref[...], gpost_ref[...])


def _ffn(x2, gpre, w_in, w_down, gpost, tm=1024, nf=2):
    t = x2.shape[0]
    tf = D_FF // nf
    assert tf % LANE == 0
    return pl.pallas_call(
        _ffn_kernel,
        grid=(t // tm, nf),
        in_specs=[
            pl.BlockSpec((tm, D_MODEL), lambda i, j: (i, 0)),
            pl.BlockSpec((1, D_MODEL), lambda i, j: (0, 0)),
            pl.BlockSpec((D_MODEL, tf), lambda i, j: (0, j)),
            pl.BlockSpec((D_MODEL, tf), lambda i, j: (0, nf + j)),
            pl.BlockSpec((tf, D_MODEL), lambda i, j: (j, 0)),
            pl.BlockSpec((1, D_MODEL), lambda i, j: (0, 0)),
        ],
        out_specs=pl.BlockSpec((tm, D_MODEL), lambda i, j: (i, 0)),
        out_shape=jax.ShapeDtypeStruct((t, D_MODEL), F32),
        scratch_shapes=[pltpu.VMEM((tm, D_MODEL), BF16), pltpu.VMEM((tm, D_MODEL), F32)],
        compiler_params=_cparams("parallel", "arbitrary"),
        name="ffn",
    )(x2, gpre, w_in, w_in, w_down, gpost)


def _layer(x, mem, p):
    b, s, _ = x.shape
    t = b * s
    x2 = x.reshape(t, D_MODEL)

    w_in = p["w_in"]
    kv0 = _OFF_KV

    def kv_cols(k):
        return w_in[:, kv0 + k * NSA_KV:kv0 + (k + 1) * NSA_KV]

    wn = jnp.concatenate([
        w_in[:, _OFF_GMERGE:_IN_WIDTH],
        kv_cols(2), kv_cols(4),
        w_in[:, _OFF_QKVML:_OFF_IF],
        w_in[:, _OFF_OML:_OFF_QMEM],
        w_in[:, _OFF_QMEM:_OFF_GMERGE],
    ], axis=1).astype(BF16)
    wt = jnp.concatenate([
        w_in[:, _OFF_Q:_OFF_KV] * (NSA_SCALE * LOG2E),
        kv_cols(3), kv_cols(5),
    ], axis=1).T.astype(BF16)
    wg = jnp.concatenate([
        w_in[:, _OFF_GNSA:_OFF_QKVML],
        w_in[:, _OFF_IF:_OFF_OML],
        jnp.zeros((D_MODEL, ZG_WIDTH - 3 * NSA_HEADS - 2 * ML_HEADS), F32),
    ], axis=1).astype(BF16)
    wc = w_in[:, kv0:kv0 + KVC_WIDTH].astype(BF16)
    zt, zb, zg, zgt, kvc = _inproj(x2, p["g_pre_mix"].reshape(1, D_MODEL), wt, wn, wg, wg.T, wc, b, s)
    zb3 = zb.reshape(b, s, ZB_WIDTH)
    zg3 = zg.reshape(b, s, ZG_WIDTH)

    n_rows = s // CMP_STRIDE
    half = (CMP_BLOCK // 2) * NSA_DH
    a = kvc.reshape(b, n_rows, CMP_STRIDE * KVC_WIDTH)
    pe = jnp.stack([p["cmp_pe_k"], p["cmp_pe_v"]]).reshape(2, 1, 2 * half)
    pe = jnp.pad(pe, ((0, 0), (0, 7), (0, 0))).astype(BF16)
    w1 = jnp.stack([p["cmp_w1_k"], p["cmp_w1_v"]])
    eye2 = jnp.eye(2, dtype=F32)
    wpa = jnp.einsum("ab,kldh->kladbh", eye2, w1[:, :CMP_STRIDE]).reshape(2, CMP_STRIDE, LANE, 2 * CMP_HIDDEN)
    wpb = jnp.einsum("ab,kldh->kladbh", eye2, w1[:, CMP_STRIDE:]).reshape(2, CMP_STRIDE, LANE, 2 * CMP_HIDDEN)
    wp = jnp.concatenate([wpa, wpb], axis=-1).astype(BF16)
    kc, vct = _compress(a, pe, w1.reshape(2, 2 * half, CMP_HIDDEN).astype(BF16), wp,
                        p["cmp_w2_k"].astype(BF16), p["cmp_w2_v"].T.astype(BF16))
    ocmpt, selb = _cmpsel(zt, kc, vct)
    y_nsa = _nsa(zt, zb3, selb, ocmpt, zgt)

    gate_b = jnp.zeros((1, ZG_WIDTH), F32).at[0, ZG_IF:ZG_IF + 2 * ML_HEADS].set(p["ml_gate_b"])
    y_ml = _mlstm(zb3, zg3, p["ml_conv_w"], p["ml_conv_b"].reshape(1, 2 * ML_W), gate_b,
                  p["ml_head_g"].reshape(1, ML_W))

    kvm = _memkv(mem, p["g_mem"].reshape(1, D_MODEL), p["w_mem_kv"].astype(BF16))
    y_mem = _memattn(zb3, kvm)

    x1 = _merge(x2, y_nsa.reshape(t, NSA_Q), y_ml.reshape(t, ML_W), y_mem.reshape(t, MEM_W), zb,
                p["w_proj_nsa"].astype(BF16), p["w_proj_ml"].astype(BF16),
                p["w_proj_mem"].astype(BF16), p["w_out"].astype(BF16),
                p["g_post_mix"].reshape(1, D_MODEL))
    x2o = _ffn(x1, p["g_pre_ffn"].reshape(1, D_MODEL), p["w_ffn_in"].astype(BF16),
               p["w_ffn_down"].astype(BF16), p["g_post_ffn"].reshape(1, D_MODEL))
    return x2o.reshape(b, s, D_MODEL)


def kernel(x, mem, g_pre_mix, w_in, cmp_pe_k, cmp_w1_k, cmp_w2_k, cmp_pe_v, cmp_w1_v, cmp_w2_v,
           ml_conv_w, ml_conv_b, ml_gate_b, ml_head_g, g_mem, w_mem_kv, w_proj_nsa, w_proj_ml,
           w_proj_mem, w_out, g_post_mix, g_pre_ffn, w_ffn_in, w_ffn_down, g_post_ffn):
    params = dict(
        g_pre_mix=g_pre_mix, w_in=w_in, cmp_pe_k=cmp_pe_k, cmp_w1_k=cmp_w1_k, cmp_w2_k=cmp_w2_k,
        cmp_pe_v=cmp_pe_v, cmp_w1_v=cmp_w1_v, cmp_w2_v=cmp_w2_v, ml_conv_w=ml_conv_w,
        ml_conv_b=ml_conv_b, ml_gate_b=ml_gate_b, ml_head_g=ml_head_g, g_mem=g_mem,
        w_mem_kv=w_mem_kv, w_proj_nsa=w_proj_nsa, w_proj_ml=w_proj_ml, w_proj_mem=w_proj_mem,
        w_out=w_out, g_post_mix=g_post_mix, g_pre_ffn=g_pre_ffn, w_ffn_in=w_ffn_in,
        w_ffn_down=w_ffn_down, g_post_ffn=g_post_ffn)
    depth = w_in.shape[0]
    for l in range(depth):
        x = _layer(x, mem, {k: v[l] for k, v in params.items()})
    return x
```

```python
import functools
import math

import jax
import jax.numpy as jnp
from jax import lax
from jax.experimental import pallas as pl
from jax.experimental.pallas import tpu as pltpu

F32 = jnp.float32
BF16 = jnp.bfloat16

D_MODEL = 1024
EPS = 1e-6
NEG_INF = -1e30
BIG = 1e30

NSA_HEADS = 16
NSA_GROUPS = 4
NSA_HPG = NSA_HEADS // NSA_GROUPS
NSA_DH = 64
NSA_SCALE = NSA_DH ** -0.5
CMP_BLOCK = 32
CMP_STRIDE = 16
CMP_HIDDEN = 128
SEL_BLOCK = 64
SEL_TOPK = 16
N_LOCAL_FORCED = 2
WINDOW = 512

ML_HEADS = 4
ML_DH = 128
ML_CHUNK = 128
CONV_WIDTH = 4

MEM_HEADS = 4
MEM_DH = 128
MEM_SCALE = MEM_DH ** -0.5

D_FF = -(-8 * D_MODEL // (3 * 256)) * 256

NSA_Q = NSA_HEADS * NSA_DH
NSA_KV = NSA_GROUPS * NSA_DH
ML_W = ML_HEADS * ML_DH
MEM_W = MEM_HEADS * MEM_DH

_OFF_Q = 0
_OFF_KV = _OFF_Q + NSA_Q
_OFF_GNSA = _OFF_KV + 6 * NSA_KV
_OFF_QKVML = _OFF_GNSA + 3 * NSA_HEADS
_OFF_IF = _OFF_QKVML + 3 * ML_W
_OFF_OML = _OFF_IF + 2 * ML_HEADS
_OFF_QMEM = _OFF_OML + ML_W
_OFF_GMERGE = _OFF_QMEM + MEM_W
_IN_WIDTH = _OFF_GMERGE + 3 * D_MODEL

ZB_GMERGE = 0
ZB_KSLC = ZB_GMERGE + 3 * D_MODEL
ZB_KWIN = ZB_KSLC + NSA_KV
ZB_QKVML = ZB_KWIN + NSA_KV
ZB_OML = ZB_QKVML + 3 * ML_W
ZB_QMEM = ZB_OML + ML_W
ZB_WIDTH = ZB_QMEM + MEM_W
KVC_WIDTH = 2 * NSA_KV
ZT_Q = 0
ZT_VSLC = ZT_Q + NSA_Q
ZT_VWIN = ZT_VSLC + NSA_KV
ZT_ROWS = ZT_VWIN + NSA_KV
ZG_WIDTH = 128
ZG_GNSA = 0
ZG_IF = 3 * NSA_HEADS

LANE = 128
SLAB_ROWS = 8
VMEM_LIMIT = 48 * 1024 * 1024
LOG2E = math.log2(math.e)

_NT = (((1,), (1,)), ((), ()))
_TN = (((0,), (0,)), ((), ()))


def _cparams(*sem):
    return pltpu.CompilerParams(dimension_semantics=sem, vmem_limit_bytes=VMEM_LIMIT)


def _rms(x, g):
    return x * lax.rsqrt(jnp.mean(x * x, axis=-1, keepdims=True) + EPS) * g


def _sigmoid(x):
    return 1.0 / (1.0 + jnp.exp(-x))


def _silu(x):
    return x * _sigmoid(x)


def _log_sigmoid(x):
    return jnp.minimum(x, 0.0) - jnp.log(1.0 + jnp.exp(-jnp.abs(x)))


def _wprep_kernel(w_ref, wn_ref, wt_ref, wg_ref, wc_ref):
    def rows(lo, hi):
        return w_ref[lo:hi, :]

    def kv_rows(k):
        return rows(_OFF_KV + k * NSA_KV, _OFF_KV + (k + 1) * NSA_KV)

    wn_ref[ZB_GMERGE:ZB_KSLC, :] = rows(_OFF_GMERGE, _IN_WIDTH).astype(BF16)
    wn_ref[ZB_KSLC:ZB_KWIN, :] = kv_rows(2).astype(BF16)
    wn_ref[ZB_KWIN:ZB_QKVML, :] = kv_rows(4).astype(BF16)
    wn_ref[ZB_QKVML:ZB_OML, :] = rows(_OFF_QKVML, _OFF_IF).astype(BF16)
    wn_ref[ZB_OML:ZB_WIDTH, :] = rows(_OFF_OML, _OFF_GMERGE).astype(BF16)
    wt_ref[ZT_Q:ZT_VSLC, :] = (rows(_OFF_Q, _OFF_KV) * (NSA_SCALE * LOG2E)).astype(BF16)
    wt_ref[ZT_VSLC:ZT_VWIN, :] = kv_rows(3).astype(BF16)
    wt_ref[ZT_VWIN:ZT_ROWS, :] = kv_rows(5).astype(BF16)
    n_g = 3 * NSA_HEADS
    n_if = 2 * ML_HEADS
    wg_ref[...] = jnp.zeros(wg_ref.shape, BF16)
    wg_ref[ZG_GNSA:ZG_GNSA + n_g, :] = rows(_OFF_GNSA, _OFF_QKVML).astype(BF16)
    wg_ref[ZG_IF:ZG_IF + n_if, :] = rows(_OFF_IF, _OFF_OML).astype(BF16)
    wc_ref[...] = rows(_OFF_KV, _OFF_KV + KVC_WIDTH).astype(BF16)


def _wprep(w_t, tl=128):
    assert ZB_QMEM == ZB_OML + ML_W and _OFF_QMEM == _OFF_OML + ML_W
    return pl.pallas_call(
        _wprep_kernel,
        grid=(D_MODEL // tl,),
        in_specs=[pl.BlockSpec((_IN_WIDTH, tl), lambda i: (0, i))],
        out_specs=[
            pl.BlockSpec((ZB_WIDTH, tl), lambda i: (0, i)),
            pl.BlockSpec((ZT_ROWS, tl), lambda i: (0, i)),
            pl.BlockSpec((ZG_WIDTH, tl), lambda i: (0, i)),
            pl.BlockSpec((KVC_WIDTH, tl), lambda i: (0, i)),
        ],
        out_shape=[
            jax.ShapeDtypeStruct((ZB_WIDTH, D_MODEL), BF16),
            jax.ShapeDtypeStruct((ZT_ROWS, D_MODEL), BF16),
            jax.ShapeDtypeStruct((ZG_WIDTH, D_MODEL), BF16),
            jax.ShapeDtypeStruct((KVC_WIDTH, D_MODEL), BF16),
        ],
        compiler_params=_cparams("parallel"),
        name="wprep",
    )(w_t)


def _inproj_kernel(x_ref, g_ref, wt_ref, wn_ref, wg_ref, wc_ref,
                   zt_ref, zb_ref, zg_ref, zgt_ref, kvc_ref, h_ref, *, nt):
    j = pl.program_id(1)

    @pl.when(j == 0)
    def _():
        hb = _rms(x_ref[...], g_ref[...]).astype(BF16)
        h_ref[...] = hb
        zg_ref[...] = lax.dot_general(hb, wg_ref[...], _NT, preferred_element_type=F32)
        zgt_ref[...] = lax.dot_general(wg_ref[...], hb, _NT, preferred_element_type=F32)
        kvc_ref[...] = lax.dot_general(hb, wc_ref[...], _NT, preferred_element_type=F32).astype(BF16)

    @pl.when(j < nt)
    def _():
        zt_ref[...] = lax.dot_general(wt_ref[...], h_ref[...], _NT,
                                      preferred_element_type=F32).astype(BF16)

    @pl.when(j >= nt)
    def _():
        zb_ref[...] = lax.dot_general(h_ref[...], wn_ref[...], _NT,
                                      preferred_element_type=F32).astype(BF16)


def _inproj(x2, g, wt, wn, wg, wc, b, s, tm=1024, nt=2, nn=4):
    t = x2.shape[0]
    tr = ZT_ROWS // nt
    tn = ZB_WIDTH // nn
    assert tr * nt == ZT_ROWS and tn * nn == ZB_WIDTH and tn % LANE == 0 and s % tm == 0
    spb = s // tm
    kern = functools.partial(_inproj_kernel, nt=nt)
    return pl.pallas_call(
        kern,
        grid=(t // tm, nt + nn),
        in_specs=[
            pl.BlockSpec((tm, D_MODEL), lambda i, j: (i, 0)),
            pl.BlockSpec((1, D_MODEL), lambda i, j: (0, 0)),
            pl.BlockSpec((tr, D_MODEL), lambda i, j: (jnp.minimum(j, nt - 1), 0)),
            pl.BlockSpec((tn, D_MODEL), lambda i, j: (jnp.maximum(j - nt, 0), 0)),
            pl.BlockSpec((ZG_WIDTH, D_MODEL), lambda i, j: (0, 0)),
            pl.BlockSpec((KVC_WIDTH, D_MODEL), lambda i, j: (0, 0)),
        ],
        out_specs=[
            pl.BlockSpec((None, tr, tm), lambda i, j: (i // spb, jnp.minimum(j, nt - 1), i % spb)),
            pl.BlockSpec((tm, tn), lambda i, j: (i, jnp.maximum(j - nt, 0))),
            pl.BlockSpec((tm, ZG_WIDTH), lambda i, j: (i, 0)),
            pl.BlockSpec((None, ZG_WIDTH, tm), lambda i, j: (i // spb, 0, i % spb)),
            pl.BlockSpec((tm, KVC_WIDTH), lambda i, j: (i, 0)),
        ],
        out_shape=[
            jax.ShapeDtypeStruct((b, ZT_ROWS, s), BF16),
            jax.ShapeDtypeStruct((t, ZB_WIDTH), BF16),
            jax.ShapeDtypeStruct((t, ZG_WIDTH), F32),
            jax.ShapeDtypeStruct((b, ZG_WIDTH, s), F32),
            jax.ShapeDtypeStruct((t, KVC_WIDTH), BF16),
        ],
        scratch_shapes=[pltpu.VMEM((tm, D_MODEL), BF16)],
        compiler_params=_cparams("parallel", "arbitrary"),
        name="inproj",
    )(x2, g, wt, wn, wg, wc)


def _compress_kernel(a_ref, pe_ref, w1_ref, wp_ref, w2k_ref, w2vt_ref, kc_ref, vct_ref):
    n_rows = a_ref.shape[0]
    for kind in range(2):
        pe = pe_ref[kind]
        c = jnp.dot(pe, w1_ref[kind], preferred_element_type=F32)[0:1, :]
        for pair in range(NSA_GROUPS // 2):
            acc = jnp.zeros((n_rows, 4 * CMP_HIDDEN), F32)
            for l in range(CMP_STRIDE):
                c0 = l * KVC_WIDTH + kind * NSA_KV + pair * LANE
                acc = acc + jnp.dot(a_ref[:, c0:c0 + LANE], wp_ref[kind, l],
                                    preferred_element_type=F32)
            for gi in range(2):
                g = 2 * pair + gi
                p = acc[:, gi * CMP_HIDDEN:(gi + 1) * CMP_HIDDEN]
                q = acc[:, (2 + gi) * CMP_HIDDEN:(3 + gi) * CMP_HIDDEN]
                qs = pltpu.roll(q, shift=n_rows - 1, axis=0)
                hid = _silu(p + qs + c).astype(BF16)
                if kind == 0:
                    o = jnp.dot(hid, w2k_ref[...], preferred_element_type=F32)
                    kc_ref[:, g * NSA_DH:(g + 1) * NSA_DH] = o.astype(BF16)
                else:
                    ot = lax.dot_general(w2vt_ref[...], hid, _NT, preferred_element_type=F32)
                    vct_ref[g * NSA_DH:(g + 1) * NSA_DH, :] = ot.astype(BF16)


def _compress(a, pe, w1, wp, w2k, w2vt):
    b, n_rows, width = a.shape
    half = (CMP_BLOCK // 2) * NSA_DH
    return pl.pallas_call(
        _compress_kernel,
        grid=(b,),
        in_specs=[
            pl.BlockSpec((None, n_rows, width), lambda i: (i, 0, 0)),
            pl.BlockSpec((2, 8, 2 * half), lambda i: (0, 0, 0)),
            pl.BlockSpec((2, 2 * half, CMP_HIDDEN), lambda i: (0, 0, 0)),
            pl.BlockSpec((2, CMP_STRIDE, LANE, 4 * CMP_HIDDEN), lambda i: (0, 0, 0, 0)),
            pl.BlockSpec((CMP_HIDDEN, NSA_DH), lambda i: (0, 0)),
            pl.BlockSpec((NSA_DH, CMP_HIDDEN), lambda i: (0, 0)),
        ],
        out_specs=[
            pl.BlockSpec((None, n_rows, NSA_KV), lambda i: (i, 0, 0)),
            pl.BlockSpec((None, NSA_KV, n_rows), lambda i: (i, 0, 0)),
        ],
        out_shape=[
            jax.ShapeDtypeStruct((b, n_rows, NSA_KV), BF16),
            jax.ShapeDtypeStruct((b, NSA_KV, n_rows), BF16),
        ],
        compiler_params=_cparams("parallel"),
        name="compress",
    )(a, pe, w1, wp, w2k, w2vt)


def _pad_query(qt, g):
    z = jnp.zeros_like(qt)
    return jnp.concatenate([qt, z] if g % 2 == 0 else [z, qt], axis=0)


def _cmpsel_kernel(qt_ref, kc_ref, vct_ref, ocmpt_ref, selb_ref, *, tq, n_sel, topk):
    nc = kc_ref.shape[0]
    t0 = pl.program_id(1) * tq

    n_t = lax.broadcasted_iota(jnp.int32, (nc, tq), 0)
    t_t = t0 + lax.broadcasted_iota(jnp.int32, (nc, tq), 1)
    mask_t = (n_t * CMP_STRIDE + (CMP_BLOCK - 1)) <= t_t

    j_m = lax.broadcasted_iota(jnp.int32, (n_sel, nc), 0) * SEL_BLOCK
    c_m = lax.broadcasted_iota(jnp.int32, (n_sel, nc), 1) * CMP_STRIDE
    ov = jnp.minimum(c_m + CMP_BLOCK, j_m + SEL_BLOCK) - jnp.maximum(c_m, j_m)
    map_t = jnp.maximum(ov, 0).astype(F32) * (1.0 / CMP_BLOCK)

    j_s = lax.broadcasted_iota(jnp.int32, (n_sel, tq), 0)
    qblk = (t0 + lax.broadcasted_iota(jnp.int32, (n_sel, tq), 1)) // SEL_BLOCK
    rel = qblk - j_s
    causal = rel >= 0
    forced = causal & ((j_s == 0) | (rel < N_LOCAL_FORCED))

    for g in range(NSA_GROUPS):
        pair = g // 2
        kcp = kc_ref[:, pair * LANE:(pair + 1) * LANE]
        vct = vct_ref[g * NSA_DH:(g + 1) * NSA_DH, :]
        psum_t = jnp.zeros((nc, tq), F32)
        sts = [jnp.dot(kcp, _pad_query(qt_ref[(g * NSA_HPG + h) * NSA_DH:(g * NSA_HPG + h + 1) * NSA_DH, :], g),
                       preferred_element_type=F32) for h in range(NSA_HPG)]
        for h in range(NSA_HPG):
            hh = g * NSA_HPG + h
            st = jnp.where(mask_t, sts[h], NEG_INF)
            et = jnp.where(mask_t, jnp.exp2(st - jnp.max(st, axis=0, keepdims=True)), 0.0)
            lt = jnp.sum(et, axis=0, keepdims=True)
            pt = et * (1.0 / jnp.where(lt > 0.0, lt, 1.0))
            psum_t = psum_t + pt
            ot = jnp.dot(vct, pt.astype(BF16), preferred_element_type=F32)
            ocmpt_ref[hh * NSA_DH:(hh + 1) * NSA_DH, :] = ot.astype(BF16)
        imp_t = jnp.dot(map_t, psum_t, preferred_element_type=F32,
                        precision=lax.Precision.HIGHEST)
        score = jnp.where(forced, BIG, jnp.where(causal, imp_t, NEG_INF))
        rank = jnp.zeros((n_sel, tq), jnp.int32)
        for jp in range(n_sel):
            row = score[jp:jp + 1, :]
            before = (row > score) | ((row == score) & (j_s > jp))
            rank = rank + before.astype(jnp.int32)
        chosen = (rank < topk) & (score > 0.5 * NEG_INF)
        selb_ref[g * n_sel:(g + 1) * n_sel, :] = jnp.where(chosen, 0.0, NEG_INF)


def _cmpsel(zt, kc, vct, tq=256):
    b, _, s = zt.shape
    n_sel = s // SEL_BLOCK
    topk = min(SEL_TOPK, n_sel)
    nc = kc.shape[1]
    kern = functools.partial(_cmpsel_kernel, tq=tq, n_sel=n_sel, topk=topk)
    return pl.pallas_call(
        kern,
        grid=(b, s // tq),
        in_specs=[
            pl.BlockSpec((None, NSA_Q, tq), lambda i, j: (i, ZT_Q // NSA_Q, j)),
            pl.BlockSpec((None, nc, NSA_KV), lambda i, j: (i, 0, 0)),
            pl.BlockSpec((None, NSA_KV, nc), lambda i, j: (i, 0, 0)),
        ],
        out_specs=[
            pl.BlockSpec((None, NSA_Q, tq), lambda i, j: (i, 0, j)),
            pl.BlockSpec((None, NSA_GROUPS * n_sel, tq), lambda i, j: (i, 0, j)),
        ],
        out_shape=[
            jax.ShapeDtypeStruct((b, NSA_Q, s), BF16),
            jax.ShapeDtypeStruct((b, NSA_GROUPS * n_sel, s), F32),
        ],
        compiler_params=_cparams("parallel", "parallel"),
        name="cmpsel",
    )(zt, kc, vct)


def _nsa_kernel(qt_ref, ks_ref, kw_ref, vst_ref, vwt_ref, selb_ref, ocmpt_ref, gt_ref, y_ref,
                yt_scr, qp_scr, m_scr, acc_scr, st_scr, mx_scr, bias_scr, *, tq, n_sel, look):
    tk = tq
    nw = WINDOW // tk
    sel_per_tile = tk // SEL_BLOCK
    den_rows = 16
    i = pl.program_id(1)
    d0 = (lax.broadcasted_iota(jnp.int32, (tk, tq), 1)
          - lax.broadcasted_iota(jnp.int32, (tk, tq), 0))
    caus = jnp.where(d0 >= 0, 0.0, NEG_INF)
    lowb = jnp.where(d0 < 0, 0.0, NEG_INF)
    ones_rows = jnp.ones((den_rows, tk), BF16)
    ext = 16
    key_blk = lax.broadcasted_iota(jnp.int32, (tk, ext), 0) // SEL_BLOCK
    ext_col = lax.broadcasted_iota(jnp.int32, (tk, ext), 1)
    gates = _sigmoid(gt_ref[...])

    for hh in range(NSA_HEADS):
        qp_scr[hh] = _pad_query(qt_ref[hh * NSA_DH:(hh + 1) * NSA_DH, :], hh // NSA_HPG)
    m_scr[...] = jnp.full(m_scr.shape, NEG_INF, F32)
    acc_scr[...] = jnp.zeros(acc_scr.shape, F32)

    def run(br, k_ref, vt_ref, sel_on, tiles):
        loaded = []
        for ti, (kt, mask) in enumerate(tiles):
            r0 = pl.multiple_of(kt * tk, tk)
            ktiles = [k_ref[pl.ds(r0, tk), pair * LANE:(pair + 1) * LANE]
                      for pair in range(NSA_GROUPS // 2)]
            vt_augs = [jnp.concatenate([vt_ref[g * NSA_DH:(g + 1) * NSA_DH, pl.ds(r0, tk)], ones_rows],
                                       axis=0) for g in range(NSA_GROUPS)]
            loaded.append((ktiles, vt_augs, None))
            if sel_on:
                row0 = kt * sel_per_tile
                slab0 = pl.multiple_of((row0 // SLAB_ROWS) * SLAB_ROWS, SLAB_ROWS)
                hot = jnp.where(key_blk + row0 % SLAB_ROWS == ext_col, 1.0, 0.0).astype(BF16)
                ktiles = [jnp.concatenate([kp, hot], axis=1) for kp in ktiles]
                slabs = [jnp.concatenate(
                    [selb_ref[pl.ds(g * n_sel + slab0, SLAB_ROWS), :],
                     jnp.zeros((ext - SLAB_ROWS, tq), F32)], axis=0).astype(BF16)
                    for g in range(NSA_GROUPS)]
                loaded[-1] = (ktiles, vt_augs, slabs)
            if mask is not None:
                bias_scr[ti] = mask

        def qk(ti, hh):
            g = hh // NSA_HPG
            q = qp_scr[hh]
            if sel_on:
                q = jnp.concatenate([q, loaded[ti][2][g]], axis=0)
            st = jnp.dot(loaded[ti][0][g // 2], q, preferred_element_type=F32)
            if tiles[ti][1] is not None:
                st = st + bias_scr[ti]
            st_scr[ti * NSA_HEADS + hh] = st
            mx_scr[ti * NSA_HEADS + hh] = jnp.max(st, axis=0, keepdims=True)

        def softmax_pv(ti, hh):
            vt_aug = loaded[ti][1][hh // NSA_HPG]
            m = m_scr[br, hh]
            m_new = jnp.maximum(m, mx_scr[ti * NSA_HEADS + hh])
            alpha = jnp.exp2(m - m_new)
            p = jnp.exp2(st_scr[ti * NSA_HEADS + hh] - m_new).astype(BF16)
            acc_scr[br, hh] = alpha * acc_scr[br, hh] + jnp.dot(vt_aug, p, preferred_element_type=F32)
            m_scr[br, hh] = m_new

        seq = [(ti, hh) for ti in range(len(tiles)) for hh in range(NSA_HEADS)]
        for pos in range(min(look, len(seq))):
            qk(*seq[pos])
        for pos, item in enumerate(seq):
            if pos + look < len(seq):
                qk(*seq[pos + look])
            softmax_pv(*item)

    def slc_body(j, c):
        run(0, ks_ref, vst_ref, True, [(2 * j, None), (2 * j + 1, None)])
        return c

    lax.fori_loop(0, i // 2, slc_body, 0)

    @pl.when(i % 2 == 1)
    def _():
        run(0, ks_ref, vst_ref, True, [(i - 1, None), (i, caus)])

    @pl.when(i % 2 == 0)
    def _():
        run(0, ks_ref, vst_ref, True, [(i, caus)])

    for v in range(nw):
        @pl.when(i == v)
        def _(v=v):
            run(1, kw_ref, vwt_ref, False, [(kt, None) for kt in range(v)] + [(i, caus)])

    @pl.when(i >= nw)
    def _():
        run(1, kw_ref, vwt_ref, False,
            [(i - nw, lowb)] + [(i - nw + d, None) for d in range(1, nw)] + [(i, caus)])

    for hh in range(NSA_HEADS):
        acc_s = acc_scr[0, hh]
        acc_w = acc_scr[1, hh]
        o_slc = acc_s[0:NSA_DH] * (1.0 / acc_s[NSA_DH:NSA_DH + 1])
        o_win = acc_w[0:NSA_DH] * (1.0 / acc_w[NSA_DH:NSA_DH + 1])
        o_cmp = ocmpt_ref[hh * NSA_DH:(hh + 1) * NSA_DH, :].astype(F32)
        gc = ZG_GNSA + 3 * hh
        yt_scr[hh * NSA_DH:(hh + 1) * NSA_DH, :] = (
            gates[gc:gc + 1] * o_cmp + gates[gc + 1:gc + 2] * o_slc + gates[gc + 2:gc + 3] * o_win)

    y_ref[...] = yt_scr[...].T.astype(BF16)


def _nsa(zt, zb3, selb, ocmpt, zgt, tq=256, look=5):
    b, _, s = zt.shape
    n_sel = s // SEL_BLOCK
    tk = tq
    den_rows = 16
    max_tiles = max(2, WINDOW // tk + 1)
    assert WINDOW % tq == 0 and tq % SEL_BLOCK == 0 and SLAB_ROWS % (tk // SEL_BLOCK) == 0
    kern = functools.partial(_nsa_kernel, tq=tq, n_sel=n_sel, look=look)
    return pl.pallas_call(
        kern,
        grid=(b, s // tq),
        in_specs=[
            pl.BlockSpec((None, NSA_Q, tq), lambda bi, j: (bi, ZT_Q // NSA_Q, j)),
            pl.BlockSpec((None, s, NSA_KV), lambda bi, j: (bi, 0, ZB_KSLC // NSA_KV)),
            pl.BlockSpec((None, s, NSA_KV), lambda bi, j: (bi, 0, ZB_KWIN // NSA_KV)),
            pl.BlockSpec((None, NSA_KV, s), lambda bi, j: (bi, ZT_VSLC // NSA_KV, 0)),
            pl.BlockSpec((None, NSA_KV, s), lambda bi, j: (bi, ZT_VWIN // NSA_KV, 0)),
            pl.BlockSpec((None, NSA_GROUPS * n_sel, tq), lambda bi, j: (bi, 0, j)),
            pl.BlockSpec((None, NSA_Q, tq), lambda bi, j: (bi, 0, j)),
            pl.BlockSpec((None, ZG_WIDTH, tq), lambda bi, j: (bi, 0, j)),
        ],
        out_specs=pl.BlockSpec((None, tq, NSA_Q), lambda bi, j: (bi, j, 0)),
        out_shape=jax.ShapeDtypeStruct((b, s, NSA_Q), BF16),
        scratch_shapes=[
            pltpu.VMEM((NSA_Q, tq), F32),
            pltpu.VMEM((NSA_HEADS, 2 * NSA_DH, tq), BF16),
            pltpu.VMEM((2, NSA_HEADS, 1, tq), F32),
            pltpu.VMEM((2, NSA_HEADS, NSA_DH + den_rows, tq), F32),
            pltpu.VMEM((max_tiles * NSA_HEADS, tk, tq), F32),
            pltpu.VMEM((max_tiles * NSA_HEADS, 1, tq), F32),
            pltpu.VMEM((max_tiles, tk, tq), F32),
        ],
        compiler_params=_cparams("parallel", "arbitrary"),
        name="nsa",
    )(zt, zb3, zb3, zt, zt, selb, ocmpt, zgt)


def _mlstm_kernel(q_ref, k_ref, v_ref, o_ref, zg_ref, cw_ref, cb_ref, gb_ref, hg_ref, y_ref,
                  xbuf, c_st, n_st, m_st, *, nb):
    L = ML_CHUNK
    pad = 8
    @pl.when(pl.program_id(1) == 0)
    def _():
        xbuf[:, 0:pad, :] = jnp.zeros((nb, pad, 2 * ML_W), F32)
        c_st[...] = jnp.zeros_like(c_st)
        n_st[...] = jnp.zeros_like(n_st)
        m_st[...] = jnp.zeros_like(m_st)

    row = lax.broadcasted_iota(jnp.int32, (L, L), 0)
    col = lax.broadcasted_iota(jnp.int32, (L, L), 1)
    tril = row >= col
    tril_f = jnp.where(tril, 1.0, 0.0)
    streams = [(bb, h) for bb in range(nb) for h in range(ML_HEADS)]

    qks = []
    for bb in range(nb):
        xbuf[bb, pad:pad + L, 0:ML_W] = q_ref[bb].astype(F32)
        xbuf[bb, pad:pad + L, ML_W:2 * ML_W] = k_ref[bb].astype(F32)
        conv = jnp.zeros((L, 2 * ML_W), F32) + cb_ref[...]
        for j in range(CONV_WIDTH):
            conv = conv + xbuf[bb, pl.ds(pad - (CONV_WIDTH - 1) + j, L), :] * cw_ref[j:j + 1, :]
        xbuf[bb, 0:pad, :] = xbuf[bb, L:L + pad, :]
        qks.append(_silu(conv))

    qs = {(bb, h): qks[bb][:, h * ML_DH:(h + 1) * ML_DH] for bb, h in streams}
    ks = {(bb, h): qks[bb][:, ML_W + h * ML_DH:ML_W + (h + 1) * ML_DH] * (ML_DH ** -0.5)
          for bb, h in streams}
    qbs = {s: qs[s].astype(BF16) for s in streams}
    vbs = {(bb, h): v_ref[bb, :, h * ML_DH:(h + 1) * ML_DH] for bb, h in streams}
    c_olds = {(bb, h): c_st[bb * ML_HEADS + h] for bb, h in streams}
    n_olds = {(bb, h): n_st[bb, h:h + 1, :] for bb, h in streams}
    s_qk = {s: lax.dot_general(qbs[s], ks[s].astype(BF16), _NT, preferred_element_type=F32)
            for s in streams}
    cqs = {s: lax.dot_general(qbs[s], c_olds[s].astype(BF16), _NT, preferred_element_type=F32)
           for s in streams}

    g = []
    for bb in range(nb):
        gates = zg_ref[bb] + gb_ref[...]
        bcum = jnp.dot(tril_f, _log_sigmoid(gates), preferred_element_type=F32,
                       precision=lax.Precision.HIGHEST)
        b_al = pltpu.roll(bcum, shift=ZG_WIDTH - ML_HEADS, axis=1)
        m_prev = m_st[bb, 0:1, :]
        b_end = b_al[L - 1:L, :]
        inter = b_al + m_prev
        wlog = b_end - b_al + gates
        m_new = jnp.maximum(b_end + m_prev, jnp.max(wlog, axis=0, keepdims=True))
        ws = jnp.exp(wlog - m_new)
        decay = jnp.exp(b_end + m_prev - m_new)
        m_st[bb, 0:1, :] = m_new
        r_t = (gates - b_al).T
        g.append((b_al, inter, ws, decay, r_t))

    for bb, h in streams:
        c = ZG_IF + h
        _, _, ws, decay, _ = g[bb]
        kw = ks[bb, h] * ws[:, c:c + 1]
        upd = lax.dot_general(vbs[bb, h], kw.astype(BF16), _TN, preferred_element_type=F32)
        c_st[bb * ML_HEADS + h] = decay[:, c:c + 1] * c_olds[bb, h] + upd
        n_st[bb, h:h + 1, :] = decay[:, c:c + 1] * n_olds[bb, h] + jnp.sum(kw, axis=0, keepdims=True)

    for bb, h in streams:
        c = ZG_IF + h
        b_al, inter, _, _, r_t = g[bb]
        dlog = jnp.where(tril, b_al[:, c:c + 1] + r_t[c:c + 1, :], -jnp.inf)
        inter_c = inter[:, c:c + 1]
        m_t = jnp.maximum(inter_c, jnp.max(dlog, axis=-1, keepdims=True))
        dw = jnp.exp(dlog - m_t)
        iw = jnp.exp(inter_c - m_t)
        sqk = s_qk[bb, h] * dw
        num = jnp.dot(sqk.astype(BF16), vbs[bb, h], preferred_element_type=F32) + iw * cqs[bb, h]
        den = (jnp.sum(sqk, axis=-1, keepdims=True)
               + iw * jnp.sum(qs[bb, h] * n_olds[bb, h], axis=-1, keepdims=True))
        hs = num / jnp.maximum(jnp.abs(den), jnp.exp(-m_t))
        hn = hs * lax.rsqrt(jnp.mean(hs * hs, axis=-1, keepdims=True) + EPS)
        hn = hn * hg_ref[:, h * ML_DH:(h + 1) * ML_DH]
        og = _sigmoid(o_ref[bb, :, h * ML_DH:(h + 1) * ML_DH].astype(F32))
        y_ref[bb, :, h * ML_DH:(h + 1) * ML_DH] = (og * hn).astype(BF16)


def _mlstm(zb3, zg3, conv_w, conv_b, gate_b, head_g, nb=1):
    b, s, _ = zb3.shape
    L = ML_CHUNK
    qb = ZB_QKVML // ML_W
    assert b % nb == 0

    def zspec(k):
        return pl.BlockSpec((nb, L, ML_W), lambda bi, c, k=k: (bi, c, k))

    return pl.pallas_call(
        functools.partial(_mlstm_kernel, nb=nb),
        grid=(b // nb, s // L),
        in_specs=[
            zspec(qb), zspec(qb + 1), zspec(qb + 2), zspec(ZB_OML // ML_W),
            pl.BlockSpec((nb, L, ZG_WIDTH), lambda bi, c: (bi, c, 0)),
            pl.BlockSpec((CONV_WIDTH, 2 * ML_W), lambda bi, c: (0, 0)),
            pl.BlockSpec((1, 2 * ML_W), lambda bi, c: (0, 0)),
            pl.BlockSpec((1, ZG_WIDTH), lambda bi, c: (0, 0)),
            pl.BlockSpec((1, ML_W), lambda bi, c: (0, 0)),
        ],
        out_specs=pl.BlockSpec((nb, L, ML_W), lambda bi, c: (bi, c, 0)),
        out_shape=jax.ShapeDtypeStruct((b, s, ML_W), BF16),
        scratch_shapes=[
            pltpu.VMEM((nb, 8 + L, 2 * ML_W), F32),
            pltpu.VMEM((nb * ML_HEADS, ML_DH, ML_DH), F32),
            pltpu.VMEM((nb, 8, ML_DH), F32),
            pltpu.VMEM((nb, 8, LANE), F32),
        ],
        compiler_params=_cparams("parallel", "arbitrary"),
        name="mlstm",
    )(zb3, zb3, zb3, zb3, zg3, conv_w, conv_b, gate_b, head_g)


def _memkv_kernel(mem_ref, g_ref, w_ref, kv_ref):
    hb = _rms(mem_ref[...], g_ref[...]).astype(BF16)
    kv_ref[...] = jnp.dot(hb, w_ref[...], preferred_element_type=F32).astype(BF16)


def _memkv(mem, g, wb):
    b, m, _ = mem.shape
    return pl.pallas_call(
        _memkv_kernel,
        grid=(b,),
        in_specs=[
            pl.BlockSpec((None, m, D_MODEL), lambda i: (i, 0, 0)),
            pl.BlockSpec((1, D_MODEL), lambda i: (0, 0)),
            pl.BlockSpec((D_MODEL, 2 * MEM_W), lambda i: (0, 0)),
        ],
        out_specs=pl.BlockSpec((None, m, 2 * MEM_W), lambda i: (i, 0, 0)),
        out_shape=jax.ShapeDtypeStruct((b, m, 2 * MEM_W), BF16),
        compiler_params=_cparams("parallel"),
        name="memkv",
    )(mem, g, wb)


def _memattn_kernel(q_ref, kv_ref, y_ref):
    for h in range(MEM_HEADS):
        q = q_ref[:, h * MEM_DH:(h + 1) * MEM_DH]
        k = kv_ref[:, h * MEM_DH:(h + 1) * MEM_DH]
        v = kv_ref[:, MEM_W + h * MEM_DH:MEM_W + (h + 1) * MEM_DH]
        s = lax.dot_general(q, k, _NT, preferred_element_type=F32) * MEM_SCALE
        e = jnp.exp(s - jnp.max(s, axis=-1, keepdims=True))
        p = e / jnp.sum(e, axis=-1, keepdims=True)
        o = jnp.dot(p.astype(BF16), v, preferred_element_type=F32)
        y_ref[:, h * MEM_DH:(h + 1) * MEM_DH] = o.astype(BF16)


def _memattn(zb3, kvm, tq=512):
    b, s, _ = zb3.shape
    m = kvm.shape[1]
    return pl.pallas_call(
        _memattn_kernel,
        grid=(b, s // tq),
        in_specs=[
            pl.BlockSpec((None, tq, MEM_W), lambda i, j: (i, j, ZB_QMEM // MEM_W)),
            pl.BlockSpec((None, m, 2 * MEM_W), lambda i, j: (i, 0, 0)),
        ],
        out_specs=pl.BlockSpec((None, tq, MEM_W), lambda i, j: (i, j, 0)),
        out_shape=jax.ShapeDtypeStruct((b, s, MEM_W), BF16),
        compiler_params=_cparams("parallel", "parallel"),
        name="memattn",
    )(zb3, kvm)


def _merge_kernel(x_ref, yn_ref, yl_ref, ym_ref, g0_ref, g1_ref, g2_ref,
                  wn_ref, wl_ref, wm_ref, wo_ref, gp_ref, out_ref):
    y = _sigmoid(g0_ref[...].astype(F32)) * jnp.dot(yn_ref[...], wn_ref[...], preferred_element_type=F32)
    y = y + _sigmoid(g1_ref[...].astype(F32)) * jnp.dot(yl_ref[...], wl_ref[...], preferred_element_type=F32)
    y = y + _sigmoid(g2_ref[...].astype(F32)) * jnp.dot(ym_ref[...], wm_ref[...], preferred_element_type=F32)
    u = jnp.dot(y.astype(BF16), wo_ref[...], preferred_element_type=F32)
    out_ref[...] = x_ref[...] + _rms(u, gp_ref[...])


def _merge(x2, yn, yl, ym, zb, wn, wl, wm, wo, gp, tm=512):
    t = x2.shape[0]
    gm = ZB_GMERGE // D_MODEL

    def const(shape):
        return pl.BlockSpec(shape, lambda i: (0, 0))

    return pl.pallas_call(
        _merge_kernel,
        grid=(t // tm,),
        in_specs=[
            pl.BlockSpec((tm, D_MODEL), lambda i: (i, 0)),
            pl.BlockSpec((tm, NSA_Q), lambda i: (i, 0)),
            pl.BlockSpec((tm, ML_W), lambda i: (i, 0)),
            pl.BlockSpec((tm, MEM_W), lambda i: (i, 0)),
            pl.BlockSpec((tm, D_MODEL), lambda i: (i, gm)),
            pl.BlockSpec((tm, D_MODEL), lambda i: (i, gm + 1)),
            pl.BlockSpec((tm, D_MODEL), lambda i: (i, gm + 2)),
            const((NSA_Q, D_MODEL)), const((ML_W, D_MODEL)), const((MEM_W, D_MODEL)),
            const((D_MODEL, D_MODEL)), const((1, D_MODEL)),
        ],
        out_specs=pl.BlockSpec((tm, D_MODEL), lambda i: (i, 0)),
        out_shape=jax.ShapeDtypeStruct((t, D_MODEL), F32),
        compiler_params=_cparams("parallel"),
        name="merge",
    )(x2, yn, yl, ym, zb, zb, zb, wn, wl, wm, wo, gp)


def _ffn_kernel(x_ref, gpre_ref, wg_ref, wu_ref, wd_ref, gpost_ref, out_ref, h_ref, acc_ref):
    j = pl.program_id(1)

    @pl.when(j == 0)
    def _():
        h_ref[...] = _rms(x_ref[...], gpre_ref[...]).astype(BF16)
        acc_ref[...] = jnp.zeros_like(acc_ref)

    h = h_ref[...]
    gate = jnp.dot(h, wg_ref[...], preferred_element_type=F32)
    up = jnp.dot(h, wu_ref[...], preferred_element_type=F32)
    act = (_silu(gate) * up).astype(BF16)
    acc_ref[...] += jnp.dot(act, wd_ref[...], preferred_element_type=F32)

    @pl.when(j == pl.num_programs(1) - 1)
    def _():
        out_ref[...] = x_ref[...] + _rms(acc_ref[...], gpost_ref[...])


def _ffn(x2, gpre, w_in, w_down, gpost, tm=1024, nf=2):
    t = x2.shape[0]
    tf = D_FF // nf
    assert tf % LANE == 0
    return pl.pallas_call(
        _ffn_kernel,
        grid=(t // tm, nf),
        in_specs=[
            pl.BlockSpec((tm, D_MODEL), lambda i, j: (i, 0)),
            pl.BlockSpec((1, D_MODEL), lambda i, j: (0, 0)),
            pl.BlockSpec((D_MODEL, tf), lambda i, j: (0, j)),
            pl.BlockSpec((D_MODEL, tf), lambda i, j: (0, nf + j)),
            pl.BlockSpec((tf, D_MODEL), lambda i, j: (j, 0)),
            pl.BlockSpec((1, D_MODEL), lambda i, j: (0, 0)),
        ],
        out_specs=pl.BlockSpec((tm, D_MODEL), lambda i, j: (i, 0)),
        out_shape=jax.ShapeDtypeStruct((t, D_MODEL), F32),
        scratch_shapes=[pltpu.VMEM((tm, D_MODEL), BF16), pltpu.VMEM((tm, D_MODEL), F32)],
        compiler_params=_cparams("parallel", "arbitrary"),
        name="ffn",
    )(x2, gpre, w_in, w_in, w_down, gpost)


def _layer(x, mem, p):
    b, s, _ = x.shape
    t = b * s
    x2 = x.reshape(t, D_MODEL)

    wn, wt, wg, wc = _wprep(p["w_in"].T)
    zt, zb, zg, zgt, kvc = _inproj(x2, p["g_pre_mix"].reshape(1, D_MODEL), wt, wn, wg, wc, b, s)
    zb3 = zb.reshape(b, s, ZB_WIDTH)
    zg3 = zg.reshape(b, s, ZG_WIDTH)

    n_rows = s // CMP_STRIDE
    half = (CMP_BLOCK // 2) * NSA_DH
    a = kvc.reshape(b, n_rows, CMP_STRIDE * KVC_WIDTH)
    pe = jnp.stack([p["cmp_pe_k"], p["cmp_pe_v"]]).reshape(2, 1, 2 * half)
    pe = jnp.pad(pe, ((0, 0), (0, 7), (0, 0))).astype(BF16)
    w1 = jnp.stack([p["cmp_w1_k"], p["cmp_w1_v"]])
    eye2 = jnp.eye(2, dtype=F32)
    wpa = jnp.einsum("ab,kldh->kladbh", eye2, w1[:, :CMP_STRIDE]).reshape(2, CMP_STRIDE, LANE, 2 * CMP_HIDDEN)
    wpb = jnp.einsum("ab,kldh->kladbh", eye2, w1[:, CMP_STRIDE:]).reshape(2, CMP_STRIDE, LANE, 2 * CMP_HIDDEN)
    wp = jnp.concatenate([wpa, wpb], axis=-1).astype(BF16)
    kc, vct = _compress(a, pe, w1.reshape(2, 2 * half, CMP_HIDDEN).astype(BF16), wp,
                        p["cmp_w2_k"].astype(BF16), p["cmp_w2_v"].T.astype(BF16))
    ocmpt, selb = _cmpsel(zt, kc, vct)
    y_nsa = _nsa(zt, zb3, selb, ocmpt, zgt)

    gate_b = jnp.zeros((1, ZG_WIDTH), F32).at[0, ZG_IF:ZG_IF + 2 * ML_HEADS].set(p["ml_gate_b"])
    y_ml = _mlstm(zb3, zg3, p["ml_conv_w"], p["ml_conv_b"].reshape(1, 2 * ML_W), gate_b,
                  p["ml_head_g"].reshape(1, ML_W))

    kvm = _memkv(mem, p["g_mem"].reshape(1, D_MODEL), p["w_mem_kv"].astype(BF16))
    y_mem = _memattn(zb3, kvm)

    x1 = _merge(x2, y_nsa.reshape(t, NSA_Q), y_ml.reshape(t, ML_W), y_mem.reshape(t, MEM_W), zb,
                p["w_proj_nsa"].astype(BF16), p["w_proj_ml"].astype(BF16),
                p["w_proj_mem"].astype(BF16), p["w_out"].astype(BF16),
                p["g_post_mix"].reshape(1, D_MODEL))
    x2o = _ffn(x1, p["g_pre_ffn"].reshape(1, D_MODEL), p["w_ffn_in"].astype(BF16),
               p["w_ffn_down"].astype(BF16), p["g_post_ffn"].reshape(1, D_MODEL))
    return x2o.reshape(b, s, D_MODEL)


def kernel(x, mem, g_pre_mix, w_in, cmp_pe_k, cmp_w1_k, cmp_w2_k, cmp_pe_v, cmp_w1_v, cmp_w2_v,
           ml_conv_w, ml_conv_b, ml_gate_b, ml_head_g, g_mem, w_mem_kv, w_proj_nsa, w_proj_ml,
           w_proj_mem, w_out, g_post_mix, g_pre_ffn, w_ffn_in, w_ffn_down, g_post_ffn):
    params = dict(
        g_pre_mix=g_pre_mix, w_in=w_in, cmp_pe_k=cmp_pe_k, cmp_w1_k=cmp_w1_k, cmp_w2_k=cmp_w2_k,
        cmp_pe_v=cmp_pe_v, cmp_w1_v=cmp_w1_v, cmp_w2_v=cmp_w2_v, ml_conv_w=ml_conv_w,
        ml_conv_b=ml_conv_b, ml_gate_b=ml_gate_b, ml_head_g=ml_head_g, g_mem=g_mem,
        w_mem_kv=w_mem_kv, w_proj_nsa=w_proj_nsa, w_proj_ml=w_proj_ml, w_proj_mem=w_proj_mem,
        w_out=w_out, g_post_mix=g_post_mix, g_pre_ffn=g_pre_ffn, w_ffn_in=w_ffn_in,
        w_ffn_down=w_ffn_down, g_post_ffn=g_post_ffn)
    depth = w_in.shape[0]
    for l in range(depth):
        x = _layer(x, mem, {k: v[l] for k, v in params.items()})
    return x
```

```python
import functools
import math

import jax
import jax.numpy as jnp
from jax import lax
from jax.experimental import pallas as pl
from jax.experimental.pallas import tpu as pltpu

F32 = jnp.float32
BF16 = jnp.bfloat16

D_MODEL = 1024
EPS = 1e-6
NEG_INF = -1e30
BIG = 1e30

NSA_HEADS = 16
NSA_GROUPS = 4
NSA_HPG = NSA_HEADS // NSA_GROUPS
NSA_DH = 64
NSA_SCALE = NSA_DH ** -0.5
CMP_BLOCK = 32
CMP_STRIDE = 16
CMP_HIDDEN = 128
SEL_BLOCK = 64
SEL_TOPK = 16
N_LOCAL_FORCED = 2
WINDOW = 512

ML_HEADS = 4
ML_DH = 128
ML_CHUNK = 128
CONV_WIDTH = 4

MEM_HEADS = 4
MEM_DH = 128
MEM_SCALE = MEM_DH ** -0.5

D_FF = -(-8 * D_MODEL // (3 * 256)) * 256

NSA_Q = NSA_HEADS * NSA_DH
NSA_KV = NSA_GROUPS * NSA_DH
ML_W = ML_HEADS * ML_DH
MEM_W = MEM_HEADS * MEM_DH

_OFF_Q = 0
_OFF_KV = _OFF_Q + NSA_Q
_OFF_GNSA = _OFF_KV + 6 * NSA_KV
_OFF_QKVML = _OFF_GNSA + 3 * NSA_HEADS
_OFF_IF = _OFF_QKVML + 3 * ML_W
_OFF_OML = _OFF_IF + 2 * ML_HEADS
_OFF_QMEM = _OFF_OML + ML_W
_OFF_GMERGE = _OFF_QMEM + MEM_W
_IN_WIDTH = _OFF_GMERGE + 3 * D_MODEL

ZB_GMERGE = 0
ZB_KSLC = ZB_GMERGE + 3 * D_MODEL
ZB_KWIN = ZB_KSLC + NSA_KV
ZB_QKVML = ZB_KWIN + NSA_KV
ZB_OML = ZB_QKVML + 3 * ML_W
ZB_QMEM = ZB_OML + ML_W
ZB_WIDTH = ZB_QMEM + MEM_W
KVC_WIDTH = 2 * NSA_KV
ZT_Q = 0
ZT_VSLC = ZT_Q + NSA_Q
ZT_VWIN = ZT_VSLC + NSA_KV
ZT_ROWS = ZT_VWIN + NSA_KV
ZG_WIDTH = 128
ZG_GNSA = 0
ZG_IF = 3 * NSA_HEADS

LANE = 128
SLAB_ROWS = 8
VMEM_LIMIT = 56 * 1024 * 1024
LOG2E = math.log2(math.e)

_NT = (((1,), (1,)), ((), ()))
_TN = (((0,), (0,)), ((), ()))


def _cparams(*sem):
    return pltpu.CompilerParams(dimension_semantics=sem, vmem_limit_bytes=VMEM_LIMIT)


def _rms(x, g):
    return x * lax.rsqrt(jnp.mean(x * x, axis=-1, keepdims=True) + EPS) * g


def _sigmoid(x):
    return 1.0 / (1.0 + jnp.exp(-x))


def _silu(x):
    return x * _sigmoid(x)


def _log_sigmoid(x):
    return jnp.minimum(x, 0.0) - jnp.log(1.0 + jnp.exp(-jnp.abs(x)))


def _wprep_kernel(w_ref, wn_ref, wt_ref, wg_ref, wc_ref):
    def rows(lo, hi):
        return w_ref[lo:hi, :]

    def kv_rows(k):
        return rows(_OFF_KV + k * NSA_KV, _OFF_KV + (k + 1) * NSA_KV)

    wn_ref[ZB_GMERGE:ZB_KSLC, :] = rows(_OFF_GMERGE, _IN_WIDTH).astype(BF16)
    wn_ref[ZB_KSLC:ZB_KWIN, :] = kv_rows(2).astype(BF16)
    wn_ref[ZB_KWIN:ZB_QKVML, :] = kv_rows(4).astype(BF16)
    wn_ref[ZB_QKVML:ZB_OML, :] = rows(_OFF_QKVML, _OFF_IF).astype(BF16)
    wn_ref[ZB_OML:ZB_WIDTH, :] = rows(_OFF_OML, _OFF_GMERGE).astype(BF16)
    wt_ref[ZT_Q:ZT_VSLC, :] = (rows(_OFF_Q, _OFF_KV) * (NSA_SCALE * LOG2E)).astype(BF16)
    wt_ref[ZT_VSLC:ZT_VWIN, :] = kv_rows(3).astype(BF16)
    wt_ref[ZT_VWIN:ZT_ROWS, :] = kv_rows(5).astype(BF16)
    n_g = 3 * NSA_HEADS
    n_if = 2 * ML_HEADS
    wg_ref[...] = jnp.zeros(wg_ref.shape, BF16)
    wg_ref[ZG_GNSA:ZG_GNSA + n_g, :] = rows(_OFF_GNSA, _OFF_QKVML).astype(BF16)
    wg_ref[ZG_IF:ZG_IF + n_if, :] = rows(_OFF_IF, _OFF_OML).astype(BF16)
    wc_ref[...] = rows(_OFF_KV, _OFF_KV + KVC_WIDTH).astype(BF16)


def _wprep(w_t, tl=128):
    assert ZB_QMEM == ZB_OML + ML_W and _OFF_QMEM == _OFF_OML + ML_W
    return pl.pallas_call(
        _wprep_kernel,
        grid=(D_MODEL // tl,),
        in_specs=[pl.BlockSpec((_IN_WIDTH, tl), lambda i: (0, i))],
        out_specs=[
            pl.BlockSpec((ZB_WIDTH, tl), lambda i: (0, i)),
            pl.BlockSpec((ZT_ROWS, tl), lambda i: (0, i)),
            pl.BlockSpec((ZG_WIDTH, tl), lambda i: (0, i)),
            pl.BlockSpec((KVC_WIDTH, tl), lambda i: (0, i)),
        ],
        out_shape=[
            jax.ShapeDtypeStruct((ZB_WIDTH, D_MODEL), BF16),
            jax.ShapeDtypeStruct((ZT_ROWS, D_MODEL), BF16),
            jax.ShapeDtypeStruct((ZG_WIDTH, D_MODEL), BF16),
            jax.ShapeDtypeStruct((KVC_WIDTH, D_MODEL), BF16),
        ],
        compiler_params=_cparams("parallel"),
        name="wprep",
    )(w_t)


def _inproj_kernel(x_ref, g_ref, wt_ref, wn_ref, wg_ref, wc_ref,
                   zt_ref, zb_ref, zg_ref, zgt_ref, kvc_ref, h_ref, *, nt):
    j = pl.program_id(1)

    @pl.when(j == 0)
    def _():
        hb = _rms(x_ref[...], g_ref[...]).astype(BF16)
        h_ref[...] = hb
        zg_ref[...] = lax.dot_general(hb, wg_ref[...], _NT, preferred_element_type=F32)
        zgt_ref[...] = lax.dot_general(wg_ref[...], hb, _NT, preferred_element_type=F32)
        kvc_ref[...] = lax.dot_general(hb, wc_ref[...], _NT, preferred_element_type=F32)

    @pl.when(j < nt)
    def _():
        zt_ref[...] = lax.dot_general(wt_ref[...], h_ref[...], _NT,
                                      preferred_element_type=F32).astype(BF16)

    @pl.when(j >= nt)
    def _():
        zb_ref[...] = lax.dot_general(h_ref[...], wn_ref[...], _NT,
                                      preferred_element_type=F32).astype(BF16)


def _inproj(x2, g, wt, wn, wg, wc, b, s, tm=1024, nt=1, nn=3):
    t = x2.shape[0]
    tr = ZT_ROWS // nt
    tn = ZB_WIDTH // nn
    assert tr * nt == ZT_ROWS and tn * nn == ZB_WIDTH and tn % LANE == 0 and s % tm == 0
    spb = s // tm
    kern = functools.partial(_inproj_kernel, nt=nt)
    return pl.pallas_call(
        kern,
        grid=(t // tm, nt + nn),
        in_specs=[
            pl.BlockSpec((tm, D_MODEL), lambda i, j: (i, 0)),
            pl.BlockSpec((1, D_MODEL), lambda i, j: (0, 0)),
            pl.BlockSpec((tr, D_MODEL), lambda i, j: (jnp.minimum(j, nt - 1), 0)),
            pl.BlockSpec((tn, D_MODEL), lambda i, j: (jnp.maximum(j - nt, 0), 0)),
            pl.BlockSpec((ZG_WIDTH, D_MODEL), lambda i, j: (0, 0)),
            pl.BlockSpec((KVC_WIDTH, D_MODEL), lambda i, j: (0, 0)),
        ],
        out_specs=[
            pl.BlockSpec((None, tr, tm), lambda i, j: (i // spb, jnp.minimum(j, nt - 1), i % spb)),
            pl.BlockSpec((tm, tn), lambda i, j: (i, jnp.maximum(j - nt, 0))),
            pl.BlockSpec((tm, ZG_WIDTH), lambda i, j: (i, 0)),
            pl.BlockSpec((None, ZG_WIDTH, tm), lambda i, j: (i // spb, 0, i % spb)),
            pl.BlockSpec((tm, KVC_WIDTH), lambda i, j: (i, 0)),
        ],
        out_shape=[
            jax.ShapeDtypeStruct((b, ZT_ROWS, s), BF16),
            jax.ShapeDtypeStruct((t, ZB_WIDTH), BF16),
            jax.ShapeDtypeStruct((t, ZG_WIDTH), F32),
            jax.ShapeDtypeStruct((b, ZG_WIDTH, s), F32),
            jax.ShapeDtypeStruct((t, KVC_WIDTH), F32),
        ],
        scratch_shapes=[pltpu.VMEM((tm, D_MODEL), BF16)],
        compiler_params=_cparams("parallel", "arbitrary"),
        name="inproj",
    )(x2, g, wt, wn, wg, wc)


def _compress_kernel(a0_ref, a1_ref, a2_ref, a3_ref, pe_ref, w1_ref, wp_ref, w2k_ref, w2vt_ref,
                     kc_ref, vct_ref):
    a_refs = (a0_ref, a1_ref, a2_ref, a3_ref)
    n_rows = a0_ref.shape[0] // CMP_STRIDE
    for kind in range(2):
        pe = pe_ref[kind]
        c = jnp.dot(pe, w1_ref[kind], preferred_element_type=F32)[0:1, :]
        for pair in range(NSA_GROUPS // 2):
            a_ref = a_refs[kind * 2 + pair]
            acc = jnp.zeros((n_rows, 4 * CMP_HIDDEN), F32)
            for l in range(CMP_STRIDE):
                rows = a_ref[pl.ds(l, n_rows, stride=CMP_STRIDE), :].astype(BF16)
                acc = acc + jnp.dot(rows, wp_ref[kind, l], preferred_element_type=F32)
            for gi in range(2):
                g = 2 * pair + gi
                p = acc[:, gi * CMP_HIDDEN:(gi + 1) * CMP_HIDDEN]
                q = acc[:, (2 + gi) * CMP_HIDDEN:(3 + gi) * CMP_HIDDEN]
                qs = pltpu.roll(q, shift=n_rows - 1, axis=0)
                hid = _silu(p + qs + c).astype(BF16)
                if kind == 0:
                    o = jnp.dot(hid, w2k_ref[...], preferred_element_type=F32)
                    kc_ref[:, g * NSA_DH:(g + 1) * NSA_DH] = o.astype(BF16)
                else:
                    ot = lax.dot_general(w2vt_ref[...], hid, _NT, preferred_element_type=F32)
                    vct_ref[g * NSA_DH:(g + 1) * NSA_DH, :] = ot.astype(BF16)


def _compress(kvc3, pe, w1, wp, w2k, w2vt):
    b, s, _ = kvc3.shape
    n_rows = s // CMP_STRIDE
    half = (CMP_BLOCK // 2) * NSA_DH

    def col(k):
        return pl.BlockSpec((None, s, LANE), lambda i, k=k: (i, 0, k))

    return pl.pallas_call(
        _compress_kernel,
        grid=(b,),
        in_specs=[
            col(0), col(1), col(2), col(3),
            pl.BlockSpec((2, 8, 2 * half), lambda i: (0, 0, 0)),
            pl.BlockSpec((2, 2 * half, CMP_HIDDEN), lambda i: (0, 0, 0)),
            pl.BlockSpec((2, CMP_STRIDE, LANE, 4 * CMP_HIDDEN), lambda i: (0, 0, 0, 0)),
            pl.BlockSpec((CMP_HIDDEN, NSA_DH), lambda i: (0, 0)),
            pl.BlockSpec((NSA_DH, CMP_HIDDEN), lambda i: (0, 0)),
        ],
        out_specs=[
            pl.BlockSpec((None, n_rows, NSA_KV), lambda i: (i, 0, 0)),
            pl.BlockSpec((None, NSA_KV, n_rows), lambda i: (i, 0, 0)),
        ],
        out_shape=[
            jax.ShapeDtypeStruct((b, n_rows, NSA_KV), BF16),
            jax.ShapeDtypeStruct((b, NSA_KV, n_rows), BF16),
        ],
        compiler_params=_cparams("parallel"),
        name="compress",
    )(kvc3, kvc3, kvc3, kvc3, pe, w1, wp, w2k, w2vt)


def _pad_query(qt, g):
    z = jnp.zeros_like(qt)
    return jnp.concatenate([qt, z] if g % 2 == 0 else [z, qt], axis=0)


def _cmpsel_kernel(qt_ref, kc_ref, vct_ref, ocmpt_ref, selb_ref, *, tq, n_sel, topk):
    nc = kc_ref.shape[0]
    t0 = pl.program_id(1) * tq

    n_t = lax.broadcasted_iota(jnp.int32, (nc, tq), 0)
    t_t = t0 + lax.broadcasted_iota(jnp.int32, (nc, tq), 1)
    mask_t = (n_t * CMP_STRIDE + (CMP_BLOCK - 1)) <= t_t

    j_m = lax.broadcasted_iota(jnp.int32, (n_sel, nc), 0) * SEL_BLOCK
    c_m = lax.broadcasted_iota(jnp.int32, (n_sel, nc), 1) * CMP_STRIDE
    ov = jnp.minimum(c_m + CMP_BLOCK, j_m + SEL_BLOCK) - jnp.maximum(c_m, j_m)
    map_t = jnp.maximum(ov, 0).astype(F32) * (1.0 / CMP_BLOCK)

    j_s = lax.broadcasted_iota(jnp.int32, (n_sel, tq), 0)
    qblk = (t0 + lax.broadcasted_iota(jnp.int32, (n_sel, tq), 1)) // SEL_BLOCK
    rel = qblk - j_s
    causal = rel >= 0
    forced = causal & ((j_s == 0) | (rel < N_LOCAL_FORCED))

    for g in range(NSA_GROUPS):
        pair = g // 2
        kcp = kc_ref[:, pair * LANE:(pair + 1) * LANE]
        vct = vct_ref[g * NSA_DH:(g + 1) * NSA_DH, :]
        psum_t = jnp.zeros((nc, tq), F32)
        sts = [jnp.dot(kcp, _pad_query(qt_ref[(g * NSA_HPG + h) * NSA_DH:(g * NSA_HPG + h + 1) * NSA_DH, :], g),
                       preferred_element_type=F32) for h in range(NSA_HPG)]
        for h in range(NSA_HPG):
            hh = g * NSA_HPG + h
            st = jnp.where(mask_t, sts[h], NEG_INF)
            et = jnp.where(mask_t, jnp.exp2(st - jnp.max(st, axis=0, keepdims=True)), 0.0)
            lt = jnp.sum(et, axis=0, keepdims=True)
            pt = et * (1.0 / jnp.where(lt > 0.0, lt, 1.0))
            psum_t = psum_t + pt
            ot = jnp.dot(vct, pt.astype(BF16), preferred_element_type=F32)
            ocmpt_ref[hh * NSA_DH:(hh + 1) * NSA_DH, :] = ot.astype(BF16)
        imp_t = jnp.dot(map_t, psum_t, preferred_element_type=F32,
                        precision=lax.Precision.HIGHEST)
        score = jnp.where(forced, BIG, jnp.where(causal, imp_t, NEG_INF))
        rank = jnp.zeros((n_sel, tq), jnp.int32)
        for jp in range(n_sel):
            row = score[jp:jp + 1, :]
            before = (row > score) | ((row == score) & (j_s > jp))
            rank = rank + before.astype(jnp.int32)
        chosen = (rank < topk) & (score > 0.5 * NEG_INF)
        selb_ref[g * n_sel:(g + 1) * n_sel, :] = jnp.where(chosen, 0.0, NEG_INF)


def _cmpsel(zt, kc, vct, tq=256):
    b, _, s = zt.shape
    n_sel = s // SEL_BLOCK
    topk = min(SEL_TOPK, n_sel)
    nc = kc.shape[1]
    kern = functools.partial(_cmpsel_kernel, tq=tq, n_sel=n_sel, topk=topk)
    return pl.pallas_call(
        kern,
        grid=(b, s // tq),
        in_specs=[
            pl.BlockSpec((None, NSA_Q, tq), lambda i, j: (i, ZT_Q // NSA_Q, j)),
            pl.BlockSpec((None, nc, NSA_KV), lambda i, j: (i, 0, 0)),
            pl.BlockSpec((None, NSA_KV, nc), lambda i, j: (i, 0, 0)),
        ],
        out_specs=[
            pl.BlockSpec((None, NSA_Q, tq), lambda i, j: (i, 0, j)),
            pl.BlockSpec((None, NSA_GROUPS * n_sel, tq), lambda i, j: (i, 0, j)),
        ],
        out_shape=[
            jax.ShapeDtypeStruct((b, NSA_Q, s), BF16),
            jax.ShapeDtypeStruct((b, NSA_GROUPS * n_sel, s), F32),
        ],
        compiler_params=_cparams("parallel", "parallel"),
        name="cmpsel",
    )(zt, kc, vct)


def _nsa_kernel(qt_ref, ks_ref, kw_ref, vst_ref, vwt_ref, selb_ref, ocmpt_ref, gt_ref, y_ref,
                yt_scr, qp_scr, m_scr, acc_scr, st_scr, mx_scr, bias_scr, *, tq, n_sel, look):
    tk = tq
    nw = WINDOW // tk
    sel_per_tile = tk // SEL_BLOCK
    den_rows = 16
    i = pl.program_id(1)
    d0 = (lax.broadcasted_iota(jnp.int32, (tk, tq), 1)
          - lax.broadcasted_iota(jnp.int32, (tk, tq), 0))
    caus = jnp.where(d0 >= 0, 0.0, NEG_INF)
    lowb = jnp.where(d0 < 0, 0.0, NEG_INF)
    ones_rows = jnp.ones((den_rows, tk), BF16)
    ext = 16
    key_blk = lax.broadcasted_iota(jnp.int32, (tk, ext), 0) // SEL_BLOCK
    ext_col = lax.broadcasted_iota(jnp.int32, (tk, ext), 1)
    gates = _sigmoid(gt_ref[...])

    for hh in range(NSA_HEADS):
        qp_scr[hh] = _pad_query(qt_ref[hh * NSA_DH:(hh + 1) * NSA_DH, :], hh // NSA_HPG)
    m_scr[...] = jnp.full(m_scr.shape, NEG_INF, F32)
    acc_scr[...] = jnp.zeros(acc_scr.shape, F32)

    def run(br, k_ref, vt_ref, sel_on, tiles):
        loaded = []
        for ti, (kt, mask) in enumerate(tiles):
            r0 = pl.multiple_of(kt * tk, tk)
            ktiles = [k_ref[pl.ds(r0, tk), pair * LANE:(pair + 1) * LANE]
                      for pair in range(NSA_GROUPS // 2)]
            vt_augs = [jnp.concatenate([vt_ref[g * NSA_DH:(g + 1) * NSA_DH, pl.ds(r0, tk)], ones_rows],
                                       axis=0) for g in range(NSA_GROUPS)]
            loaded.append((ktiles, vt_augs, None))
            if sel_on:
                row0 = kt * sel_per_tile
                slab0 = pl.multiple_of((row0 // SLAB_ROWS) * SLAB_ROWS, SLAB_ROWS)
                hot = jnp.where(key_blk + row0 % SLAB_ROWS == ext_col, 1.0, 0.0).astype(BF16)
                ktiles = [jnp.concatenate([kp, hot], axis=1) for kp in ktiles]
                slabs = [jnp.concatenate(
                    [selb_ref[pl.ds(g * n_sel + slab0, SLAB_ROWS), :],
                     jnp.zeros((ext - SLAB_ROWS, tq), F32)], axis=0).astype(BF16)
                    for g in range(NSA_GROUPS)]
                loaded[-1] = (ktiles, vt_augs, slabs)
            if mask is not None:
                bias_scr[ti] = mask

        def qk(ti, hh):
            g = hh // NSA_HPG
            q = qp_scr[hh]
            if sel_on:
                q = jnp.concatenate([q, loaded[ti][2][g]], axis=0)
            st = jnp.dot(loaded[ti][0][g // 2], q, preferred_element_type=F32)
            if tiles[ti][1] is not None:
                st = st + bias_scr[ti]
            st_scr[ti * NSA_HEADS + hh] = st
            mx_scr[ti * NSA_HEADS + hh] = jnp.max(st, axis=0, keepdims=True)

        def softmax_pv(ti, hh):
            vt_aug = loaded[ti][1][hh // NSA_HPG]
            m = m_scr[br, hh]
            m_new = jnp.maximum(m, mx_scr[ti * NSA_HEADS + hh])
            alpha = jnp.exp2(m - m_new)
            p = jnp.exp2(st_scr[ti * NSA_HEADS + hh] - m_new).astype(BF16)
            acc_scr[br, hh] = alpha * acc_scr[br, hh] + jnp.dot(vt_aug, p, preferred_element_type=F32)
            m_scr[br, hh] = m_new

        seq = [(ti, hh) for ti in range(len(tiles)) for hh in range(NSA_HEADS)]
        for pos in range(min(look, len(seq))):
            qk(*seq[pos])
        for pos, item in enumerate(seq):
            if pos + look < len(seq):
                qk(*seq[pos + look])
            softmax_pv(*item)

    def slc_body(j, c):
        run(0, ks_ref, vst_ref, True, [(2 * j, None), (2 * j + 1, None)])
        return c

    lax.fori_loop(0, i // 2, slc_body, 0)

    @pl.when(i % 2 == 1)
    def _():
        run(0, ks_ref, vst_ref, True, [(i - 1, None), (i, caus)])

    @pl.when(i % 2 == 0)
    def _():
        run(0, ks_ref, vst_ref, True, [(i, caus)])

    for v in range(nw):
        @pl.when(i == v)
        def _(v=v):
            run(1, kw_ref, vwt_ref, False, [(kt, None) for kt in range(v)] + [(i, caus)])

    @pl.when(i >= nw)
    def _():
        run(1, kw_ref, vwt_ref, False,
            [(i - nw, lowb)] + [(i - nw + d, None) for d in range(1, nw)] + [(i, caus)])

    for hh in range(NSA_HEADS):
        acc_s = acc_scr[0, hh]
        acc_w = acc_scr[1, hh]
        o_slc = acc_s[0:NSA_DH] * (1.0 / acc_s[NSA_DH:NSA_DH + 1])
        o_win = acc_w[0:NSA_DH] * (1.0 / acc_w[NSA_DH:NSA_DH + 1])
        o_cmp = ocmpt_ref[hh * NSA_DH:(hh + 1) * NSA_DH, :].astype(F32)
        gc = ZG_GNSA + 3 * hh
        yt_scr[hh * NSA_DH:(hh + 1) * NSA_DH, :] = (
            gates[gc:gc + 1] * o_cmp + gates[gc + 1:gc + 2] * o_slc + gates[gc + 2:gc + 3] * o_win)

    y_ref[...] = yt_scr[...].T.astype(BF16)


def _nsa(zt, zb3, selb, ocmpt, zgt, tq=256, look=5):
    b, _, s = zt.shape
    n_sel = s // SEL_BLOCK
    tk = tq
    den_rows = 16
    max_tiles = max(2, WINDOW // tk + 1)
    assert WINDOW % tq == 0 and tq % SEL_BLOCK == 0 and SLAB_ROWS % (tk // SEL_BLOCK) == 0
    kern = functools.partial(_nsa_kernel, tq=tq, n_sel=n_sel, look=look)
    return pl.pallas_call(
        kern,
        grid=(b, s // tq),
        in_specs=[
            pl.BlockSpec((None, NSA_Q, tq), lambda bi, j: (bi, ZT_Q // NSA_Q, j)),
            pl.BlockSpec((None, s, NSA_KV), lambda bi, j: (bi, 0, ZB_KSLC // NSA_KV)),
            pl.BlockSpec((None, s, NSA_KV), lambda bi, j: (bi, 0, ZB_KWIN // NSA_KV)),
            pl.BlockSpec((None, NSA_KV, s), lambda bi, j: (bi, ZT_VSLC // NSA_KV, 0)),
            pl.BlockSpec((None, NSA_KV, s), lambda bi, j: (bi, ZT_VWIN // NSA_KV, 0)),
            pl.BlockSpec((None, NSA_GROUPS * n_sel, tq), lambda bi, j: (bi, 0, j)),
            pl.BlockSpec((None, NSA_Q, tq), lambda bi, j: (bi, 0, j)),
            pl.BlockSpec((None, ZG_WIDTH, tq), lambda bi, j: (bi, 0, j)),
        ],
        out_specs=pl.BlockSpec((None, tq, NSA_Q), lambda bi, j: (bi, j, 0)),
        out_shape=jax.ShapeDtypeStruct((b, s, NSA_Q), BF16),
        scratch_shapes=[
            pltpu.VMEM((NSA_Q, tq), F32),
            pltpu.VMEM((NSA_HEADS, 2 * NSA_DH, tq), BF16),
            pltpu.VMEM((2, NSA_HEADS, 1, tq), F32),
            pltpu.VMEM((2, NSA_HEADS, NSA_DH + den_rows, tq), F32),
            pltpu.VMEM((max_tiles * NSA_HEADS, tk, tq), F32),
            pltpu.VMEM((max_tiles * NSA_HEADS, 1, tq), F32),
            pltpu.VMEM((max_tiles, tk, tq), F32),
        ],
        compiler_params=_cparams("parallel", "arbitrary"),
        name="nsa",
    )(zt, zb3, zb3, zt, zt, selb, ocmpt, zgt)


def _mlstm_kernel(q_ref, k_ref, v_ref, o_ref, zg_ref, cw_ref, cb_ref, gb_ref, hg_ref, y_ref,
                  xbuf, c_st, n_st, m_st, *, nb):
    L = ML_CHUNK
    pad = 8
    @pl.when(pl.program_id(1) == 0)
    def _():
        xbuf[:, 0:pad, :] = jnp.zeros((nb, pad, 2 * ML_W), F32)
        c_st[...] = jnp.zeros_like(c_st)
        n_st[...] = jnp.zeros_like(n_st)
        m_st[...] = jnp.zeros_like(m_st)

    row = lax.broadcasted_iota(jnp.int32, (L, L), 0)
    col = lax.broadcasted_iota(jnp.int32, (L, L), 1)
    tril = row >= col
    tril_f = jnp.where(tril, 1.0, 0.0)
    streams = [(bb, h) for bb in range(nb) for h in range(ML_HEADS)]

    qks = []
    for bb in range(nb):
        xbuf[bb, pad:pad + L, 0:ML_W] = q_ref[bb].astype(F32)
        xbuf[bb, pad:pad + L, ML_W:2 * ML_W] = k_ref[bb].astype(F32)
        conv = jnp.zeros((L, 2 * ML_W), F32) + cb_ref[...]
        for j in range(CONV_WIDTH):
            conv = conv + xbuf[bb, pl.ds(pad - (CONV_WIDTH - 1) + j, L), :] * cw_ref[j:j + 1, :]
        xbuf[bb, 0:pad, :] = xbuf[bb, L:L + pad, :]
        qks.append(_silu(conv))

    qs = {(bb, h): qks[bb][:, h * ML_DH:(h + 1) * ML_DH] for bb, h in streams}
    ks = {(bb, h): qks[bb][:, ML_W + h * ML_DH:ML_W + (h + 1) * ML_DH] * (ML_DH ** -0.5)
          for bb, h in streams}
    qbs = {s: qs[s].astype(BF16) for s in streams}
    vbs = {(bb, h): v_ref[bb, :, h * ML_DH:(h + 1) * ML_DH] for bb, h in streams}
    c_olds = {(bb, h): c_st[bb * ML_HEADS + h] for bb, h in streams}
    n_olds = {(bb, h): n_st[bb, h:h + 1, :] for bb, h in streams}
    s_qk = {s: lax.dot_general(qbs[s], ks[s].astype(BF16), _NT, preferred_element_type=F32)
            for s in streams}
    cqs = {s: lax.dot_general(qbs[s], c_olds[s].astype(BF16), _NT, preferred_element_type=F32)
           for s in streams}

    g = []
    for bb in range(nb):
        gates = zg_ref[bb] + gb_ref[...]
        bcum = jnp.dot(tril_f, _log_sigmoid(gates), preferred_element_type=F32,
                       precision=lax.Precision.HIGHEST)
        b_al = pltpu.roll(bcum, shift=ZG_WIDTH - ML_HEADS, axis=1)
        m_prev = m_st[bb, 0:1, :]
        b_end = b_al[L - 1:L, :]
        inter = b_al + m_prev
        wlog = b_end - b_al + gates
        m_new = jnp.maximum(b_end + m_prev, jnp.max(wlog, axis=0, keepdims=True))
        ws = jnp.exp(wlog - m_new)
        decay = jnp.exp(b_end + m_prev - m_new)
        m_st[bb, 0:1, :] = m_new
        r_t = (gates - b_al).T
        g.append((b_al, inter, ws, decay, r_t))

    for bb, h in streams:
        c = ZG_IF + h
        _, _, ws, decay, _ = g[bb]
        kw = ks[bb, h] * ws[:, c:c + 1]
        upd = lax.dot_general(vbs[bb, h], kw.astype(BF16), _TN, preferred_element_type=F32)
        c_st[bb * ML_HEADS + h] = decay[:, c:c + 1] * c_olds[bb, h] + upd
        n_st[bb, h:h + 1, :] = decay[:, c:c + 1] * n_olds[bb, h] + jnp.sum(kw, axis=0, keepdims=True)

    for bb, h in streams:
        c = ZG_IF + h
        b_al, inter, _, _, r_t = g[bb]
        dlog = jnp.where(tril, b_al[:, c:c + 1] + r_t[c:c + 1, :], -jnp.inf)
        inter_c = inter[:, c:c + 1]
        m_t = jnp.maximum(inter_c, jnp.max(dlog, axis=-1, keepdims=True))
        dw = jnp.exp(dlog - m_t)
        iw = jnp.exp(inter_c - m_t)
        sqk = s_qk[bb, h] * dw
        num = jnp.dot(sqk.astype(BF16), vbs[bb, h], preferred_element_type=F32) + iw * cqs[bb, h]
        den = (jnp.sum(sqk, axis=-1, keepdims=True)
               + iw * jnp.sum(qs[bb, h] * n_olds[bb, h], axis=-1, keepdims=True))
        hs = num / jnp.maximum(jnp.abs(den), jnp.exp(-m_t))
        hn = hs * lax.rsqrt(jnp.mean(hs * hs, axis=-1, keepdims=True) + EPS)
        hn = hn * hg_ref[:, h * ML_DH:(h + 1) * ML_DH]
        og = _sigmoid(o_ref[bb, :, h * ML_DH:(h + 1) * ML_DH].astype(F32))
        y_ref[bb, :, h * ML_DH:(h + 1) * ML_DH] = (og * hn).astype(BF16)


def _mlstm(zb3, zg3, conv_w, conv_b, gate_b, head_g, nb=1):
    b, s, _ = zb3.shape
    L = ML_CHUNK
    qb = ZB_QKVML // ML_W
    assert b % nb == 0

    def zspec(k):
        return pl.BlockSpec((nb, L, ML_W), lambda bi, c, k=k: (bi, c, k))

    return pl.pallas_call(
        functools.partial(_mlstm_kernel, nb=nb),
        grid=(b // nb, s // L),
        in_specs=[
            zspec(qb), zspec(qb + 1), zspec(qb + 2), zspec(ZB_OML // ML_W),
            pl.BlockSpec((nb, L, ZG_WIDTH), lambda bi, c: (bi, c, 0)),
            pl.BlockSpec((CONV_WIDTH, 2 * ML_W), lambda bi, c: (0, 0)),
            pl.BlockSpec((1, 2 * ML_W), lambda bi, c: (0, 0)),
            pl.BlockSpec((1, ZG_WIDTH), lambda bi, c: (0, 0)),
            pl.BlockSpec((1, ML_W), lambda bi, c: (0, 0)),
        ],
        out_specs=pl.BlockSpec((nb, L, ML_W), lambda bi, c: (bi, c, 0)),
        out_shape=jax.ShapeDtypeStruct((b, s, ML_W), BF16),
        scratch_shapes=[
            pltpu.VMEM((nb, 8 + L, 2 * ML_W), F32),
            pltpu.VMEM((nb * ML_HEADS, ML_DH, ML_DH), F32),
            pltpu.VMEM((nb, 8, ML_DH), F32),
            pltpu.VMEM((nb, 8, LANE), F32),
        ],
        compiler_params=_cparams("parallel", "arbitrary"),
        name="mlstm",
    )(zb3, zb3, zb3, zb3, zg3, conv_w, conv_b, gate_b, head_g)


def _memkv_kernel(mem_ref, g_ref, w_ref, kv_ref):
    hb = _rms(mem_ref[...], g_ref[...]).astype(BF16)
    kv_ref[...] = jnp.dot(hb, w_ref[...], preferred_element_type=F32).astype(BF16)


def _memkv(mem, g, wb):
    b, m, _ = mem.shape
    return pl.pallas_call(
        _memkv_kernel,
        grid=(b,),
        in_specs=[
            pl.BlockSpec((None, m, D_MODEL), lambda i: (i, 0, 0)),
            pl.BlockSpec((1, D_MODEL), lambda i: (0, 0)),
            pl.BlockSpec((D_MODEL, 2 * MEM_W), lambda i: (0, 0)),
        ],
        out_specs=pl.BlockSpec((None, m, 2 * MEM_W), lambda i: (i, 0, 0)),
        out_shape=jax.ShapeDtypeStruct((b, m, 2 * MEM_W), BF16),
        compiler_params=_cparams("parallel"),
        name="memkv",
    )(mem, g, wb)


def _memattn_kernel(q_ref, kv_ref, y_ref):
    for h in range(MEM_HEADS):
        q = q_ref[:, h * MEM_DH:(h + 1) * MEM_DH]
        k = kv_ref[:, h * MEM_DH:(h + 1) * MEM_DH]
        v = kv_ref[:, MEM_W + h * MEM_DH:MEM_W + (h + 1) * MEM_DH]
        s = lax.dot_general(q, k, _NT, preferred_element_type=F32) * MEM_SCALE
        e = jnp.exp(s - jnp.max(s, axis=-1, keepdims=True))
        p = e / jnp.sum(e, axis=-1, keepdims=True)
        o = jnp.dot(p.astype(BF16), v, preferred_element_type=F32)
        y_ref[:, h * MEM_DH:(h + 1) * MEM_DH] = o.astype(BF16)


def _memattn(zb3, kvm, tq=512):
    b, s, _ = zb3.shape
    m = kvm.shape[1]
    return pl.pallas_call(
        _memattn_kernel,
        grid=(b, s // tq),
        in_specs=[
            pl.BlockSpec((None, tq, MEM_W), lambda i, j: (i, j, ZB_QMEM // MEM_W)),
            pl.BlockSpec((None, m, 2 * MEM_W), lambda i, j: (i, 0, 0)),
        ],
        out_specs=pl.BlockSpec((None, tq, MEM_W), lambda i, j: (i, j, 0)),
        out_shape=jax.ShapeDtypeStruct((b, s, MEM_W), BF16),
        compiler_params=_cparams("parallel", "parallel"),
        name="memattn",
    )(zb3, kvm)


def _merge_kernel(x_ref, yn_ref, yl_ref, ym_ref, g0_ref, g1_ref, g2_ref,
                  wn_ref, wl_ref, wm_ref, wo_ref, gp_ref, out_ref):
    y = _sigmoid(g0_ref[...].astype(F32)) * jnp.dot(yn_ref[...], wn_ref[...], preferred_element_type=F32)
    y = y + _sigmoid(g1_ref[...].astype(F32)) * jnp.dot(yl_ref[...], wl_ref[...], preferred_element_type=F32)
    y = y + _sigmoid(g2_ref[...].astype(F32)) * jnp.dot(ym_ref[...], wm_ref[...], preferred_element_type=F32)
    u = jnp.dot(y.astype(BF16), wo_ref[...], preferred_element_type=F32)
    out_ref[...] = x_ref[...] + _rms(u, gp_ref[...])


def _merge(x2, yn, yl, ym, zb, wn, wl, wm, wo, gp, tm=512):
    t = x2.shape[0]
    gm = ZB_GMERGE // D_MODEL

    def const(shape):
        return pl.BlockSpec(shape, lambda i: (0, 0))

    return pl.pallas_call(
        _merge_kernel,
        grid=(t // tm,),
        in_specs=[
            pl.BlockSpec((tm, D_MODEL), lambda i: (i, 0)),
            pl.BlockSpec((tm, NSA_Q), lambda i: (i, 0)),
            pl.BlockSpec((tm, ML_W), lambda i: (i, 0)),
            pl.BlockSpec((tm, MEM_W), lambda i: (i, 0)),
            pl.BlockSpec((tm, D_MODEL), lambda i: (i, gm)),
            pl.BlockSpec((tm, D_MODEL), lambda i: (i, gm + 1)),
            pl.BlockSpec((tm, D_MODEL), lambda i: (i, gm + 2)),
            const((NSA_Q, D_MODEL)), const((ML_W, D_MODEL)), const((MEM_W, D_MODEL)),
            const((D_MODEL, D_MODEL)), const((1, D_MODEL)),
        ],
        out_specs=pl.BlockSpec((tm, D_MODEL), lambda i: (i, 0)),
        out_shape=jax.ShapeDtypeStruct((t, D_MODEL), F32),
        compiler_params=_cparams("parallel"),
        name="merge",
    )(x2, yn, yl, ym, zb, zb, zb, wn, wl, wm, wo, gp)


def _ffn_kernel(x_ref, gpre_ref, wg_ref, wu_ref, wd_ref, gpost_ref, out_ref, h_ref, acc_ref):
    j = pl.program_id(1)

    @pl.when(j == 0)
    def _():
        h_ref[...] = _rms(x_ref[...], gpre_ref[...]).astype(BF16)
        acc_ref[...] = jnp.zeros_like(acc_ref)

    h = h_ref[...]
    gate = jnp.dot(h, wg_ref[...], preferred_element_type=F32)
    up = jnp.dot(h, wu_ref[...], preferred_element_type=F32)
    act = (_silu(gate) * up).astype(BF16)
    acc_ref[...] += jnp.dot(act, wd_ref[...], preferred_element_type=F32)

    @pl.when(j == pl.num_programs(1) - 1)
    def _():
        out_ref[...] = x_ref[...] + _rms(acc_ref[...], gpost_ref[...])


def _ffn(x2, gpre, w_in, w_down, gpost, tm=512, nf=1):
    t = x2.shape[0]
    tf = D_FF // nf
    assert tf % LANE == 0
    wmode = dict(pipeline_mode=pl.Buffered(1)) if nf == 1 else {}
    return pl.pallas_call(
        _ffn_kernel,
        grid=(t // tm, nf),
        in_specs=[
            pl.BlockSpec((tm, D_MODEL), lambda i, j: (i, 0)),
            pl.BlockSpec((1, D_MODEL), lambda i, j: (0, 0)),
            pl.BlockSpec((D_MODEL, tf), lambda i, j: (0, j), **wmode),
            pl.BlockSpec((D_MODEL, tf), lambda i, j: (0, nf + j), **wmode),
            pl.BlockSpec((tf, D_MODEL), lambda i, j: (j, 0), **wmode),
            pl.BlockSpec((1, D_MODEL), lambda i, j: (0, 0)),
        ],
        out_specs=pl.BlockSpec((tm, D_MODEL), lambda i, j: (i, 0)),
        out_shape=jax.ShapeDtypeStruct((t, D_MODEL), F32),
        scratch_shapes=[pltpu.VMEM((tm, D_MODEL), BF16), pltpu.VMEM((tm, D_MODEL), F32)],
        compiler_params=_cparams("parallel", "arbitrary"),
        name="ffn",
    )(x2, gpre, w_in, w_in, w_down, gpost)


def _layer(x, mem, p):
    b, s, _ = x.shape
    t = b * s
    x2 = x.reshape(t, D_MODEL)

    wn, wt, wg, wc = _wprep(p["w_in"].T)
    zt, zb, zg, zgt, kvc = _inproj(x2, p["g_pre_mix"].reshape(1, D_MODEL), wt, wn, wg, wc, b, s)
    zb3 = zb.reshape(b, s, ZB_WIDTH)
    zg3 = zg.reshape(b, s, ZG_WIDTH)

    half = (CMP_BLOCK // 2) * NSA_DH
    pe = jnp.stack([p["cmp_pe_k"], p["cmp_pe_v"]]).reshape(2, 1, 2 * half)
    pe = jnp.pad(pe, ((0, 0), (0, 7), (0, 0))).astype(BF16)
    w1 = jnp.stack([p["cmp_w1_k"], p["cmp_w1_v"]])
    eye2 = jnp.eye(2, dtype=F32)
    wpa = jnp.einsum("ab,kldh->kladbh", eye2, w1[:, :CMP_STRIDE]).reshape(2, CMP_STRIDE, LANE, 2 * CMP_HIDDEN)
    wpb = jnp.einsum("ab,kldh->kladbh", eye2, w1[:, CMP_STRIDE:]).reshape(2, CMP_STRIDE, LANE, 2 * CMP_HIDDEN)
    wp = jnp.concatenate([wpa, wpb], axis=-1).astype(BF16)
    kc, vct = _compress(kvc.reshape(b, s, KVC_WIDTH), pe, w1.reshape(2, 2 * half, CMP_HIDDEN).astype(BF16), wp,
                        p["cmp_w2_k"].astype(BF16), p["cmp_w2_v"].T.astype(BF16))
    ocmpt, selb = _cmpsel(zt, kc, vct)
    y_nsa = _nsa(zt, zb3, selb, ocmpt, zgt)

    gate_b = jnp.zeros((1, ZG_WIDTH), F32).at[0, ZG_IF:ZG_IF + 2 * ML_HEADS].set(p["ml_gate_b"])
    y_ml = _mlstm(zb3, zg3, p["ml_conv_w"], p["ml_conv_b"].reshape(1, 2 * ML_W), gate_b,
                  p["ml_head_g"].reshape(1, ML_W))

    kvm = _memkv(mem, p["g_mem"].reshape(1, D_MODEL), p["w_mem_kv"].astype(BF16))
    y_mem = _memattn(zb3, kvm)

    x1 = _merge(x2, y_nsa.reshape(t, NSA_Q), y_ml.reshape(t, ML_W), y_mem.reshape(t, MEM_W), zb,
                p["w_proj_nsa"].astype(BF16), p["w_proj_ml"].astype(BF16),
                p["w_proj_mem"].astype(BF16), p["w_out"].astype(BF16),
                p["g_post_mix"].reshape(1, D_MODEL))
    x2o = _ffn(x1, p["g_pre_ffn"].reshape(1, D_MODEL), p["w_ffn_in"].astype(BF16),
               p["w_ffn_down"].astype(BF16), p["g_post_ffn"].reshape(1, D_MODEL))
    return x2o.reshape(b, s, D_MODEL)


def kernel(x, mem, g_pre_mix, w_in, cmp_pe_k, cmp_w1_k, cmp_w2_k, cmp_pe_v, cmp_w1_v, cmp_w2_v,
           ml_conv_w, ml_conv_b, ml_gate_b, ml_head_g, g_mem, w_mem_kv, w_proj_nsa, w_proj_ml,
           w_proj_mem, w_out, g_post_mix, g_pre_ffn, w_ffn_in, w_ffn_down, g_post_ffn):
    params = dict(
        g_pre_mix=g_pre_mix, w_in=w_in, cmp_pe_k=cmp_pe_k, cmp_w1_k=cmp_w1_k, cmp_w2_k=cmp_w2_k,
        cmp_pe_v=cmp_pe_v, cmp_w1_v=cmp_w1_v, cmp_w2_v=cmp_w2_v, ml_conv_w=ml_conv_w,
        ml_conv_b=ml_conv_b, ml_gate_b=ml_gate_b, ml_head_g=ml_head_g, g_mem=g_mem,
        w_mem_kv=w_mem_kv, w_proj_nsa=w_proj_nsa, w_proj_ml=w_proj_ml, w_proj_mem=w_proj_mem,
        w_out=w_out, g_post_mix=g_post_mix, g_pre_ffn=g_pre_ffn, w_ffn_in=w_ffn_in,
        w_ffn_down=w_ffn_down, g_post_ffn=g_post_ffn)
    depth = w_in.shape[0]
    for l in range(depth):
        x = _layer(x, mem, {k: v[l] for k, v in params.items()})
    return x
```

```python
import functools
import math

import jax
import jax.numpy as jnp
from jax import lax
from jax.experimental import pallas as pl
from jax.experimental.pallas import tpu as pltpu

F32 = jnp.float32
BF16 = jnp.bfloat16

D_MODEL = 1024
EPS = 1e-6
NEG_INF = -1e30
BIG = 1e30

NSA_HEADS = 16
NSA_GROUPS = 4
NSA_HPG = NSA_HEADS // NSA_GROUPS
NSA_DH = 64
NSA_SCALE = NSA_DH ** -0.5
CMP_BLOCK = 32
CMP_STRIDE = 16
CMP_HIDDEN = 128
SEL_BLOCK = 64
SEL_TOPK = 16
N_LOCAL_FORCED = 2
WINDOW = 512

ML_HEADS = 4
ML_DH = 128
ML_CHUNK = 128
CONV_WIDTH = 4

MEM_HEADS = 4
MEM_DH = 128
MEM_SCALE = MEM_DH ** -0.5

D_FF = -(-8 * D_MODEL // (3 * 256)) * 256

NSA_Q = NSA_HEADS * NSA_DH
NSA_KV = NSA_GROUPS * NSA_DH
ML_W = ML_HEADS * ML_DH
MEM_W = MEM_HEADS * MEM_DH

_OFF_Q = 0
_OFF_KV = _OFF_Q + NSA_Q
_OFF_GNSA = _OFF_KV + 6 * NSA_KV
_OFF_QKVML = _OFF_GNSA + 3 * NSA_HEADS
_OFF_IF = _OFF_QKVML + 3 * ML_W
_OFF_OML = _OFF_IF + 2 * ML_HEADS
_OFF_QMEM = _OFF_OML + ML_W
_OFF_GMERGE = _OFF_QMEM + MEM_W
_IN_WIDTH = _OFF_GMERGE + 3 * D_MODEL

ZB_GMERGE = 0
ZB_KSLC = ZB_GMERGE + 3 * D_MODEL
ZB_KWIN = ZB_KSLC + NSA_KV
ZB_QKVML = ZB_KWIN + NSA_KV
ZB_OML = ZB_QKVML + 3 * ML_W
ZB_QMEM = ZB_OML + ML_W
ZB_WIDTH = ZB_QMEM + MEM_W
KVC_WIDTH = 2 * NSA_KV
ZT_Q = 0
ZT_VSLC = ZT_Q + NSA_Q
ZT_VWIN = ZT_VSLC + NSA_KV
ZT_ROWS = ZT_VWIN + NSA_KV
ZG_WIDTH = 128
ZG_GNSA = 0
ZG_IF = 3 * NSA_HEADS

LANE = 128
SLAB_ROWS = 8
VMEM_LIMIT = 56 * 1024 * 1024
LOG2E = math.log2(math.e)

_NT = (((1,), (1,)), ((), ()))
_TN = (((0,), (0,)), ((), ()))


def _cparams(*sem):
    return pltpu.CompilerParams(dimension_semantics=sem, vmem_limit_bytes=VMEM_LIMIT)


def _rms(x, g):
    return x * lax.rsqrt(jnp.mean(x * x, axis=-1, keepdims=True) + EPS) * g


def _sigmoid(x):
    return 1.0 / (1.0 + jnp.exp(-x))


def _silu(x):
    return x * _sigmoid(x)


def _log_sigmoid(x):
    return jnp.minimum(x, 0.0) - jnp.log(1.0 + jnp.exp(-jnp.abs(x)))


def _wprep_kernel(w_ref, wn_ref, wt_ref, wg_ref, wc_ref):
    def rows(lo, hi):
        return w_ref[lo:hi, :]

    def kv_rows(k):
        return rows(_OFF_KV + k * NSA_KV, _OFF_KV + (k + 1) * NSA_KV)

    wn_ref[ZB_GMERGE:ZB_KSLC, :] = rows(_OFF_GMERGE, _IN_WIDTH).astype(BF16)
    wn_ref[ZB_KSLC:ZB_KWIN, :] = kv_rows(2).astype(BF16)
    wn_ref[ZB_KWIN:ZB_QKVML, :] = kv_rows(4).astype(BF16)
    wn_ref[ZB_QKVML:ZB_OML, :] = rows(_OFF_QKVML, _OFF_IF).astype(BF16)
    wn_ref[ZB_OML:ZB_WIDTH, :] = rows(_OFF_OML, _OFF_GMERGE).astype(BF16)
    wt_ref[ZT_Q:ZT_VSLC, :] = (rows(_OFF_Q, _OFF_KV) * (NSA_SCALE * LOG2E)).astype(BF16)
    wt_ref[ZT_VSLC:ZT_VWIN, :] = kv_rows(3).astype(BF16)
    wt_ref[ZT_VWIN:ZT_ROWS, :] = kv_rows(5).astype(BF16)
    n_g = 3 * NSA_HEADS
    n_if = 2 * ML_HEADS
    wg_ref[...] = jnp.zeros(wg_ref.shape, BF16)
    wg_ref[ZG_GNSA:ZG_GNSA + n_g, :] = rows(_OFF_GNSA, _OFF_QKVML).astype(BF16)
    wg_ref[ZG_IF:ZG_IF + n_if, :] = rows(_OFF_IF, _OFF_OML).astype(BF16)
    wc_ref[...] = rows(_OFF_KV, _OFF_KV + KVC_WIDTH).astype(BF16)


def _wprep(w_t, tl=128):
    assert ZB_QMEM == ZB_OML + ML_W and _OFF_QMEM == _OFF_OML + ML_W
    return pl.pallas_call(
        _wprep_kernel,
        grid=(D_MODEL // tl,),
        in_specs=[pl.BlockSpec((_IN_WIDTH, tl), lambda i: (0, i))],
        out_specs=[
            pl.BlockSpec((ZB_WIDTH, tl), lambda i: (0, i)),
            pl.BlockSpec((ZT_ROWS, tl), lambda i: (0, i)),
            pl.BlockSpec((ZG_WIDTH, tl), lambda i: (0, i)),
            pl.BlockSpec((KVC_WIDTH, tl), lambda i: (0, i)),
        ],
        out_shape=[
            jax.ShapeDtypeStruct((ZB_WIDTH, D_MODEL), BF16),
            jax.ShapeDtypeStruct((ZT_ROWS, D_MODEL), BF16),
            jax.ShapeDtypeStruct((ZG_WIDTH, D_MODEL), BF16),
            jax.ShapeDtypeStruct((KVC_WIDTH, D_MODEL), BF16),
        ],
        compiler_params=_cparams("parallel"),
        name="wprep",
    )(w_t)


def _inproj_kernel(x_ref, g_ref, wt_ref, wn_ref, wg_ref, wc_ref,
                   zt_ref, zb_ref, zg_ref, zgt_ref, kvc_ref, h_ref, *, nt):
    j = pl.program_id(1)

    @pl.when(j == 0)
    def _():
        hb = _rms(x_ref[...], g_ref[...]).astype(BF16)
        h_ref[...] = hb
        zg_ref[...] = lax.dot_general(hb, wg_ref[...], _NT, preferred_element_type=F32)
        zgt_ref[...] = lax.dot_general(wg_ref[...], hb, _NT, preferred_element_type=F32)
        kvc_ref[...] = lax.dot_general(hb, wc_ref[...], _NT, preferred_element_type=F32)

    @pl.when(j < nt)
    def _():
        zt_ref[...] = lax.dot_general(wt_ref[...], h_ref[...], _NT,
                                      preferred_element_type=F32).astype(BF16)

    @pl.when(j >= nt)
    def _():
        tn = zb_ref.shape[1]
        r0 = pl.multiple_of((j - nt) * tn, tn)
        zb_ref[...] = lax.dot_general(h_ref[...], wn_ref[pl.ds(r0, tn), :], _NT,
                                      preferred_element_type=F32).astype(BF16)


def _inproj(x2, g, wt, wn, wg, wc, b, s, tm=1024, nt=1, nn=3):
    t = x2.shape[0]
    tr = ZT_ROWS // nt
    tn = ZB_WIDTH // nn
    assert tr * nt == ZT_ROWS and tn * nn == ZB_WIDTH and tn % LANE == 0 and s % tm == 0
    spb = s // tm
    kern = functools.partial(_inproj_kernel, nt=nt)
    once = dict(pipeline_mode=pl.Buffered(1))
    wt_spec = (pl.BlockSpec((tr, D_MODEL), lambda i, j: (0, 0), **once) if nt == 1 else
               pl.BlockSpec((tr, D_MODEL), lambda i, j: (jnp.minimum(j, nt - 1), 0)))
    return pl.pallas_call(
        kern,
        grid=(t // tm, nt + nn),
        in_specs=[
            pl.BlockSpec((tm, D_MODEL), lambda i, j: (i, 0)),
            pl.BlockSpec((1, D_MODEL), lambda i, j: (0, 0)),
            wt_spec,
            pl.BlockSpec((ZB_WIDTH, D_MODEL), lambda i, j: (0, 0), **once),
            pl.BlockSpec((ZG_WIDTH, D_MODEL), lambda i, j: (0, 0), **once),
            pl.BlockSpec((KVC_WIDTH, D_MODEL), lambda i, j: (0, 0), **once),
        ],
        out_specs=[
            pl.BlockSpec((None, tr, tm), lambda i, j: (i // spb, jnp.minimum(j, nt - 1), i % spb)),
            pl.BlockSpec((tm, tn), lambda i, j: (i, jnp.maximum(j - nt, 0))),
            pl.BlockSpec((tm, ZG_WIDTH), lambda i, j: (i, 0)),
            pl.BlockSpec((None, ZG_WIDTH, tm), lambda i, j: (i // spb, 0, i % spb)),
            pl.BlockSpec((tm, KVC_WIDTH), lambda i, j: (i, 0)),
        ],
        out_shape=[
            jax.ShapeDtypeStruct((b, ZT_ROWS, s), BF16),
            jax.ShapeDtypeStruct((t, ZB_WIDTH), BF16),
            jax.ShapeDtypeStruct((t, ZG_WIDTH), F32),
            jax.ShapeDtypeStruct((b, ZG_WIDTH, s), F32),
            jax.ShapeDtypeStruct((t, KVC_WIDTH), F32),
        ],
        scratch_shapes=[pltpu.VMEM((tm, D_MODEL), BF16)],
        compiler_params=_cparams("parallel", "arbitrary"),
        name="inproj",
    )(x2, g, wt, wn, wg, wc)


def _compress_kernel(a0_ref, a1_ref, a2_ref, a3_ref, pe_ref, w1_ref, wp_ref, w2k_ref, w2vt_ref,
                     kc_ref, vct_ref):
    a_refs = (a0_ref, a1_ref, a2_ref, a3_ref)
    n_rows = a0_ref.shape[0] // CMP_STRIDE
    for kind in range(2):
        pe = pe_ref[kind]
        c = jnp.dot(pe, w1_ref[kind], preferred_element_type=F32)[0:1, :]
        for pair in range(NSA_GROUPS // 2):
            a_ref = a_refs[kind * 2 + pair]
            acc = jnp.zeros((n_rows, 4 * CMP_HIDDEN), F32)
            for l in range(CMP_STRIDE):
                rows = a_ref[pl.ds(l, n_rows, stride=CMP_STRIDE), :].astype(BF16)
                acc = acc + jnp.dot(rows, wp_ref[kind, l], preferred_element_type=F32)
            for gi in range(2):
                g = 2 * pair + gi
                p = acc[:, gi * CMP_HIDDEN:(gi + 1) * CMP_HIDDEN]
                q = acc[:, (2 + gi) * CMP_HIDDEN:(3 + gi) * CMP_HIDDEN]
                qs = pltpu.roll(q, shift=n_rows - 1, axis=0)
                hid = _silu(p + qs + c).astype(BF16)
                if kind == 0:
                    o = jnp.dot(hid, w2k_ref[...], preferred_element_type=F32)
                    kc_ref[:, g * NSA_DH:(g + 1) * NSA_DH] = o.astype(BF16)
                else:
                    ot = lax.dot_general(w2vt_ref[...], hid, _NT, preferred_element_type=F32)
                    vct_ref[g * NSA_DH:(g + 1) * NSA_DH, :] = ot.astype(BF16)


def _compress(kvc3, pe, w1, wp, w2k, w2vt):
    b, s, _ = kvc3.shape
    n_rows = s // CMP_STRIDE
    half = (CMP_BLOCK // 2) * NSA_DH

    def col(k):
        return pl.BlockSpec((None, s, LANE), lambda i, k=k: (i, 0, k))

    return pl.pallas_call(
        _compress_kernel,
        grid=(b,),
        in_specs=[
            col(0), col(1), col(2), col(3),
            pl.BlockSpec((2, 8, 2 * half), lambda i: (0, 0, 0)),
            pl.BlockSpec((2, 2 * half, CMP_HIDDEN), lambda i: (0, 0, 0)),
            pl.BlockSpec((2, CMP_STRIDE, LANE, 4 * CMP_HIDDEN), lambda i: (0, 0, 0, 0)),
            pl.BlockSpec((CMP_HIDDEN, NSA_DH), lambda i: (0, 0)),
            pl.BlockSpec((NSA_DH, CMP_HIDDEN), lambda i: (0, 0)),
        ],
        out_specs=[
            pl.BlockSpec((None, n_rows, NSA_KV), lambda i: (i, 0, 0)),
            pl.BlockSpec((None, NSA_KV, n_rows), lambda i: (i, 0, 0)),
        ],
        out_shape=[
            jax.ShapeDtypeStruct((b, n_rows, NSA_KV), BF16),
            jax.ShapeDtypeStruct((b, NSA_KV, n_rows), BF16),
        ],
        compiler_params=_cparams("parallel"),
        name="compress",
    )(kvc3, kvc3, kvc3, kvc3, pe, w1, wp, w2k, w2vt)


def _pad_query(qt, g):
    z = jnp.zeros_like(qt)
    return jnp.concatenate([qt, z] if g % 2 == 0 else [z, qt], axis=0)


def _cmpsel_kernel(qt_ref, kc_ref, vct_ref, ocmpt_ref, selb_ref, *, tq, n_sel, topk):
    nc = kc_ref.shape[0]
    t0 = pl.program_id(1) * tq

    n_t = lax.broadcasted_iota(jnp.int32, (nc, tq), 0)
    t_t = t0 + lax.broadcasted_iota(jnp.int32, (nc, tq), 1)
    mask_t = (n_t * CMP_STRIDE + (CMP_BLOCK - 1)) <= t_t

    j_m = lax.broadcasted_iota(jnp.int32, (n_sel, nc), 0) * SEL_BLOCK
    c_m = lax.broadcasted_iota(jnp.int32, (n_sel, nc), 1) * CMP_STRIDE
    ov = jnp.minimum(c_m + CMP_BLOCK, j_m + SEL_BLOCK) - jnp.maximum(c_m, j_m)
    map_t = jnp.maximum(ov, 0).astype(F32) * (1.0 / CMP_BLOCK)

    j_s = lax.broadcasted_iota(jnp.int32, (n_sel, tq), 0)
    qblk = (t0 + lax.broadcasted_iota(jnp.int32, (n_sel, tq), 1)) // SEL_BLOCK
    rel = qblk - j_s
    causal = rel >= 0
    forced = causal & ((j_s == 0) | (rel < N_LOCAL_FORCED))

    for g in range(NSA_GROUPS):
        pair = g // 2
        kcp = kc_ref[:, pair * LANE:(pair + 1) * LANE]
        vct = vct_ref[g * NSA_DH:(g + 1) * NSA_DH, :]
        psum_t = jnp.zeros((nc, tq), F32)
        sts = [jnp.dot(kcp, _pad_query(qt_ref[(g * NSA_HPG + h) * NSA_DH:(g * NSA_HPG + h + 1) * NSA_DH, :], g),
                       preferred_element_type=F32) for h in range(NSA_HPG)]
        for h in range(NSA_HPG):
            hh = g * NSA_HPG + h
            st = jnp.where(mask_t, sts[h], NEG_INF)
            et = jnp.where(mask_t, jnp.exp2(st - jnp.max(st, axis=0, keepdims=True)), 0.0)
            lt = jnp.sum(et, axis=0, keepdims=True)
            pt = et * (1.0 / jnp.where(lt > 0.0, lt, 1.0))
            psum_t = psum_t + pt
            ot = jnp.dot(vct, pt.astype(BF16), preferred_element_type=F32)
            ocmpt_ref[hh * NSA_DH:(hh + 1) * NSA_DH, :] = ot.astype(BF16)
        imp_t = jnp.dot(map_t, psum_t, preferred_element_type=F32,
                        precision=lax.Precision.HIGHEST)
        score = jnp.where(forced, BIG, jnp.where(causal, imp_t, NEG_INF))
        rank = jnp.zeros((n_sel, tq), jnp.int32)
        for jp in range(n_sel):
            row = score[jp:jp + 1, :]
            before = (row > score) | ((row == score) & (j_s > jp))
            rank = rank + before.astype(jnp.int32)
        chosen = (rank < topk) & (score > 0.5 * NEG_INF)
        selb_ref[g * n_sel:(g + 1) * n_sel, :] = jnp.where(chosen, 0.0, NEG_INF)


def _cmpsel(zt, kc, vct, tq=256):
    b, _, s = zt.shape
    n_sel = s // SEL_BLOCK
    topk = min(SEL_TOPK, n_sel)
    nc = kc.shape[1]
    kern = functools.partial(_cmpsel_kernel, tq=tq, n_sel=n_sel, topk=topk)
    return pl.pallas_call(
        kern,
        grid=(b, s // tq),
        in_specs=[
            pl.BlockSpec((None, NSA_Q, tq), lambda i, j: (i, ZT_Q // NSA_Q, j)),
            pl.BlockSpec((None, nc, NSA_KV), lambda i, j: (i, 0, 0)),
            pl.BlockSpec((None, NSA_KV, nc), lambda i, j: (i, 0, 0)),
        ],
        out_specs=[
            pl.BlockSpec((None, NSA_Q, tq), lambda i, j: (i, 0, j)),
            pl.BlockSpec((None, NSA_GROUPS * n_sel, tq), lambda i, j: (i, 0, j)),
        ],
        out_shape=[
            jax.ShapeDtypeStruct((b, NSA_Q, s), BF16),
            jax.ShapeDtypeStruct((b, NSA_GROUPS * n_sel, s), F32),
        ],
        compiler_params=_cparams("parallel", "parallel"),
        name="cmpsel",
    )(zt, kc, vct)


def _nsa_kernel(qt_ref, ks_ref, kw_ref, vst_ref, vwt_ref, selb_ref, ocmpt_ref, gt_ref, y_ref,
                yt_scr, qp_scr, m_scr, acc_scr, st_scr, mx_scr, bias_scr, *, tq, n_sel, look):
    tk = tq
    nw = WINDOW // tk
    sel_per_tile = tk // SEL_BLOCK
    den_rows = 16
    i = pl.program_id(1)
    d0 = (lax.broadcasted_iota(jnp.int32, (tk, tq), 1)
          - lax.broadcasted_iota(jnp.int32, (tk, tq), 0))
    caus = jnp.where(d0 >= 0, 0.0, NEG_INF)
    lowb = jnp.where(d0 < 0, 0.0, NEG_INF)
    ones_rows = jnp.ones((den_rows, tk), BF16)
    ext = 16
    key_blk = lax.broadcasted_iota(jnp.int32, (tk, ext), 0) // SEL_BLOCK
    ext_col = lax.broadcasted_iota(jnp.int32, (tk, ext), 1)
    gates = _sigmoid(gt_ref[...])

    for hh in range(NSA_HEADS):
        qp_scr[hh] = _pad_query(qt_ref[hh * NSA_DH:(hh + 1) * NSA_DH, :], hh // NSA_HPG)
    m_scr[...] = jnp.full(m_scr.shape, NEG_INF, F32)
    acc_scr[...] = jnp.zeros(acc_scr.shape, F32)

    def run(br, k_ref, vt_ref, sel_on, tiles):
        loaded = []
        for ti, (kt, mask) in enumerate(tiles):
            r0 = pl.multiple_of(kt * tk, tk)
            ktiles = [k_ref[pl.ds(r0, tk), pair * LANE:(pair + 1) * LANE]
                      for pair in range(NSA_GROUPS // 2)]
            vt_augs = [jnp.concatenate([vt_ref[g * NSA_DH:(g + 1) * NSA_DH, pl.ds(r0, tk)], ones_rows],
                                       axis=0) for g in range(NSA_GROUPS)]
            loaded.append((ktiles, vt_augs, None))
            if sel_on:
                row0 = kt * sel_per_tile
                slab0 = pl.multiple_of((row0 // SLAB_ROWS) * SLAB_ROWS, SLAB_ROWS)
                hot = jnp.where(key_blk + row0 % SLAB_ROWS == ext_col, 1.0, 0.0).astype(BF16)
                ktiles = [jnp.concatenate([kp, hot], axis=1) for kp in ktiles]
                slabs = [jnp.concatenate(
                    [selb_ref[pl.ds(g * n_sel + slab0, SLAB_ROWS), :],
                     jnp.zeros((ext - SLAB_ROWS, tq), F32)], axis=0).astype(BF16)
                    for g in range(NSA_GROUPS)]
                loaded[-1] = (ktiles, vt_augs, slabs)
            if mask is not None:
                bias_scr[ti] = mask

        def qk(ti, hh):
            g = hh // NSA_HPG
            q = qp_scr[hh]
            if sel_on:
                q = jnp.concatenate([q, loaded[ti][2][g]], axis=0)
            st = jnp.dot(loaded[ti][0][g // 2], q, preferred_element_type=F32)
            if tiles[ti][1] is not None:
                st = st + bias_scr[ti]
            st_scr[ti * NSA_HEADS + hh] = st
            mx_scr[ti * NSA_HEADS + hh] = jnp.max(st, axis=0, keepdims=True)

        def softmax_pv(ti, hh):
            vt_aug = loaded[ti][1][hh // NSA_HPG]
            m = m_scr[br, hh]
            m_new = jnp.maximum(m, mx_scr[ti * NSA_HEADS + hh])
            alpha = jnp.exp2(m - m_new)
            p = jnp.exp2(st_scr[ti * NSA_HEADS + hh] - m_new).astype(BF16)
            acc_scr[br, hh] = alpha * acc_scr[br, hh] + jnp.dot(vt_aug, p, preferred_element_type=F32)
            m_scr[br, hh] = m_new

        seq = [(ti, hh) for ti in range(len(tiles)) for hh in range(NSA_HEADS)]
        for pos in range(min(look, len(seq))):
            qk(*seq[pos])
        for pos, item in enumerate(seq):
            if pos + look < len(seq):
                qk(*seq[pos + look])
            softmax_pv(*item)

    def slc_body(j, c):
        run(0, ks_ref, vst_ref, True, [(2 * j, None), (2 * j + 1, None)])
        return c

    lax.fori_loop(0, i // 2, slc_body, 0)

    @pl.when(i % 2 == 1)
    def _():
        run(0, ks_ref, vst_ref, True, [(i - 1, None), (i, caus)])

    @pl.when(i % 2 == 0)
    def _():
        run(0, ks_ref, vst_ref, True, [(i, caus)])

    for v in range(nw):
        @pl.when(i == v)
        def _(v=v):
            run(1, kw_ref, vwt_ref, False, [(kt, None) for kt in range(v)] + [(i, caus)])

    @pl.when(i >= nw)
    def _():
        run(1, kw_ref, vwt_ref, False,
            [(i - nw, lowb)] + [(i - nw + d, None) for d in range(1, nw)] + [(i, caus)])

    for hh in range(NSA_HEADS):
        acc_s = acc_scr[0, hh]
        acc_w = acc_scr[1, hh]
        o_slc = acc_s[0:NSA_DH] * (1.0 / acc_s[NSA_DH:NSA_DH + 1])
        o_win = acc_w[0:NSA_DH] * (1.0 / acc_w[NSA_DH:NSA_DH + 1])
        o_cmp = ocmpt_ref[hh * NSA_DH:(hh + 1) * NSA_DH, :].astype(F32)
        gc = ZG_GNSA + 3 * hh
        yt_scr[hh * NSA_DH:(hh + 1) * NSA_DH, :] = (
            gates[gc:gc + 1] * o_cmp + gates[gc + 1:gc + 2] * o_slc + gates[gc + 2:gc + 3] * o_win)

    y_ref[...] = yt_scr[...].T.astype(BF16)


def _nsa(zt, zb3, selb, ocmpt, zgt, tq=256, look=5):
    b, _, s = zt.shape
    n_sel = s // SEL_BLOCK
    tk = tq
    den_rows = 16
    max_tiles = max(2, WINDOW // tk + 1)
    assert WINDOW % tq == 0 and tq % SEL_BLOCK == 0 and SLAB_ROWS % (tk // SEL_BLOCK) == 0
    kern = functools.partial(_nsa_kernel, tq=tq, n_sel=n_sel, look=look)
    return pl.pallas_call(
        kern,
        grid=(b, s // tq),
        in_specs=[
            pl.BlockSpec((None, NSA_Q, tq), lambda bi, j: (bi, ZT_Q // NSA_Q, j)),
            pl.BlockSpec((None, s, NSA_KV), lambda bi, j: (bi, 0, ZB_KSLC // NSA_KV)),
            pl.BlockSpec((None, s, NSA_KV), lambda bi, j: (bi, 0, ZB_KWIN // NSA_KV)),
            pl.BlockSpec((None, NSA_KV, s), lambda bi, j: (bi, ZT_VSLC // NSA_KV, 0)),
            pl.BlockSpec((None, NSA_KV, s), lambda bi, j: (bi, ZT_VWIN // NSA_KV, 0)),
            pl.BlockSpec((None, NSA_GROUPS * n_sel, tq), lambda bi, j: (bi, 0, j)),
            pl.BlockSpec((None, NSA_Q, tq), lambda bi, j: (bi, 0, j)),
            pl.BlockSpec((None, ZG_WIDTH, tq), lambda bi, j: (bi, 0, j)),
        ],
        out_specs=pl.BlockSpec((None, tq, NSA_Q), lambda bi, j: (bi, j, 0)),
        out_shape=jax.ShapeDtypeStruct((b, s, NSA_Q), BF16),
        scratch_shapes=[
            pltpu.VMEM((NSA_Q, tq), F32),
            pltpu.VMEM((NSA_HEADS, 2 * NSA_DH, tq), BF16),
            pltpu.VMEM((2, NSA_HEADS, 1, tq), F32),
            pltpu.VMEM((2, NSA_HEADS, NSA_DH + den_rows, tq), F32),
            pltpu.VMEM((max_tiles * NSA_HEADS, tk, tq), F32),
            pltpu.VMEM((max_tiles * NSA_HEADS, 1, tq), F32),
            pltpu.VMEM((max_tiles, tk, tq), F32),
        ],
        compiler_params=_cparams("parallel", "arbitrary"),
        name="nsa",
    )(zt, zb3, zb3, zt, zt, selb, ocmpt, zgt)


def _mlstm_kernel(q_ref, k_ref, v_ref, o_ref, zg_ref, cw_ref, cb_ref, gb_ref, hg_ref, y_ref,
                  xbuf, c_st, n_st, m_st, *, nb):
    L = ML_CHUNK
    hist = xbuf.shape[1]
    @pl.when(pl.program_id(1) == 0)
    def _():
        xbuf[...] = jnp.zeros_like(xbuf)
        c_st[...] = jnp.zeros_like(c_st)
        n_st[...] = jnp.zeros_like(n_st)
        m_st[...] = jnp.zeros_like(m_st)

    row = lax.broadcasted_iota(jnp.int32, (L, L), 0)
    col = lax.broadcasted_iota(jnp.int32, (L, L), 1)
    tril = row >= col
    tril_f = jnp.where(tril, 1.0, 0.0)
    streams = [(bb, h) for bb in range(nb) for h in range(ML_HEADS)]

    sr = lax.broadcasted_iota(jnp.int32, (CONV_WIDTH * L, hist + L), 0)
    scol = lax.broadcasted_iota(jnp.int32, (CONV_WIDTH * L, hist + L), 1)
    shift = jnp.where(scol == hist - (CONV_WIDTH - 1) + sr % L + sr // L, 1.0, 0.0).astype(BF16)
    qks = []
    for bb in range(nb):
        halves = []
        for hf, x_ref in enumerate((q_ref, k_ref)):
            cur = x_ref[bb]
            ext = jnp.concatenate([xbuf[bb, :, hf * ML_W:(hf + 1) * ML_W], cur], axis=0)
            taps = jnp.dot(shift, ext, preferred_element_type=F32)
            conv = jnp.zeros((L, ML_W), F32) + cb_ref[:, hf * ML_W:(hf + 1) * ML_W]
            for j in range(CONV_WIDTH):
                conv = conv + taps[j * L:(j + 1) * L, :] * cw_ref[j:j + 1, hf * ML_W:(hf + 1) * ML_W]
            xbuf[bb, :, hf * ML_W:(hf + 1) * ML_W] = cur[L - hist:L, :]
            halves.append(_silu(conv))
        qks.append(jnp.concatenate(halves, axis=1))

    qs = {(bb, h): qks[bb][:, h * ML_DH:(h + 1) * ML_DH] for bb, h in streams}
    ks = {(bb, h): qks[bb][:, ML_W + h * ML_DH:ML_W + (h + 1) * ML_DH] * (ML_DH ** -0.5)
          for bb, h in streams}
    qbs = {s: qs[s].astype(BF16) for s in streams}
    vbs = {(bb, h): v_ref[bb, :, h * ML_DH:(h + 1) * ML_DH] for bb, h in streams}
    c_olds = {(bb, h): c_st[bb * ML_HEADS + h] for bb, h in streams}
    n_olds = {(bb, h): n_st[bb, h:h + 1, :] for bb, h in streams}
    s_qk = {s: lax.dot_general(qbs[s], ks[s].astype(BF16), _NT, preferred_element_type=F32)
            for s in streams}
    cqs = {s: lax.dot_general(qbs[s], c_olds[s].astype(BF16), _NT, preferred_element_type=F32)
           for s in streams}

    g = []
    for bb in range(nb):
        gates = zg_ref[bb] + gb_ref[...]
        bcum = jnp.dot(tril_f, _log_sigmoid(gates), preferred_element_type=F32,
                       precision=lax.Precision.HIGHEST)
        b_al = pltpu.roll(bcum, shift=ZG_WIDTH - ML_HEADS, axis=1)
        m_prev = m_st[bb, 0:1, :]
        b_end = b_al[L - 1:L, :]
        inter = b_al + m_prev
        wlog = b_end - b_al + gates
        m_new = jnp.maximum(b_end + m_prev, jnp.max(wlog, axis=0, keepdims=True))
        ws = jnp.exp(wlog - m_new)
        decay = jnp.exp(b_end + m_prev - m_new)
        m_st[bb, 0:1, :] = m_new
        r_t = (gates - b_al).T
        g.append((b_al, inter, ws, decay, r_t))

    for bb, h in streams:
        c = ZG_IF + h
        _, _, ws, decay, _ = g[bb]
        kw = ks[bb, h] * ws[:, c:c + 1]
        upd = lax.dot_general(vbs[bb, h], kw.astype(BF16), _TN, preferred_element_type=F32)
        c_st[bb * ML_HEADS + h] = decay[:, c:c + 1] * c_olds[bb, h] + upd
        n_st[bb, h:h + 1, :] = decay[:, c:c + 1] * n_olds[bb, h] + jnp.sum(kw, axis=0, keepdims=True)

    for bb, h in streams:
        c = ZG_IF + h
        b_al, inter, _, _, r_t = g[bb]
        dlog = jnp.where(tril, b_al[:, c:c + 1] + r_t[c:c + 1, :], -jnp.inf)
        inter_c = inter[:, c:c + 1]
        m_t = jnp.maximum(inter_c, jnp.max(dlog, axis=-1, keepdims=True))
        dw = jnp.exp(dlog - m_t)
        iw = jnp.exp(inter_c - m_t)
        sqk = s_qk[bb, h] * dw
        num = jnp.dot(sqk.astype(BF16), vbs[bb, h], preferred_element_type=F32) + iw * cqs[bb, h]
        den = (jnp.sum(sqk, axis=-1, keepdims=True)
               + iw * jnp.sum(qs[bb, h] * n_olds[bb, h], axis=-1, keepdims=True))
        hs = num / jnp.maximum(jnp.abs(den), jnp.exp(-m_t))
        hn = hs * lax.rsqrt(jnp.mean(hs * hs, axis=-1, keepdims=True) + EPS)
        hn = hn * hg_ref[:, h * ML_DH:(h + 1) * ML_DH]
        og = _sigmoid(o_ref[bb, :, h * ML_DH:(h + 1) * ML_DH].astype(F32))
        y_ref[bb, :, h * ML_DH:(h + 1) * ML_DH] = (og * hn).astype(BF16)


def _mlstm(zb3, zg3, conv_w, conv_b, gate_b, head_g, nb=2):
    b, s, _ = zb3.shape
    L = ML_CHUNK
    qb = ZB_QKVML // ML_W
    assert b % nb == 0

    def zspec(k):
        return pl.BlockSpec((nb, L, ML_W), lambda bi, c, k=k: (bi, c, k))

    return pl.pallas_call(
        functools.partial(_mlstm_kernel, nb=nb),
        grid=(b // nb, s // L),
        in_specs=[
            zspec(qb), zspec(qb + 1), zspec(qb + 2), zspec(ZB_OML // ML_W),
            pl.BlockSpec((nb, L, ZG_WIDTH), lambda bi, c: (bi, c, 0)),
            pl.BlockSpec((CONV_WIDTH, 2 * ML_W), lambda bi, c: (0, 0)),
            pl.BlockSpec((1, 2 * ML_W), lambda bi, c: (0, 0)),
            pl.BlockSpec((1, ZG_WIDTH), lambda bi, c: (0, 0)),
            pl.BlockSpec((1, ML_W), lambda bi, c: (0, 0)),
        ],
        out_specs=pl.BlockSpec((nb, L, ML_W), lambda bi, c: (bi, c, 0)),
        out_shape=jax.ShapeDtypeStruct((b, s, ML_W), BF16),
        scratch_shapes=[
            pltpu.VMEM((nb, 16, 2 * ML_W), BF16),
            pltpu.VMEM((nb * ML_HEADS, ML_DH, ML_DH), F32),
            pltpu.VMEM((nb, 8, ML_DH), F32),
            pltpu.VMEM((nb, 8, LANE), F32),
        ],
        compiler_params=_cparams("parallel", "arbitrary"),
        name="mlstm",
    )(zb3, zb3, zb3, zb3, zg3, conv_w, conv_b, gate_b, head_g)


def _memkv_kernel(mem_ref, g_ref, w_ref, kv_ref):
    hb = _rms(mem_ref[...], g_ref[...]).astype(BF16)
    kv_ref[...] = jnp.dot(hb, w_ref[...], preferred_element_type=F32).astype(BF16)


def _memkv(mem, g, wb):
    b, m, _ = mem.shape
    return pl.pallas_call(
        _memkv_kernel,
        grid=(b,),
        in_specs=[
            pl.BlockSpec((None, m, D_MODEL), lambda i: (i, 0, 0)),
            pl.BlockSpec((1, D_MODEL), lambda i: (0, 0)),
            pl.BlockSpec((D_MODEL, 2 * MEM_W), lambda i: (0, 0)),
        ],
        out_specs=pl.BlockSpec((None, m, 2 * MEM_W), lambda i: (i, 0, 0)),
        out_shape=jax.ShapeDtypeStruct((b, m, 2 * MEM_W), BF16),
        compiler_params=_cparams("parallel"),
        name="memkv",
    )(mem, g, wb)


def _memattn_kernel(q_ref, kv_ref, y_ref):
    for h in range(MEM_HEADS):
        q = q_ref[:, h * MEM_DH:(h + 1) * MEM_DH]
        k = kv_ref[:, h * MEM_DH:(h + 1) * MEM_DH]
        v = kv_ref[:, MEM_W + h * MEM_DH:MEM_W + (h + 1) * MEM_DH]
        s = lax.dot_general(q, k, _NT, preferred_element_type=F32) * MEM_SCALE
        e = jnp.exp(s - jnp.max(s, axis=-1, keepdims=True))
        p = e / jnp.sum(e, axis=-1, keepdims=True)
        o = jnp.dot(p.astype(BF16), v, preferred_element_type=F32)
        y_ref[:, h * MEM_DH:(h + 1) * MEM_DH] = o.astype(BF16)


def _memattn(zb3, kvm, tq=512):
    b, s, _ = zb3.shape
    m = kvm.shape[1]
    return pl.pallas_call(
        _memattn_kernel,
        grid=(b, s // tq),
        in_specs=[
            pl.BlockSpec((None, tq, MEM_W), lambda i, j: (i, j, ZB_QMEM // MEM_W)),
            pl.BlockSpec((None, m, 2 * MEM_W), lambda i, j: (i, 0, 0)),
        ],
        out_specs=pl.BlockSpec((None, tq, MEM_W), lambda i, j: (i, j, 0)),
        out_shape=jax.ShapeDtypeStruct((b, s, MEM_W), BF16),
        compiler_params=_cparams("parallel", "parallel"),
        name="memattn",
    )(zb3, kvm)


def _merge_kernel(x_ref, yn_ref, yl_ref, ym_ref, g0_ref, g1_ref, g2_ref,
                  wn_ref, wl_ref, wm_ref, wo_ref, gp_ref, out_ref):
    y = _sigmoid(g0_ref[...].astype(F32)) * jnp.dot(yn_ref[...], wn_ref[...], preferred_element_type=F32)
    y = y + _sigmoid(g1_ref[...].astype(F32)) * jnp.dot(yl_ref[...], wl_ref[...], preferred_element_type=F32)
    y = y + _sigmoid(g2_ref[...].astype(F32)) * jnp.dot(ym_ref[...], wm_ref[...], preferred_element_type=F32)
    u = jnp.dot(y.astype(BF16), wo_ref[...], preferred_element_type=F32)
    out_ref[...] = x_ref[...] + _rms(u, gp_ref[...])


def _merge(x2, yn, yl, ym, zb, wn, wl, wm, wo, gp, tm=512):
    t = x2.shape[0]
    gm = ZB_GMERGE // D_MODEL

    def const(shape):
        return pl.BlockSpec(shape, lambda i: (0, 0), pipeline_mode=pl.Buffered(1))

    return pl.pallas_call(
        _merge_kernel,
        grid=(t // tm,),
        in_specs=[
            pl.BlockSpec((tm, D_MODEL), lambda i: (i, 0)),
            pl.BlockSpec((tm, NSA_Q), lambda i: (i, 0)),
            pl.BlockSpec((tm, ML_W), lambda i: (i, 0)),
            pl.BlockSpec((tm, MEM_W), lambda i: (i, 0)),
            pl.BlockSpec((tm, D_MODEL), lambda i: (i, gm)),
            pl.BlockSpec((tm, D_MODEL), lambda i: (i, gm + 1)),
            pl.BlockSpec((tm, D_MODEL), lambda i: (i, gm + 2)),
            const((NSA_Q, D_MODEL)), const((ML_W, D_MODEL)), const((MEM_W, D_MODEL)),
            const((D_MODEL, D_MODEL)), const((1, D_MODEL)),
        ],
        out_specs=pl.BlockSpec((tm, D_MODEL), lambda i: (i, 0)),
        out_shape=jax.ShapeDtypeStruct((t, D_MODEL), F32),
        compiler_params=_cparams("parallel"),
        name="merge",
    )(x2, yn, yl, ym, zb, zb, zb, wn, wl, wm, wo, gp)


def _ffn_kernel(x_ref, gpre_ref, wg_ref, wu_ref, wd_ref, gpost_ref, out_ref, h_ref, acc_ref):
    j = pl.program_id(1)

    @pl.when(j == 0)
    def _():
        h_ref[...] = _rms(x_ref[...], gpre_ref[...]).astype(BF16)
        acc_ref[...] = jnp.zeros_like(acc_ref)

    h = h_ref[...]
    gate = jnp.dot(h, wg_ref[...], preferred_element_type=F32)
    up = jnp.dot(h, wu_ref[...], preferred_element_type=F32)
    act = (_silu(gate) * up).astype(BF16)
    acc_ref[...] += jnp.dot(act, wd_ref[...], preferred_element_type=F32)

    @pl.when(j == pl.num_programs(1) - 1)
    def _():
        out_ref[...] = x_ref[...] + _rms(acc_ref[...], gpost_ref[...])


def _ffn(x2, gpre, w_in, w_down, gpost, tm=512, nf=1):
    t = x2.shape[0]
    tf = D_FF // nf
    assert tf % LANE == 0
    wmode = dict(pipeline_mode=pl.Buffered(1)) if nf == 1 else {}
    return pl.pallas_call(
        _ffn_kernel,
        grid=(t // tm, nf),
        in_specs=[
            pl.BlockSpec((tm, D_MODEL), lambda i, j: (i, 0)),
            pl.BlockSpec((1, D_MODEL), lambda i, j: (0, 0)),
            pl.BlockSpec((D_MODEL, tf), lambda i, j: (0, j), **wmode),
            pl.BlockSpec((D_MODEL, tf), lambda i, j: (0, nf + j), **wmode),
            pl.BlockSpec((tf, D_MODEL), lambda i, j: (j, 0), **wmode),
            pl.BlockSpec((1, D_MODEL), lambda i, j: (0, 0)),
        ],
        out_specs=pl.BlockSpec((tm, D_MODEL), lambda i, j: (i, 0)),
        out_shape=jax.ShapeDtypeStruct((t, D_MODEL), F32),
        scratch_shapes=[pltpu.VMEM((tm, D_MODEL), BF16), pltpu.VMEM((tm, D_MODEL), F32)],
        compiler_params=_cparams("parallel", "arbitrary"),
        name="ffn",
    )(x2, gpre, w_in, w_in, w_down, gpost)


def _layer(x, mem, p):
    b, s, _ = x.shape
    t = b * s
    x2 = x.reshape(t, D_MODEL)

    wn, wt, wg, wc = _wprep(p["w_in"].T)
    zt, zb, zg, zgt, kvc = _inproj(x2, p["g_pre_mix"].reshape(1, D_MODEL), wt, wn, wg, wc, b, s)
    zb3 = zb.reshape(b, s, ZB_WIDTH)
    zg3 = zg.reshape(b, s, ZG_WIDTH)

    half = (CMP_BLOCK // 2) * NSA_DH
    pe = jnp.stack([p["cmp_pe_k"], p["cmp_pe_v"]]).reshape(2, 1, 2 * half)
    pe = jnp.pad(pe, ((0, 0), (0, 7), (0, 0))).astype(BF16)
    w1 = jnp.stack([p["cmp_w1_k"], p["cmp_w1_v"]])
    eye2 = jnp.eye(2, dtype=F32)
    wpa = jnp.einsum("ab,kldh->kladbh", eye2, w1[:, :CMP_STRIDE]).reshape(2, CMP_STRIDE, LANE, 2 * CMP_HIDDEN)
    wpb = jnp.einsum("ab,kldh->kladbh", eye2, w1[:, CMP_STRIDE:]).reshape(2, CMP_STRIDE, LANE, 2 * CMP_HIDDEN)
    wp = jnp.concatenate([wpa, wpb], axis=-1).astype(BF16)
    kc, vct = _compress(kvc.reshape(b, s, KVC_WIDTH), pe, w1.reshape(2, 2 * half, CMP_HIDDEN).astype(BF16), wp,
                        p["cmp_w2_k"].astype(BF16), p["cmp_w2_v"].T.astype(BF16))
    ocmpt, selb = _cmpsel(zt, kc, vct)
    y_nsa = _nsa(zt, zb3, selb, ocmpt, zgt)

    gate_b = jnp.zeros((1, ZG_WIDTH), F32).at[0, ZG_IF:ZG_IF + 2 * ML_HEADS].set(p["ml_gate_b"])
    y_ml = _mlstm(zb3, zg3, p["ml_conv_w"], p["ml_conv_b"].reshape(1, 2 * ML_W), gate_b,
                  p["ml_head_g"].reshape(1, ML_W))

    kvm = _memkv(mem, p["g_mem"].reshape(1, D_MODEL), p["w_mem_kv"].astype(BF16))
    y_mem = _memattn(zb3, kvm)

    x1 = _merge(x2, y_nsa.reshape(t, NSA_Q), y_ml.reshape(t, ML_W), y_mem.reshape(t, MEM_W), zb,
                p["w_proj_nsa"].astype(BF16), p["w_proj_ml"].astype(BF16),
                p["w_proj_mem"].astype(BF16), p["w_out"].astype(BF16),
                p["g_post_mix"].reshape(1, D_MODEL))
    x2o = _ffn(x1, p["g_pre_ffn"].reshape(1, D_MODEL), p["w_ffn_in"].astype(BF16),
               p["w_ffn_down"].astype(BF16), p["g_post_ffn"].reshape(1, D_MODEL))
    return x2o.reshape(b, s, D_MODEL)


def kernel(x, mem, g_pre_mix, w_in, cmp_pe_k, cmp_w1_k, cmp_w2_k, cmp_pe_v, cmp_w1_v, cmp_w2_v,
           ml_conv_w, ml_conv_b, ml_gate_b, ml_head_g, g_mem, w_mem_kv, w_proj_nsa, w_proj_ml,
           w_proj_mem, w_out, g_post_mix, g_pre_ffn, w_ffn_in, w_ffn_down, g_post_ffn):
    params = dict(
        g_pre_mix=g_pre_mix, w_in=w_in, cmp_pe_k=cmp_pe_k, cmp_w1_k=cmp_w1_k, cmp_w2_k=cmp_w2_k,
        cmp_pe_v=cmp_pe_v, cmp_w1_v=cmp_w1_v, cmp_w2_v=cmp_w2_v, ml_conv_w=ml_conv_w,
        ml_conv_b=ml_conv_b, ml_gate_b=ml_gate_b, ml_head_g=ml_head_g, g_mem=g_mem,
        w_mem_kv=w_mem_kv, w_proj_nsa=w_proj_nsa, w_proj_ml=w_proj_ml, w_proj_mem=w_proj_mem,
        w_out=w_out, g_post_mix=g_post_mix, g_pre_ffn=g_pre_ffn, w_ffn_in=w_ffn_in,
        w_ffn_down=w_ffn_down, g_post_ffn=g_post_ffn)
    depth = w_in.shape[0]
    for l in range(depth):
        x = _layer(x, mem, {k: v[l] for k, v in params.items()})
    return x
```

```python
import functools
import math

import jax
import jax.numpy as jnp
from jax import lax
from jax.experimental import pallas as pl
from jax.experimental.pallas import tpu as pltpu

F32 = jnp.float32
BF16 = jnp.bfloat16

D_MODEL = 1024
EPS = 1e-6
NEG_INF = -1e30
BIG = 1e30

NSA_HEADS = 16
NSA_GROUPS = 4
NSA_HPG = NSA_HEADS // NSA_GROUPS
NSA_DH = 64
NSA_SCALE = NSA_DH ** -0.5
CMP_BLOCK = 32
CMP_STRIDE = 16
CMP_HIDDEN = 128
SEL_BLOCK = 64
SEL_TOPK = 16
N_LOCAL_FORCED = 2
WINDOW = 512

ML_HEADS = 4
ML_DH = 128
ML_CHUNK = 128
CONV_WIDTH = 4

MEM_HEADS = 4
MEM_DH = 128
MEM_SCALE = MEM_DH ** -0.5

D_FF = -(-8 * D_MODEL // (3 * 256)) * 256

NSA_Q = NSA_HEADS * NSA_DH
NSA_KV = NSA_GROUPS * NSA_DH
ML_W = ML_HEADS * ML_DH
MEM_W = MEM_HEADS * MEM_DH

_OFF_Q = 0
_OFF_KV = _OFF_Q + NSA_Q
_OFF_GNSA = _OFF_KV + 6 * NSA_KV
_OFF_QKVML = _OFF_GNSA + 3 * NSA_HEADS
_OFF_IF = _OFF_QKVML + 3 * ML_W
_OFF_OML = _OFF_IF + 2 * ML_HEADS
_OFF_QMEM = _OFF_OML + ML_W
_OFF_GMERGE = _OFF_QMEM + MEM_W
_IN_WIDTH = _OFF_GMERGE + 3 * D_MODEL

ZB_GMERGE = 0
ZB_KSLC = ZB_GMERGE + 3 * D_MODEL
ZB_KWIN = ZB_KSLC + NSA_KV
ZB_QKVML = ZB_KWIN + NSA_KV
ZB_OML = ZB_QKVML + 3 * ML_W
ZB_QMEM = ZB_OML + ML_W
ZB_WIDTH = ZB_QMEM + MEM_W
KVC_WIDTH = 2 * NSA_KV
ZT_Q = 0
ZT_VSLC = ZT_Q + NSA_Q
ZT_VWIN = ZT_VSLC + NSA_KV
ZT_ROWS = ZT_VWIN + NSA_KV
ZG_WIDTH = 128
ZG_GNSA = 0
ZG_IF = 3 * NSA_HEADS

LANE = 128
SLAB_ROWS = 8
VMEM_LIMIT = 56 * 1024 * 1024
LOG2E = math.log2(math.e)

_NT = (((1,), (1,)), ((), ()))
_TN = (((0,), (0,)), ((), ()))


def _cparams(*sem):
    return pltpu.CompilerParams(dimension_semantics=sem, vmem_limit_bytes=VMEM_LIMIT)


def _rms(x, g):
    return x * lax.rsqrt(jnp.mean(x * x, axis=-1, keepdims=True) + EPS) * g


def _sigmoid(x):
    return 1.0 / (1.0 + jnp.exp(-x))


def _silu(x):
    return x * _sigmoid(x)


def _log_sigmoid(x):
    return jnp.minimum(x, 0.0) - jnp.log(1.0 + jnp.exp(-jnp.abs(x)))


def _wprep_kernel(w_ref, wn_ref, wt_ref, wc_ref):
    def rows(lo, hi):
        return w_ref[lo:hi, :]

    def kv_rows(k):
        return rows(_OFF_KV + k * NSA_KV, _OFF_KV + (k + 1) * NSA_KV)

    wn_ref[ZB_GMERGE:ZB_KSLC, :] = rows(_OFF_GMERGE, _IN_WIDTH).astype(BF16)
    wn_ref[ZB_KSLC:ZB_KWIN, :] = kv_rows(2).astype(BF16)
    wn_ref[ZB_KWIN:ZB_QKVML, :] = kv_rows(4).astype(BF16)
    wn_ref[ZB_QKVML:ZB_OML, :] = rows(_OFF_QKVML, _OFF_IF).astype(BF16)
    wn_ref[ZB_OML:ZB_WIDTH, :] = rows(_OFF_OML, _OFF_GMERGE).astype(BF16)
    wt_ref[ZT_Q:ZT_VSLC, :] = (rows(_OFF_Q, _OFF_KV) * (NSA_SCALE * LOG2E)).astype(BF16)
    wt_ref[ZT_VSLC:ZT_VWIN, :] = kv_rows(3).astype(BF16)
    wt_ref[ZT_VWIN:ZT_ROWS, :] = kv_rows(5).astype(BF16)
    n_g = 3 * NSA_HEADS
    n_if = 2 * ML_HEADS
    wt_ref[ZT_ROWS:ZT_ROWS + ZG_WIDTH, :] = jnp.zeros((ZG_WIDTH, wt_ref.shape[1]), BF16)
    wt_ref[ZT_ROWS + ZG_GNSA:ZT_ROWS + ZG_GNSA + n_g, :] = rows(_OFF_GNSA, _OFF_QKVML).astype(BF16)
    wt_ref[ZT_ROWS + ZG_IF:ZT_ROWS + ZG_IF + n_if, :] = rows(_OFF_IF, _OFF_OML).astype(BF16)
    wc_ref[...] = rows(_OFF_KV, _OFF_KV + KVC_WIDTH).astype(BF16)


def _wprep(w_t, tl=128):
    assert ZB_QMEM == ZB_OML + ML_W and _OFF_QMEM == _OFF_OML + ML_W
    return pl.pallas_call(
        _wprep_kernel,
        grid=(D_MODEL // tl,),
        in_specs=[pl.BlockSpec((_IN_WIDTH, tl), lambda i: (0, i))],
        out_specs=[
            pl.BlockSpec((ZB_WIDTH, tl), lambda i: (0, i)),
            pl.BlockSpec((ZT_ROWS + ZG_WIDTH, tl), lambda i: (0, i)),
            pl.BlockSpec((KVC_WIDTH, tl), lambda i: (0, i)),
        ],
        out_shape=[
            jax.ShapeDtypeStruct((ZB_WIDTH, D_MODEL), BF16),
            jax.ShapeDtypeStruct((ZT_ROWS + ZG_WIDTH, D_MODEL), BF16),
            jax.ShapeDtypeStruct((KVC_WIDTH, D_MODEL), BF16),
        ],
        compiler_params=_cparams("parallel"),
        name="wprep",
    )(w_t)


def _inproj_kernel(x_ref, g_ref, wt_ref, wn_ref, wc_ref,
                   zt_ref, zb_ref, zg_ref, zgt_ref, kvc_ref, h_ref, *, nt):
    j = pl.program_id(1)

    @pl.when(j == 0)
    def _():
        hb = _rms(x_ref[...], g_ref[...]).astype(BF16)
        h_ref[...] = hb
        kvc_ref[...] = lax.dot_general(hb, wc_ref[...], _NT, preferred_element_type=F32)
        zf = lax.dot_general(wt_ref[...], hb, _NT, preferred_element_type=F32)
        zt_ref[...] = zf[0:ZT_ROWS, :].astype(BF16)
        gt = zf[ZT_ROWS:ZT_ROWS + ZG_WIDTH, :]
        zgt_ref[...] = gt
        zg_ref[...] = gt.T

    @pl.when(j >= nt)
    def _():
        tn = zb_ref.shape[1]
        r0 = pl.multiple_of((j - nt) * tn, tn)
        zb_ref[...] = lax.dot_general(h_ref[...], wn_ref[pl.ds(r0, tn), :], _NT,
                                      preferred_element_type=F32).astype(BF16)


def _inproj(x2, g, wt, wn, wc, b, s, tm=1024, nn=3):
    t = x2.shape[0]
    nt = 1
    tn = ZB_WIDTH // nn
    assert tn * nn == ZB_WIDTH and tn % LANE == 0 and s % tm == 0
    spb = s // tm
    kern = functools.partial(_inproj_kernel, nt=nt)
    once = dict(pipeline_mode=pl.Buffered(1))
    return pl.pallas_call(
        kern,
        grid=(t // tm, nt + nn),
        in_specs=[
            pl.BlockSpec((tm, D_MODEL), lambda i, j: (i, 0)),
            pl.BlockSpec((1, D_MODEL), lambda i, j: (0, 0)),
            pl.BlockSpec((ZT_ROWS + ZG_WIDTH, D_MODEL), lambda i, j: (0, 0), **once),
            pl.BlockSpec((ZB_WIDTH, D_MODEL), lambda i, j: (0, 0), **once),
            pl.BlockSpec((KVC_WIDTH, D_MODEL), lambda i, j: (0, 0), **once),
        ],
        out_specs=[
            pl.BlockSpec((None, ZT_ROWS, tm), lambda i, j: (i // spb, 0, i % spb)),
            pl.BlockSpec((tm, tn), lambda i, j: (i, jnp.maximum(j - nt, 0))),
            pl.BlockSpec((tm, ZG_WIDTH), lambda i, j: (i, 0)),
            pl.BlockSpec((None, ZG_WIDTH, tm), lambda i, j: (i // spb, 0, i % spb)),
            pl.BlockSpec((tm, KVC_WIDTH), lambda i, j: (i, 0)),
        ],
        out_shape=[
            jax.ShapeDtypeStruct((b, ZT_ROWS, s), BF16),
            jax.ShapeDtypeStruct((t, ZB_WIDTH), BF16),
            jax.ShapeDtypeStruct((t, ZG_WIDTH), F32),
            jax.ShapeDtypeStruct((b, ZG_WIDTH, s), F32),
            jax.ShapeDtypeStruct((t, KVC_WIDTH), F32),
        ],
        scratch_shapes=[pltpu.VMEM((tm, D_MODEL), BF16)],
        compiler_params=_cparams("parallel", "arbitrary"),
        name="inproj",
    )(x2, g, wt, wn, wc)


def _compress_kernel(a0_ref, a1_ref, a2_ref, a3_ref, pe_ref, w1_ref, wp_ref, w2k_ref, w2vt_ref,
                     kc_ref, vct_ref):
    a_refs = (a0_ref, a1_ref, a2_ref, a3_ref)
    n_rows = a0_ref.shape[0] // CMP_STRIDE
    for kind in range(2):
        pe = pe_ref[kind]
        c = jnp.dot(pe, w1_ref[kind], preferred_element_type=F32)[0:1, :]
        for pair in range(NSA_GROUPS // 2):
            a_ref = a_refs[kind * 2 + pair]
            acc = jnp.zeros((n_rows, 4 * CMP_HIDDEN), F32)
            for l in range(CMP_STRIDE):
                rows = a_ref[pl.ds(l, n_rows, stride=CMP_STRIDE), :].astype(BF16)
                acc = acc + jnp.dot(rows, wp_ref[kind, l], preferred_element_type=F32)
            for gi in range(2):
                g = 2 * pair + gi
                p = acc[:, gi * CMP_HIDDEN:(gi + 1) * CMP_HIDDEN]
                q = acc[:, (2 + gi) * CMP_HIDDEN:(3 + gi) * CMP_HIDDEN]
                qs = pltpu.roll(q, shift=n_rows - 1, axis=0)
                hid = _silu(p + qs + c).astype(BF16)
                if kind == 0:
                    o = jnp.dot(hid, w2k_ref[...], preferred_element_type=F32)
                    kc_ref[:, g * NSA_DH:(g + 1) * NSA_DH] = o.astype(BF16)
                else:
                    ot = lax.dot_general(w2vt_ref[...], hid, _NT, preferred_element_type=F32)
                    vct_ref[g * NSA_DH:(g + 1) * NSA_DH, :] = ot.astype(BF16)


def _compress(kvc3, pe, w1, wp, w2k, w2vt):
    b, s, _ = kvc3.shape
    n_rows = s // CMP_STRIDE
    half = (CMP_BLOCK // 2) * NSA_DH

    def col(k):
        return pl.BlockSpec((None, s, LANE), lambda i, k=k: (i, 0, k))

    return pl.pallas_call(
        _compress_kernel,
        grid=(b,),
        in_specs=[
            col(0), col(1), col(2), col(3),
            pl.BlockSpec((2, 8, 2 * half), lambda i: (0, 0, 0)),
            pl.BlockSpec((2, 2 * half, CMP_HIDDEN), lambda i: (0, 0, 0)),
            pl.BlockSpec((2, CMP_STRIDE, LANE, 4 * CMP_HIDDEN), lambda i: (0, 0, 0, 0)),
            pl.BlockSpec((CMP_HIDDEN, NSA_DH), lambda i: (0, 0)),
            pl.BlockSpec((NSA_DH, CMP_HIDDEN), lambda i: (0, 0)),
        ],
        out_specs=[
            pl.BlockSpec((None, n_rows, NSA_KV), lambda i: (i, 0, 0)),
            pl.BlockSpec((None, NSA_KV, n_rows), lambda i: (i, 0, 0)),
        ],
        out_shape=[
            jax.ShapeDtypeStruct((b, n_rows, NSA_KV), BF16),
            jax.ShapeDtypeStruct((b, NSA_KV, n_rows), BF16),
        ],
        compiler_params=_cparams("parallel"),
        name="compress",
    )(kvc3, kvc3, kvc3, kvc3, pe, w1, wp, w2k, w2vt)


def _pad_query(qt, g):
    z = jnp.zeros_like(qt)
    return jnp.concatenate([qt, z] if g % 2 == 0 else [z, qt], axis=0)


def _cmpsel_kernel(qt_ref, kc_ref, vct_ref, ocmpt_ref, selb_ref, *, tq, n_sel, topk):
    nc = kc_ref.shape[0]
    t0 = pl.program_id(1) * tq

    n_t = lax.broadcasted_iota(jnp.int32, (nc, tq), 0)
    t_t = t0 + lax.broadcasted_iota(jnp.int32, (nc, tq), 1)
    mask_t = (n_t * CMP_STRIDE + (CMP_BLOCK - 1)) <= t_t

    j_m = lax.broadcasted_iota(jnp.int32, (n_sel, nc), 0) * SEL_BLOCK
    c_m = lax.broadcasted_iota(jnp.int32, (n_sel, nc), 1) * CMP_STRIDE
    ov = jnp.minimum(c_m + CMP_BLOCK, j_m + SEL_BLOCK) - jnp.maximum(c_m, j_m)
    map_t = jnp.maximum(ov, 0).astype(F32) * (1.0 / CMP_BLOCK)

    j_s = lax.broadcasted_iota(jnp.int32, (n_sel, tq), 0)
    qblk = (t0 + lax.broadcasted_iota(jnp.int32, (n_sel, tq), 1)) // SEL_BLOCK
    rel = qblk - j_s
    causal = rel >= 0
    forced = causal & ((j_s == 0) | (rel < N_LOCAL_FORCED))

    for g in range(NSA_GROUPS):
        pair = g // 2
        kcp = kc_ref[:, pair * LANE:(pair + 1) * LANE]
        vct = vct_ref[g * NSA_DH:(g + 1) * NSA_DH, :]
        psum_t = jnp.zeros((nc, tq), F32)
        sts = [jnp.dot(kcp, _pad_query(qt_ref[(g * NSA_HPG + h) * NSA_DH:(g * NSA_HPG + h + 1) * NSA_DH, :], g),
                       preferred_element_type=F32) for h in range(NSA_HPG)]
        for h in range(NSA_HPG):
            hh = g * NSA_HPG + h
            st = jnp.where(mask_t, sts[h], NEG_INF)
            et = jnp.where(mask_t, jnp.exp2(st - jnp.max(st, axis=0, keepdims=True)), 0.0)
            lt = jnp.sum(et, axis=0, keepdims=True)
            pt = et * (1.0 / jnp.where(lt > 0.0, lt, 1.0))
            psum_t = psum_t + pt
            ot = jnp.dot(vct, pt.astype(BF16), preferred_element_type=F32)
            ocmpt_ref[hh * NSA_DH:(hh + 1) * NSA_DH, :] = ot.astype(BF16)
        imp_t = jnp.dot(map_t, psum_t, preferred_element_type=F32,
                        precision=lax.Precision.HIGHEST)
        score = jnp.where(forced, BIG, jnp.where(causal, imp_t, NEG_INF))
        rank = jnp.zeros((n_sel, tq), jnp.int32)
        for jp in range(n_sel):
            row = score[jp:jp + 1, :]
            before = (row > score) | ((row == score) & (j_s > jp))
            rank = rank + before.astype(jnp.int32)
        chosen = (rank < topk) & (score > 0.5 * NEG_INF)
        selb_ref[g * n_sel:(g + 1) * n_sel, :] = jnp.where(chosen, 0.0, NEG_INF)


def _cmpsel(zt, kc, vct, tq=256):
    b, _, s = zt.shape
    n_sel = s // SEL_BLOCK
    topk = min(SEL_TOPK, n_sel)
    nc = kc.shape[1]
    kern = functools.partial(_cmpsel_kernel, tq=tq, n_sel=n_sel, topk=topk)
    return pl.pallas_call(
        kern,
        grid=(b, s // tq),
        in_specs=[
            pl.BlockSpec((None, NSA_Q, tq), lambda i, j: (i, ZT_Q // NSA_Q, j)),
            pl.BlockSpec((None, nc, NSA_KV), lambda i, j: (i, 0, 0)),
            pl.BlockSpec((None, NSA_KV, nc), lambda i, j: (i, 0, 0)),
        ],
        out_specs=[
            pl.BlockSpec((None, NSA_Q, tq), lambda i, j: (i, 0, j)),
            pl.BlockSpec((None, NSA_GROUPS * n_sel, tq), lambda i, j: (i, 0, j)),
        ],
        out_shape=[
            jax.ShapeDtypeStruct((b, NSA_Q, s), BF16),
            jax.ShapeDtypeStruct((b, NSA_GROUPS * n_sel, s), F32),
        ],
        compiler_params=_cparams("parallel", "parallel"),
        name="cmpsel",
    )(zt, kc, vct)


def _nsa_kernel(qt_ref, ks_ref, kw_ref, vst_ref, vwt_ref, selb_ref, ocmpt_ref, gt_ref, y_ref,
                yt_scr, qp_scr, m_scr, acc_scr, st_scr, mx_scr, bias_scr, *, tq, n_sel, look):
    tk = tq
    nw = WINDOW // tk
    sel_per_tile = tk // SEL_BLOCK
    den_rows = 16
    i = pl.program_id(1)
    d0 = (lax.broadcasted_iota(jnp.int32, (tk, tq), 1)
          - lax.broadcasted_iota(jnp.int32, (tk, tq), 0))
    caus = jnp.where(d0 >= 0, 0.0, NEG_INF)
    lowb = jnp.where(d0 < 0, 0.0, NEG_INF)
    ones_rows = jnp.ones((den_rows, tk), BF16)
    ext = 16
    key_blk = lax.broadcasted_iota(jnp.int32, (tk, ext), 0) // SEL_BLOCK
    ext_col = lax.broadcasted_iota(jnp.int32, (tk, ext), 1)
    gates = _sigmoid(gt_ref[...])

    for hh in range(NSA_HEADS):
        qp_scr[hh] = _pad_query(qt_ref[hh * NSA_DH:(hh + 1) * NSA_DH, :], hh // NSA_HPG)
    m_scr[...] = jnp.full(m_scr.shape, NEG_INF, F32)
    acc_scr[...] = jnp.zeros(acc_scr.shape, F32)

    SLC = (0, ks_ref, vst_ref, True)
    WIN = (1, kw_ref, vwt_ref, False)

    def run(tiles):
        loaded = []
        for ti, ((br, k_ref, vt_ref, sel_on), kt, mask) in enumerate(tiles):
            r0 = pl.multiple_of(kt * tk, tk)
            ktiles = [k_ref[pl.ds(r0, tk), pair * LANE:(pair + 1) * LANE]
                      for pair in range(NSA_GROUPS // 2)]
            vt_augs = [jnp.concatenate([vt_ref[g * NSA_DH:(g + 1) * NSA_DH, pl.ds(r0, tk)], ones_rows],
                                       axis=0) for g in range(NSA_GROUPS)]
            slabs = None
            if sel_on:
                row0 = kt * sel_per_tile
                slab0 = pl.multiple_of((row0 // SLAB_ROWS) * SLAB_ROWS, SLAB_ROWS)
                hot = jnp.where(key_blk + row0 % SLAB_ROWS == ext_col, 1.0, 0.0).astype(BF16)
                ktiles = [jnp.concatenate([kp, hot], axis=1) for kp in ktiles]
                slabs = [jnp.concatenate(
                    [selb_ref[pl.ds(g * n_sel + slab0, SLAB_ROWS), :],
                     jnp.zeros((ext - SLAB_ROWS, tq), F32)], axis=0).astype(BF16)
                    for g in range(NSA_GROUPS)]
            loaded.append((ktiles, vt_augs, slabs))
            if mask is not None:
                bias_scr[ti] = mask

        def qk(ti, hh):
            g = hh // NSA_HPG
            ktiles, _, slabs = loaded[ti]
            q = qp_scr[hh]
            if slabs is not None:
                q = jnp.concatenate([q, slabs[g]], axis=0)
            st = jnp.dot(ktiles[g // 2], q, preferred_element_type=F32)
            if tiles[ti][2] is not None:
                st = st + bias_scr[ti]
            st_scr[ti * NSA_HEADS + hh] = st
            mx_scr[ti * NSA_HEADS + hh] = jnp.max(st, axis=0, keepdims=True)

        def softmax_pv(ti, hh):
            br = tiles[ti][0][0]
            vt_aug = loaded[ti][1][hh // NSA_HPG]
            m = m_scr[br, hh]
            m_new = jnp.maximum(m, mx_scr[ti * NSA_HEADS + hh])
            alpha = jnp.exp2(m - m_new)
            p = jnp.exp2(st_scr[ti * NSA_HEADS + hh] - m_new).astype(BF16)
            acc_scr[br, hh] = alpha * acc_scr[br, hh] + jnp.dot(vt_aug, p, preferred_element_type=F32)
            m_scr[br, hh] = m_new

        seq = [(ti, hh) for ti in range(len(tiles)) for hh in range(NSA_HEADS)]
        for pos in range(min(look, len(seq))):
            qk(*seq[pos])
        for pos, item in enumerate(seq):
            if pos + look < len(seq):
                qk(*seq[pos + look])
            softmax_pv(*item)

    def slc_body(j, c):
        run([(SLC, 2 * j, None), (SLC, 2 * j + 1, None)])
        return c

    lax.fori_loop(0, i // 2, slc_body, 0)

    def tail(par, v):
        slc = [(SLC, i - 1, None)] * par + [(SLC, i, caus)]
        if v < nw:
            win = [(WIN, kt, None) for kt in range(v)] + [(WIN, i, caus)]
        else:
            win = ([(WIN, i - nw, lowb)] + [(WIN, i - nw + d, None) for d in range(1, nw)]
                   + [(WIN, i, caus)])
        run(slc + win)

    for v in range(nw):
        @pl.when(i == v)
        def _(v=v):
            tail(v % 2, v)

    for par in range(2):
        @pl.when((i >= nw) & (i % 2 == par))
        def _(par=par):
            tail(par, nw)

    for hh in range(NSA_HEADS):
        acc_s = acc_scr[0, hh]
        acc_w = acc_scr[1, hh]
        o_slc = acc_s[0:NSA_DH] * (1.0 / acc_s[NSA_DH:NSA_DH + 1])
        o_win = acc_w[0:NSA_DH] * (1.0 / acc_w[NSA_DH:NSA_DH + 1])
        o_cmp = ocmpt_ref[hh * NSA_DH:(hh + 1) * NSA_DH, :].astype(F32)
        gc = ZG_GNSA + 3 * hh
        yt_scr[hh * NSA_DH:(hh + 1) * NSA_DH, :] = (
            gates[gc:gc + 1] * o_cmp + gates[gc + 1:gc + 2] * o_slc + gates[gc + 2:gc + 3] * o_win)

    y_ref[...] = yt_scr[...].T.astype(BF16)


def _nsa(zt, zb3, selb, ocmpt, zgt, tq=256, look=5):
    b, _, s = zt.shape
    n_sel = s // SEL_BLOCK
    tk = tq
    den_rows = 16
    max_tiles = WINDOW // tk + 3
    assert WINDOW % tq == 0 and tq % SEL_BLOCK == 0 and SLAB_ROWS % (tk // SEL_BLOCK) == 0
    kern = functools.partial(_nsa_kernel, tq=tq, n_sel=n_sel, look=look)
    return pl.pallas_call(
        kern,
        grid=(b, s // tq),
        in_specs=[
            pl.BlockSpec((None, NSA_Q, tq), lambda bi, j: (bi, ZT_Q // NSA_Q, j)),
            pl.BlockSpec((None, s, NSA_KV), lambda bi, j: (bi, 0, ZB_KSLC // NSA_KV)),
            pl.BlockSpec((None, s, NSA_KV), lambda bi, j: (bi, 0, ZB_KWIN // NSA_KV)),
            pl.BlockSpec((None, NSA_KV, s), lambda bi, j: (bi, ZT_VSLC // NSA_KV, 0)),
            pl.BlockSpec((None, NSA_KV, s), lambda bi, j: (bi, ZT_VWIN // NSA_KV, 0)),
            pl.BlockSpec((None, NSA_GROUPS * n_sel, tq), lambda bi, j: (bi, 0, j)),
            pl.BlockSpec((None, NSA_Q, tq), lambda bi, j: (bi, 0, j)),
            pl.BlockSpec((None, ZG_WIDTH, tq), lambda bi, j: (bi, 0, j)),
        ],
        out_specs=pl.BlockSpec((None, tq, NSA_Q), lambda bi, j: (bi, j, 0)),
        out_shape=jax.ShapeDtypeStruct((b, s, NSA_Q), BF16),
        scratch_shapes=[
            pltpu.VMEM((NSA_Q, tq), F32),
            pltpu.VMEM((NSA_HEADS, 2 * NSA_DH, tq), BF16),
            pltpu.VMEM((2, NSA_HEADS, 1, tq), F32),
            pltpu.VMEM((2, NSA_HEADS, NSA_DH + den_rows, tq), F32),
            pltpu.VMEM((max_tiles * NSA_HEADS, tk, tq), F32),
            pltpu.VMEM((max_tiles * NSA_HEADS, 1, tq), F32),
            pltpu.VMEM((max_tiles, tk, tq), F32),
        ],
        compiler_params=_cparams("parallel", "arbitrary"),
        name="nsa",
    )(zt, zb3, zb3, zt, zt, selb, ocmpt, zgt)


def _mlstm_kernel(q_ref, k_ref, v_ref, o_ref, zg_ref, cw_ref, cb_ref, gb_ref, hg_ref, y_ref,
                  xbuf, c_st, n_st, m_st, *, nb):
    L = ML_CHUNK
    hist = xbuf.shape[1]
    @pl.when(pl.program_id(1) == 0)
    def _():
        xbuf[...] = jnp.zeros_like(xbuf)
        c_st[...] = jnp.zeros_like(c_st)
        n_st[...] = jnp.zeros_like(n_st)
        m_st[...] = jnp.zeros_like(m_st)

    row = lax.broadcasted_iota(jnp.int32, (L, L), 0)
    col = lax.broadcasted_iota(jnp.int32, (L, L), 1)
    tril = row >= col
    tril_f = jnp.where(tril, 1.0, 0.0)
    streams = [(bb, h) for bb in range(nb) for h in range(ML_HEADS)]

    sr = lax.broadcasted_iota(jnp.int32, (CONV_WIDTH * L, hist + L), 0)
    scol = lax.broadcasted_iota(jnp.int32, (CONV_WIDTH * L, hist + L), 1)
    shift = jnp.where(scol == hist - (CONV_WIDTH - 1) + sr % L + sr // L, 1.0, 0.0).astype(BF16)
    qks = []
    for bb in range(nb):
        halves = []
        for hf, x_ref in enumerate((q_ref, k_ref)):
            cur = x_ref[bb]
            ext = jnp.concatenate([xbuf[bb, :, hf * ML_W:(hf + 1) * ML_W], cur], axis=0)
            taps = jnp.dot(shift, ext, preferred_element_type=F32)
            conv = jnp.zeros((L, ML_W), F32) + cb_ref[:, hf * ML_W:(hf + 1) * ML_W]
            for j in range(CONV_WIDTH):
                conv = conv + taps[j * L:(j + 1) * L, :] * cw_ref[j:j + 1, hf * ML_W:(hf + 1) * ML_W]
            xbuf[bb, :, hf * ML_W:(hf + 1) * ML_W] = cur[L - hist:L, :]
            halves.append(_silu(conv))
        qks.append(jnp.concatenate(halves, axis=1))

    qs = {(bb, h): qks[bb][:, h * ML_DH:(h + 1) * ML_DH] for bb, h in streams}
    ks = {(bb, h): qks[bb][:, ML_W + h * ML_DH:ML_W + (h + 1) * ML_DH] * (ML_DH ** -0.5)
          for bb, h in streams}
    qbs = {s: qs[s].astype(BF16) for s in streams}
    vbs = {(bb, h): v_ref[bb, :, h * ML_DH:(h + 1) * ML_DH] for bb, h in streams}
    c_olds = {(bb, h): c_st[bb * ML_HEADS + h] for bb, h in streams}
    n_olds = {(bb, h): n_st[bb, h:h + 1, :] for bb, h in streams}
    s_qk = {s: lax.dot_general(qbs[s], ks[s].astype(BF16), _NT, preferred_element_type=F32)
            for s in streams}
    cqs = {s: lax.dot_general(qbs[s], c_olds[s].astype(BF16), _NT, preferred_element_type=F32)
           for s in streams}

    g = []
    for bb in range(nb):
        gates = zg_ref[bb] + gb_ref[...]
        bcum = jnp.dot(tril_f, _log_sigmoid(gates), preferred_element_type=F32,
                       precision=lax.Precision.HIGHEST)
        b_al = pltpu.roll(bcum, shift=ZG_WIDTH - ML_HEADS, axis=1)
        m_prev = m_st[bb, 0:1, :]
        b_end = b_al[L - 1:L, :]
        inter = b_al + m_prev
        wlog = b_end - b_al + gates
        m_new = jnp.maximum(b_end + m_prev, jnp.max(wlog, axis=0, keepdims=True))
        ws = jnp.exp(wlog - m_new)
        decay = jnp.exp(b_end + m_prev - m_new)
        m_st[bb, 0:1, :] = m_new
        r_t = (gates - b_al).T
        g.append((b_al, inter, ws, decay, r_t))

    for bb, h in streams:
        c = ZG_IF + h
        _, _, ws, decay, _ = g[bb]
        kw = ks[bb, h] * ws[:, c:c + 1]
        upd = lax.dot_general(vbs[bb, h], kw.astype(BF16), _TN, preferred_element_type=F32)
        c_st[bb * ML_HEADS + h] = decay[:, c:c + 1] * c_olds[bb, h] + upd
        n_st[bb, h:h + 1, :] = decay[:, c:c + 1] * n_olds[bb, h] + jnp.sum(kw, axis=0, keepdims=True)

    for bb, h in streams:
        c = ZG_IF + h
        b_al, inter, _, _, r_t = g[bb]
        dlog = jnp.where(tril, b_al[:, c:c + 1] + r_t[c:c + 1, :], -jnp.inf)
        inter_c = inter[:, c:c + 1]
        m_t = jnp.maximum(inter_c, jnp.max(dlog, axis=-1, keepdims=True))
        dw = jnp.exp(dlog - m_t)
        iw = jnp.exp(inter_c - m_t)
        sqk = s_qk[bb, h] * dw
        num = jnp.dot(sqk.astype(BF16), vbs[bb, h], preferred_element_type=F32) + iw * cqs[bb, h]
        den = (jnp.sum(sqk, axis=-1, keepdims=True)
               + iw * jnp.sum(qs[bb, h] * n_olds[bb, h], axis=-1, keepdims=True))
        hs = num / jnp.maximum(jnp.abs(den), jnp.exp(-m_t))
        hn = hs * lax.rsqrt(jnp.mean(hs * hs, axis=-1, keepdims=True) + EPS)
        hn = hn * hg_ref[:, h * ML_DH:(h + 1) * ML_DH]
        og = _sigmoid(o_ref[bb, :, h * ML_DH:(h + 1) * ML_DH].astype(F32))
        y_ref[bb, :, h * ML_DH:(h + 1) * ML_DH] = (og * hn).astype(BF16)


def _mlstm(zb3, zg3, conv_w, conv_b, gate_b, head_g, nb=2):
    b, s, _ = zb3.shape
    L = ML_CHUNK
    qb = ZB_QKVML // ML_W
    assert b % nb == 0

    def zspec(k):
        return pl.BlockSpec((nb, L, ML_W), lambda bi, c, k=k: (bi, c, k))

    return pl.pallas_call(
        functools.partial(_mlstm_kernel, nb=nb),
        grid=(b // nb, s // L),
        in_specs=[
            zspec(qb), zspec(qb + 1), zspec(qb + 2), zspec(ZB_OML // ML_W),
            pl.BlockSpec((nb, L, ZG_WIDTH), lambda bi, c: (bi, c, 0)),
            pl.BlockSpec((CONV_WIDTH, 2 * ML_W), lambda bi, c: (0, 0)),
            pl.BlockSpec((1, 2 * ML_W), lambda bi, c: (0, 0)),
            pl.BlockSpec((1, ZG_WIDTH), lambda bi, c: (0, 0)),
            pl.BlockSpec((1, ML_W), lambda bi, c: (0, 0)),
        ],
        out_specs=pl.BlockSpec((nb, L, ML_W), lambda bi, c: (bi, c, 0)),
        out_shape=jax.ShapeDtypeStruct((b, s, ML_W), BF16),
        scratch_shapes=[
            pltpu.VMEM((nb, 16, 2 * ML_W), BF16),
            pltpu.VMEM((nb * ML_HEADS, ML_DH, ML_DH), F32),
            pltpu.VMEM((nb, 8, ML_DH), F32),
            pltpu.VMEM((nb, 8, LANE), F32),
        ],
        compiler_params=_cparams("parallel", "arbitrary"),
        name="mlstm",
    )(zb3, zb3, zb3, zb3, zg3, conv_w, conv_b, gate_b, head_g)


def _memkv_kernel(mem_ref, g_ref, w_ref, kv_ref):
    hb = _rms(mem_ref[...], g_ref[...]).astype(BF16)
    kv_ref[...] = jnp.dot(hb, w_ref[...], preferred_element_type=F32).astype(BF16)


def _memkv(mem, g, wb):
    b, m, _ = mem.shape
    return pl.pallas_call(
        _memkv_kernel,
        grid=(b,),
        in_specs=[
            pl.BlockSpec((None, m, D_MODEL), lambda i: (i, 0, 0)),
            pl.BlockSpec((1, D_MODEL), lambda i: (0, 0)),
            pl.BlockSpec((D_MODEL, 2 * MEM_W), lambda i: (0, 0)),
        ],
        out_specs=pl.BlockSpec((None, m, 2 * MEM_W), lambda i: (i, 0, 0)),
        out_shape=jax.ShapeDtypeStruct((b, m, 2 * MEM_W), BF16),
        compiler_params=_cparams("parallel"),
        name="memkv",
    )(mem, g, wb)


def _memattn_kernel(q_ref, kv_ref, y_ref):
    for h in range(MEM_HEADS):
        q = q_ref[:, h * MEM_DH:(h + 1) * MEM_DH]
        k = kv_ref[:, h * MEM_DH:(h + 1) * MEM_DH]
        v = kv_ref[:, MEM_W + h * MEM_DH:MEM_W + (h + 1) * MEM_DH]
        s = lax.dot_general(q, k, _NT, preferred_element_type=F32) * MEM_SCALE
        e = jnp.exp(s - jnp.max(s, axis=-1, keepdims=True))
        p = e / jnp.sum(e, axis=-1, keepdims=True)
        o = jnp.dot(p.astype(BF16), v, preferred_element_type=F32)
        y_ref[:, h * MEM_DH:(h + 1) * MEM_DH] = o.astype(BF16)


def _memattn(zb3, kvm, tq=512):
    b, s, _ = zb3.shape
    m = kvm.shape[1]
    return pl.pallas_call(
        _memattn_kernel,
        grid=(b, s // tq),
        in_specs=[
            pl.BlockSpec((None, tq, MEM_W), lambda i, j: (i, j, ZB_QMEM // MEM_W)),
            pl.BlockSpec((None, m, 2 * MEM_W), lambda i, j: (i, 0, 0)),
        ],
        out_specs=pl.BlockSpec((None, tq, MEM_W), lambda i, j: (i, j, 0)),
        out_shape=jax.ShapeDtypeStruct((b, s, MEM_W), BF16),
        compiler_params=_cparams("parallel", "parallel"),
        name="memattn",
    )(zb3, kvm)


def _merge_kernel(x_ref, yn_ref, yl_ref, ym_ref, g0_ref, g1_ref, g2_ref,
                  wn_ref, wl_ref, wm_ref, wo_ref, gp_ref, out_ref):
    y = _sigmoid(g0_ref[...].astype(F32)) * jnp.dot(yn_ref[...], wn_ref[...], preferred_element_type=F32)
    y = y + _sigmoid(g1_ref[...].astype(F32)) * jnp.dot(yl_ref[...], wl_ref[...], preferred_element_type=F32)
    y = y + _sigmoid(g2_ref[...].astype(F32)) * jnp.dot(ym_ref[...], wm_ref[...], preferred_element_type=F32)
    u = jnp.dot(y.astype(BF16), wo_ref[...], preferred_element_type=F32)
    out_ref[...] = x_ref[...] + _rms(u, gp_ref[...])


def _merge(x2, yn, yl, ym, zb, wn, wl, wm, wo, gp, tm=512):
    t = x2.shape[0]
    gm = ZB_GMERGE // D_MODEL

    def const(shape):
        return pl.BlockSpec(shape, lambda i: (0, 0), pipeline_mode=pl.Buffered(1))

    return pl.pallas_call(
        _merge_kernel,
        grid=(t // tm,),
        in_specs=[
            pl.BlockSpec((tm, D_MODEL), lambda i: (i, 0)),
            pl.BlockSpec((tm, NSA_Q), lambda i: (i, 0)),
            pl.BlockSpec((tm, ML_W), lambda i: (i, 0)),
            pl.BlockSpec((tm, MEM_W), lambda i: (i, 0)),
            pl.BlockSpec((tm, D_MODEL), lambda i: (i, gm)),
            pl.BlockSpec((tm, D_MODEL), lambda i: (i, gm + 1)),
            pl.BlockSpec((tm, D_MODEL), lambda i: (i, gm + 2)),
            const((NSA_Q, D_MODEL)), const((ML_W, D_MODEL)), const((MEM_W, D_MODEL)),
            const((D_MODEL, D_MODEL)), const((1, D_MODEL)),
        ],
        out_specs=pl.BlockSpec((tm, D_MODEL), lambda i: (i, 0)),
        out_shape=jax.ShapeDtypeStruct((t, D_MODEL), F32),
        compiler_params=_cparams("parallel"),
        name="merge",
    )(x2, yn, yl, ym, zb, zb, zb, wn, wl, wm, wo, gp)


def _ffn_kernel(x_ref, gpre_ref, wg_ref, wu_ref, wd_ref, gpost_ref, out_ref, h_ref, acc_ref):
    j = pl.program_id(1)

    @pl.when(j == 0)
    def _():
        h_ref[...] = _rms(x_ref[...], gpre_ref[...]).astype(BF16)
        acc_ref[...] = jnp.zeros_like(acc_ref)

    h = h_ref[...]
    gate = jnp.dot(h, wg_ref[...], preferred_element_type=F32)
    up = jnp.dot(h, wu_ref[...], preferred_element_type=F32)
    act = (_silu(gate) * up).astype(BF16)
    acc_ref[...] += jnp.dot(act, wd_ref[...], preferred_element_type=F32)

    @pl.when(j == pl.num_programs(1) - 1)
    def _():
        out_ref[...] = x_ref[...] + _rms(acc_ref[...], gpost_ref[...])


def _ffn(x2, gpre, w_in, w_down, gpost, tm=512, nf=1):
    t = x2.shape[0]
    tf = D_FF // nf
    assert tf % LANE == 0
    wmode = dict(pipeline_mode=pl.Buffered(1)) if nf == 1 else {}
    return pl.pallas_call(
        _ffn_kernel,
        grid=(t // tm, nf),
        in_specs=[
            pl.BlockSpec((tm, D_MODEL), lambda i, j: (i, 0)),
            pl.BlockSpec((1, D_MODEL), lambda i, j: (0, 0)),
            pl.BlockSpec((D_MODEL, tf), lambda i, j: (0, j), **wmode),
            pl.BlockSpec((D_MODEL, tf), lambda i, j: (0, nf + j), **wmode),
            pl.BlockSpec((tf, D_MODEL), lambda i, j: (j, 0), **wmode),
            pl.BlockSpec((1, D_MODEL), lambda i, j: (0, 0)),
        ],
        out_specs=pl.BlockSpec((tm, D_MODEL), lambda i, j: (i, 0)),
        out_shape=jax.ShapeDtypeStruct((t, D_MODEL), F32),
        scratch_shapes=[pltpu.VMEM((tm, D_MODEL), BF16), pltpu.VMEM((tm, D_MODEL), F32)],
        compiler_params=_cparams("parallel", "arbitrary"),
        name="ffn",
    )(x2, gpre, w_in, w_in, w_down, gpost)


def _layer(x, mem, p):
    b, s, _ = x.shape
    t = b * s
    x2 = x.reshape(t, D_MODEL)

    wn, wt, wc = _wprep(p["w_in"].T)
    zt, zb, zg, zgt, kvc = _inproj(x2, p["g_pre_mix"].reshape(1, D_MODEL), wt, wn, wc, b, s)
    zb3 = zb.reshape(b, s, ZB_WIDTH)
    zg3 = zg.reshape(b, s, ZG_WIDTH)

    half = (CMP_BLOCK // 2) * NSA_DH
    pe = jnp.stack([p["cmp_pe_k"], p["cmp_pe_v"]]).reshape(2, 1, 2 * half)
    pe = jnp.pad(pe, ((0, 0), (0, 7), (0, 0))).astype(BF16)
    w1 = jnp.stack([p["cmp_w1_k"], p["cmp_w1_v"]])
    eye2 = jnp.eye(2, dtype=F32)
    wpa = jnp.einsum("ab,kldh->kladbh", eye2, w1[:, :CMP_STRIDE]).reshape(2, CMP_STRIDE, LANE, 2 * CMP_HIDDEN)
    wpb = jnp.einsum("ab,kldh->kladbh", eye2, w1[:, CMP_STRIDE:]).reshape(2, CMP_STRIDE, LANE, 2 * CMP_HIDDEN)
    wp = jnp.concatenate([wpa, wpb], axis=-1).astype(BF16)
    kc, vct = _compress(kvc.reshape(b, s, KVC_WIDTH), pe, w1.reshape(2, 2 * half, CMP_HIDDEN).astype(BF16), wp,
                        p["cmp_w2_k"].astype(BF16), p["cmp_w2_v"].T.astype(BF16))
    ocmpt, selb = _cmpsel(zt, kc, vct)
    y_nsa = _nsa(zt, zb3, selb, ocmpt, zgt)

    gate_b = jnp.zeros((1, ZG_WIDTH), F32).at[0, ZG_IF:ZG_IF + 2 * ML_HEADS].set(p["ml_gate_b"])
    y_ml = _mlstm(zb3, zg3, p["ml_conv_w"], p["ml_conv_b"].reshape(1, 2 * ML_W), gate_b,
                  p["ml_head_g"].reshape(1, ML_W))

    kvm = _memkv(mem, p["g_mem"].reshape(1, D_MODEL), p["w_mem_kv"].astype(BF16))
    y_mem = _memattn(zb3, kvm)

    x1 = _merge(x2, y_nsa.reshape(t, NSA_Q), y_ml.reshape(t, ML_W), y_mem.reshape(t, MEM_W), zb,
                p["w_proj_nsa"].astype(BF16), p["w_proj_ml"].astype(BF16),
                p["w_proj_mem"].astype(BF16), p["w_out"].astype(BF16),
                p["g_post_mix"].reshape(1, D_MODEL))
    x2o = _ffn(x1, p["g_pre_ffn"].reshape(1, D_MODEL), p["w_ffn_in"].astype(BF16),
               p["w_ffn_down"].astype(BF16), p["g_post_ffn"].reshape(1, D_MODEL))
    return x2o.reshape(b, s, D_MODEL)


def kernel(x, mem, g_pre_mix, w_in, cmp_pe_k, cmp_w1_k, cmp_w2_k, cmp_pe_v, cmp_w1_v, cmp_w2_v,
           ml_conv_w, ml_conv_b, ml_gate_b, ml_head_g, g_mem, w_mem_kv, w_proj_nsa, w_proj_ml,
           w_proj_mem, w_out, g_post_mix, g_pre_ffn, w_ffn_in, w_ffn_down, g_post_ffn):
    params = dict(
        g_pre_mix=g_pre_mix, w_in=w_in, cmp_pe_k=cmp_pe_k, cmp_w1_k=cmp_w1_k, cmp_w2_k=cmp_w2_k,
        cmp_pe_v=cmp_pe_v, cmp_w1_v=cmp_w1_v, cmp_w2_v=cmp_w2_v, ml_conv_w=ml_conv_w,
        ml_conv_b=ml_conv_b, ml_gate_b=ml_gate_b, ml_head_g=ml_head_g, g_mem=g_mem,
        w_mem_kv=w_mem_kv, w_proj_nsa=w_proj_nsa, w_proj_ml=w_proj_ml, w_proj_mem=w_proj_mem,
        w_out=w_out, g_post_mix=g_post_mix, g_pre_ffn=g_pre_ffn, w_ffn_in=w_ffn_in,
        w_ffn_down=w_ffn_down, g_post_ffn=g_post_ffn)
    depth = w_in.shape[0]
    for l in range(depth):
        x = _layer(x, mem, {k: v[l] for k, v in params.items()})
    return x
```

```python
import functools
import math

import jax
import jax.numpy as jnp
from jax import lax
from jax.experimental import pallas as pl
from jax.experimental.pallas import tpu as pltpu

F32 = jnp.float32
BF16 = jnp.bfloat16

D_MODEL = 1024
EPS = 1e-6
NEG_INF = -1e30
BIG = 1e30

NSA_HEADS = 16
NSA_GROUPS = 4
NSA_HPG = NSA_HEADS // NSA_GROUPS
NSA_DH = 64
NSA_SCALE = NSA_DH ** -0.5
CMP_BLOCK = 32
CMP_STRIDE = 16
CMP_HIDDEN = 128
SEL_BLOCK = 64
SEL_TOPK = 16
N_LOCAL_FORCED = 2
WINDOW = 512

ML_HEADS = 4
ML_DH = 128
ML_CHUNK = 128
CONV_WIDTH = 4

MEM_HEADS = 4
MEM_DH = 128
MEM_SCALE = MEM_DH ** -0.5

D_FF = -(-8 * D_MODEL // (3 * 256)) * 256

NSA_Q = NSA_HEADS * NSA_DH
NSA_KV = NSA_GROUPS * NSA_DH
ML_W = ML_HEADS * ML_DH
MEM_W = MEM_HEADS * MEM_DH

_OFF_Q = 0
_OFF_KV = _OFF_Q + NSA_Q
_OFF_GNSA = _OFF_KV + 6 * NSA_KV
_OFF_QKVML = _OFF_GNSA + 3 * NSA_HEADS
_OFF_IF = _OFF_QKVML + 3 * ML_W
_OFF_OML = _OFF_IF + 2 * ML_HEADS
_OFF_QMEM = _OFF_OML + ML_W
_OFF_GMERGE = _OFF_QMEM + MEM_W
_IN_WIDTH = _OFF_GMERGE + 3 * D_MODEL

ZB_GMERGE = 0
ZB_KSLC = ZB_GMERGE + 3 * D_MODEL
ZB_KWIN = ZB_KSLC + NSA_KV
ZB_QKVML = ZB_KWIN + NSA_KV
ZB_OML = ZB_QKVML + 3 * ML_W
ZB_QMEM = ZB_OML + ML_W
ZB_WIDTH = ZB_QMEM + MEM_W
KVC_WIDTH = 2 * NSA_KV
ZT_Q = 0
ZT_VSLC = ZT_Q + NSA_Q
ZT_VWIN = ZT_VSLC + NSA_KV
ZT_ROWS = ZT_VWIN + NSA_KV
ZG_WIDTH = 128
ZG_GNSA = 0
ZG_IF = 3 * NSA_HEADS

LANE = 128
SLAB_ROWS = 8
VMEM_LIMIT = 56 * 1024 * 1024
LOG2E = math.log2(math.e)

_NT = (((1,), (1,)), ((), ()))
_TN = (((0,), (0,)), ((), ()))


def _cparams(*sem):
    return pltpu.CompilerParams(dimension_semantics=sem, vmem_limit_bytes=VMEM_LIMIT)


def _rms(x, g):
    return x * lax.rsqrt(jnp.mean(x * x, axis=-1, keepdims=True) + EPS) * g


def _sigmoid(x):
    return 1.0 / (1.0 + jnp.exp(-x))


def _silu(x):
    return x * _sigmoid(x)


def _log_sigmoid(x):
    return jnp.minimum(x, 0.0) - jnp.log(1.0 + jnp.exp(-jnp.abs(x)))


def _wprep_kernel(w_ref, wn_ref, wt_ref, wc_ref):
    def rows(lo, hi):
        return w_ref[lo:hi, :]

    def kv_rows(k):
        return rows(_OFF_KV + k * NSA_KV, _OFF_KV + (k + 1) * NSA_KV)

    wn_ref[ZB_GMERGE:ZB_KSLC, :] = rows(_OFF_GMERGE, _IN_WIDTH).astype(BF16)
    wn_ref[ZB_KSLC:ZB_KWIN, :] = kv_rows(2).astype(BF16)
    wn_ref[ZB_KWIN:ZB_QKVML, :] = kv_rows(4).astype(BF16)
    wn_ref[ZB_QKVML:ZB_OML, :] = rows(_OFF_QKVML, _OFF_IF).astype(BF16)
    wn_ref[ZB_OML:ZB_WIDTH, :] = rows(_OFF_OML, _OFF_GMERGE).astype(BF16)
    wt_ref[ZT_Q:ZT_VSLC, :] = (rows(_OFF_Q, _OFF_KV) * (NSA_SCALE * LOG2E)).astype(BF16)
    wt_ref[ZT_VSLC:ZT_VWIN, :] = kv_rows(3).astype(BF16)
    wt_ref[ZT_VWIN:ZT_ROWS, :] = kv_rows(5).astype(BF16)
    n_g = 3 * NSA_HEADS
    n_if = 2 * ML_HEADS
    wt_ref[ZT_ROWS:ZT_ROWS + ZG_WIDTH, :] = jnp.zeros((ZG_WIDTH, wt_ref.shape[1]), BF16)
    wt_ref[ZT_ROWS + ZG_GNSA:ZT_ROWS + ZG_GNSA + n_g, :] = rows(_OFF_GNSA, _OFF_QKVML).astype(BF16)
    wt_ref[ZT_ROWS + ZG_IF:ZT_ROWS + ZG_IF + n_if, :] = rows(_OFF_IF, _OFF_OML).astype(BF16)
    wc_ref[...] = rows(_OFF_KV, _OFF_KV + KVC_WIDTH).astype(BF16)


def _wprep(w_t, tl=128):
    assert ZB_QMEM == ZB_OML + ML_W and _OFF_QMEM == _OFF_OML + ML_W
    return pl.pallas_call(
        _wprep_kernel,
        grid=(D_MODEL // tl,),
        in_specs=[pl.BlockSpec((_IN_WIDTH, tl), lambda i: (0, i))],
        out_specs=[
            pl.BlockSpec((ZB_WIDTH, tl), lambda i: (0, i)),
            pl.BlockSpec((ZT_ROWS + ZG_WIDTH, tl), lambda i: (0, i)),
            pl.BlockSpec((KVC_WIDTH, tl), lambda i: (0, i)),
        ],
        out_shape=[
            jax.ShapeDtypeStruct((ZB_WIDTH, D_MODEL), BF16),
            jax.ShapeDtypeStruct((ZT_ROWS + ZG_WIDTH, D_MODEL), BF16),
            jax.ShapeDtypeStruct((KVC_WIDTH, D_MODEL), BF16),
        ],
        compiler_params=_cparams("parallel"),
        name="wprep",
    )(w_t)


def _inproj_kernel(x_ref, g_ref, wt_ref, wn_ref, wc_ref,
                   zt_ref, zb_ref, zg_ref, zgt_ref, kvc_ref, h_ref, *, nt):
    j = pl.program_id(1)

    @pl.when(j == 0)
    def _():
        hb = _rms(x_ref[...], g_ref[...]).astype(BF16)
        h_ref[...] = hb
        kvc_ref[...] = lax.dot_general(hb, wc_ref[...], _NT, preferred_element_type=F32)
        zf = lax.dot_general(wt_ref[...], hb, _NT, preferred_element_type=F32)
        zt_ref[...] = zf[0:ZT_ROWS, :].astype(BF16)
        gt = zf[ZT_ROWS:ZT_ROWS + ZG_WIDTH, :]
        zgt_ref[...] = gt
        zg_ref[...] = gt.T

    @pl.when(j >= nt)
    def _():
        tn = zb_ref.shape[1]
        r0 = pl.multiple_of((j - nt) * tn, tn)
        zb_ref[...] = lax.dot_general(h_ref[...], wn_ref[pl.ds(r0, tn), :], _NT,
                                      preferred_element_type=F32).astype(BF16)


def _inproj(x2, g, wt, wn, wc, b, s, tm=1024, nn=3):
    t = x2.shape[0]
    nt = 1
    tn = ZB_WIDTH // nn
    assert tn * nn == ZB_WIDTH and tn % LANE == 0 and s % tm == 0
    spb = s // tm
    kern = functools.partial(_inproj_kernel, nt=nt)
    once = dict(pipeline_mode=pl.Buffered(1))
    return pl.pallas_call(
        kern,
        grid=(t // tm, nt + nn),
        in_specs=[
            pl.BlockSpec((tm, D_MODEL), lambda i, j: (i, 0)),
            pl.BlockSpec((1, D_MODEL), lambda i, j: (0, 0)),
            pl.BlockSpec((ZT_ROWS + ZG_WIDTH, D_MODEL), lambda i, j: (0, 0), **once),
            pl.BlockSpec((ZB_WIDTH, D_MODEL), lambda i, j: (0, 0), **once),
            pl.BlockSpec((KVC_WIDTH, D_MODEL), lambda i, j: (0, 0), **once),
        ],
        out_specs=[
            pl.BlockSpec((None, ZT_ROWS, tm), lambda i, j: (i // spb, 0, i % spb)),
            pl.BlockSpec((tm, tn), lambda i, j: (i, jnp.maximum(j - nt, 0))),
            pl.BlockSpec((tm, ZG_WIDTH), lambda i, j: (i, 0)),
            pl.BlockSpec((None, ZG_WIDTH, tm), lambda i, j: (i // spb, 0, i % spb)),
            pl.BlockSpec((tm, KVC_WIDTH), lambda i, j: (i, 0)),
        ],
        out_shape=[
            jax.ShapeDtypeStruct((b, ZT_ROWS, s), BF16),
            jax.ShapeDtypeStruct((t, ZB_WIDTH), BF16),
            jax.ShapeDtypeStruct((t, ZG_WIDTH), F32),
            jax.ShapeDtypeStruct((b, ZG_WIDTH, s), F32),
            jax.ShapeDtypeStruct((t, KVC_WIDTH), F32),
        ],
        scratch_shapes=[pltpu.VMEM((tm, D_MODEL), BF16)],
        compiler_params=_cparams("parallel", "arbitrary"),
        name="inproj",
    )(x2, g, wt, wn, wc)


def _compress_kernel(a0_ref, a1_ref, a2_ref, a3_ref, pe_ref, w1_ref, wp_ref, w2k_ref, w2vt_ref,
                     kc_ref, vct_ref):
    a_refs = (a0_ref, a1_ref, a2_ref, a3_ref)
    n_rows = a0_ref.shape[0] // CMP_STRIDE
    for kind in range(2):
        pe = pe_ref[kind]
        c = jnp.dot(pe, w1_ref[kind], preferred_element_type=F32)[0:1, :]
        for pair in range(NSA_GROUPS // 2):
            a_ref = a_refs[kind * 2 + pair]
            acc = jnp.zeros((n_rows, 4 * CMP_HIDDEN), F32)
            for l in range(CMP_STRIDE):
                rows = a_ref[pl.ds(l, n_rows, stride=CMP_STRIDE), :].astype(BF16)
                acc = acc + jnp.dot(rows, wp_ref[kind, l], preferred_element_type=F32)
            for gi in range(2):
                g = 2 * pair + gi
                p = acc[:, gi * CMP_HIDDEN:(gi + 1) * CMP_HIDDEN]
                q = acc[:, (2 + gi) * CMP_HIDDEN:(3 + gi) * CMP_HIDDEN]
                qs = pltpu.roll(q, shift=n_rows - 1, axis=0)
                hid = _silu(p + qs + c).astype(BF16)
                if kind == 0:
                    o = jnp.dot(hid, w2k_ref[...], preferred_element_type=F32)
                    kc_ref[:, g * NSA_DH:(g + 1) * NSA_DH] = o.astype(BF16)
                else:
                    ot = lax.dot_general(w2vt_ref[...], hid, _NT, preferred_element_type=F32)
                    vct_ref[g * NSA_DH:(g + 1) * NSA_DH, :] = ot.astype(BF16)


def _compress(kvc3, pe, w1, wp, w2k, w2vt):
    b, s, _ = kvc3.shape
    n_rows = s // CMP_STRIDE
    half = (CMP_BLOCK // 2) * NSA_DH

    def col(k):
        return pl.BlockSpec((None, s, LANE), lambda i, k=k: (i, 0, k))

    return pl.pallas_call(
        _compress_kernel,
        grid=(b,),
        in_specs=[
            col(0), col(1), col(2), col(3),
            pl.BlockSpec((2, 8, 2 * half), lambda i: (0, 0, 0)),
            pl.BlockSpec((2, 2 * half, CMP_HIDDEN), lambda i: (0, 0, 0)),
            pl.BlockSpec((2, CMP_STRIDE, LANE, 4 * CMP_HIDDEN), lambda i: (0, 0, 0, 0)),
            pl.BlockSpec((CMP_HIDDEN, NSA_DH), lambda i: (0, 0)),
            pl.BlockSpec((NSA_DH, CMP_HIDDEN), lambda i: (0, 0)),
        ],
        out_specs=[
            pl.BlockSpec((None, n_rows, NSA_KV), lambda i: (i, 0, 0)),
            pl.BlockSpec((None, NSA_KV, n_rows), lambda i: (i, 0, 0)),
        ],
        out_shape=[
            jax.ShapeDtypeStruct((b, n_rows, NSA_KV), BF16),
            jax.ShapeDtypeStruct((b, NSA_KV, n_rows), BF16),
        ],
        compiler_params=_cparams("parallel"),
        name="compress",
    )(kvc3, kvc3, kvc3, kvc3, pe, w1, wp, w2k, w2vt)


def _pad_query(qt, g):
    z = jnp.zeros_like(qt)
    return jnp.concatenate([qt, z] if g % 2 == 0 else [z, qt], axis=0)


def _cmpsel_kernel(qt_ref, kc_ref, vct_ref, ocmpt_ref, selb_ref, *, tq, n_sel, topk):
    nc = kc_ref.shape[0]
    t0 = pl.program_id(1) * tq

    n_t = lax.broadcasted_iota(jnp.int32, (nc, tq), 0)
    t_t = t0 + lax.broadcasted_iota(jnp.int32, (nc, tq), 1)
    mask_t = (n_t * CMP_STRIDE + (CMP_BLOCK - 1)) <= t_t

    j_m = lax.broadcasted_iota(jnp.int32, (n_sel, nc), 0) * SEL_BLOCK
    c_m = lax.broadcasted_iota(jnp.int32, (n_sel, nc), 1) * CMP_STRIDE
    ov = jnp.minimum(c_m + CMP_BLOCK, j_m + SEL_BLOCK) - jnp.maximum(c_m, j_m)
    map_t = jnp.maximum(ov, 0).astype(F32) * (1.0 / CMP_BLOCK)

    j_s = lax.broadcasted_iota(jnp.int32, (n_sel, tq), 0)
    qblk = (t0 + lax.broadcasted_iota(jnp.int32, (n_sel, tq), 1)) // SEL_BLOCK
    rel = qblk - j_s
    causal = rel >= 0
    forced = causal & ((j_s == 0) | (rel < N_LOCAL_FORCED))

    def scores(g):
        kcp = kc_ref[:, (g // 2) * LANE:(g // 2 + 1) * LANE]
        return [jnp.dot(kcp, _pad_query(qt_ref[(g * NSA_HPG + h) * NSA_DH:(g * NSA_HPG + h + 1) * NSA_DH, :], g),
                        preferred_element_type=F32) for h in range(NSA_HPG)]

    sts_next = scores(0)
    for g in range(NSA_GROUPS):
        sts = sts_next
        if g + 1 < NSA_GROUPS:
            sts_next = scores(g + 1)
        vct = vct_ref[g * NSA_DH:(g + 1) * NSA_DH, :]
        masked = [jnp.where(mask_t, st, NEG_INF) for st in sts]
        ets = [jnp.where(mask_t, jnp.exp2(st - jnp.max(st, axis=0, keepdims=True)), 0.0) for st in masked]
        lts = [jnp.sum(et, axis=0, keepdims=True) for et in ets]
        pts = [et * (1.0 / jnp.where(lt > 0.0, lt, 1.0)) for et, lt in zip(ets, lts)]
        for h in range(NSA_HPG):
            hh = g * NSA_HPG + h
            ot = jnp.dot(vct, pts[h].astype(BF16), preferred_element_type=F32)
            ocmpt_ref[hh * NSA_DH:(hh + 1) * NSA_DH, :] = ot.astype(BF16)
        psum_t = (pts[0] + pts[1]) + (pts[2] + pts[3])
        imp_t = jnp.dot(map_t, psum_t, preferred_element_type=F32,
                        precision=lax.Precision.HIGHEST)
        score = jnp.where(forced, BIG, jnp.where(causal, imp_t, NEG_INF))
        rank = jnp.zeros((n_sel, tq), jnp.int32)
        for jp in range(n_sel):
            row = score[jp:jp + 1, :]
            before = (row > score) | ((row == score) & (j_s > jp))
            rank = rank + before.astype(jnp.int32)
        chosen = (rank < topk) & (score > 0.5 * NEG_INF)
        selb_ref[g * n_sel:(g + 1) * n_sel, :] = jnp.where(chosen, 0.0, NEG_INF)


def _cmpsel(zt, kc, vct, tq=256):
    b, _, s = zt.shape
    n_sel = s // SEL_BLOCK
    topk = min(SEL_TOPK, n_sel)
    nc = kc.shape[1]
    kern = functools.partial(_cmpsel_kernel, tq=tq, n_sel=n_sel, topk=topk)
    return pl.pallas_call(
        kern,
        grid=(b, s // tq),
        in_specs=[
            pl.BlockSpec((None, NSA_Q, tq), lambda i, j: (i, ZT_Q // NSA_Q, j)),
            pl.BlockSpec((None, nc, NSA_KV), lambda i, j: (i, 0, 0)),
            pl.BlockSpec((None, NSA_KV, nc), lambda i, j: (i, 0, 0)),
        ],
        out_specs=[
            pl.BlockSpec((None, NSA_Q, tq), lambda i, j: (i, 0, j)),
            pl.BlockSpec((None, NSA_GROUPS * n_sel, tq), lambda i, j: (i, 0, j)),
        ],
        out_shape=[
            jax.ShapeDtypeStruct((b, NSA_Q, s), BF16),
            jax.ShapeDtypeStruct((b, NSA_GROUPS * n_sel, s), F32),
        ],
        compiler_params=_cparams("parallel", "parallel"),
        name="cmpsel",
    )(zt, kc, vct)


def _nsa_kernel(qt_ref, ks_ref, kw_ref, vst_ref, vwt_ref, selb_ref, ocmpt_ref, gt_ref, y_ref,
                yt_scr, qp_scr, m_scr, acc_scr, st_scr, mx_scr, bias_scr, *, tq, n_sel, look):
    tk = tq
    nw = WINDOW // tk
    sel_per_tile = tk // SEL_BLOCK
    den_rows = 16
    i = pl.program_id(1)
    d0 = (lax.broadcasted_iota(jnp.int32, (tk, tq), 1)
          - lax.broadcasted_iota(jnp.int32, (tk, tq), 0))
    caus = jnp.where(d0 >= 0, 0.0, NEG_INF)
    lowb = jnp.where(d0 < 0, 0.0, NEG_INF)
    ones_rows = jnp.ones((den_rows, tk), BF16)
    ext = 16
    key_blk = lax.broadcasted_iota(jnp.int32, (tk, ext), 0) // SEL_BLOCK
    ext_col = lax.broadcasted_iota(jnp.int32, (tk, ext), 1)
    gates = _sigmoid(gt_ref[...])

    for hh in range(NSA_HEADS):
        qp_scr[hh] = _pad_query(qt_ref[hh * NSA_DH:(hh + 1) * NSA_DH, :], hh // NSA_HPG)
    m_scr[...] = jnp.full(m_scr.shape, NEG_INF, F32)
    acc_scr[...] = jnp.zeros(acc_scr.shape, F32)

    SLC = (0, ks_ref, vst_ref, True)
    WIN = (1, kw_ref, vwt_ref, False)

    def run(tiles):
        loaded = []
        for ti, ((br, k_ref, vt_ref, sel_on), kt, mask) in enumerate(tiles):
            r0 = pl.multiple_of(kt * tk, tk)
            ktiles = [k_ref[pl.ds(r0, tk), pair * LANE:(pair + 1) * LANE]
                      for pair in range(NSA_GROUPS // 2)]
            vt_augs = [jnp.concatenate([vt_ref[g * NSA_DH:(g + 1) * NSA_DH, pl.ds(r0, tk)], ones_rows],
                                       axis=0) for g in range(NSA_GROUPS)]
            slabs = None
            if sel_on:
                row0 = kt * sel_per_tile
                slab0 = pl.multiple_of((row0 // SLAB_ROWS) * SLAB_ROWS, SLAB_ROWS)
                hot = jnp.where(key_blk + row0 % SLAB_ROWS == ext_col, 1.0, 0.0).astype(BF16)
                ktiles = [jnp.concatenate([kp, hot], axis=1) for kp in ktiles]
                slabs = [jnp.concatenate(
                    [selb_ref[pl.ds(g * n_sel + slab0, SLAB_ROWS), :],
                     jnp.zeros((ext - SLAB_ROWS, tq), F32)], axis=0).astype(BF16)
                    for g in range(NSA_GROUPS)]
            loaded.append((ktiles, vt_augs, slabs))
            if mask is not None:
                bias_scr[ti] = mask

        def qk(ti, hh):
            g = hh // NSA_HPG
            ktiles, _, slabs = loaded[ti]
            q = qp_scr[hh]
            if slabs is not None:
                q = jnp.concatenate([q, slabs[g]], axis=0)
            st = jnp.dot(ktiles[g // 2], q, preferred_element_type=F32)
            if tiles[ti][2] is not None:
                st = st + bias_scr[ti]
            st_scr[ti * NSA_HEADS + hh] = st
            mx_scr[ti * NSA_HEADS + hh] = jnp.max(st, axis=0, keepdims=True)

        def softmax_pv(ti, hh):
            br = tiles[ti][0][0]
            vt_aug = loaded[ti][1][hh // NSA_HPG]
            m = m_scr[br, hh]
            m_new = jnp.maximum(m, mx_scr[ti * NSA_HEADS + hh])
            alpha = jnp.exp2(m - m_new)
            p = jnp.exp2(st_scr[ti * NSA_HEADS + hh] - m_new).astype(BF16)
            acc_scr[br, hh] = alpha * acc_scr[br, hh] + jnp.dot(vt_aug, p, preferred_element_type=F32)
            m_scr[br, hh] = m_new

        seq = [(ti, hh) for ti in range(len(tiles)) for hh in range(NSA_HEADS)]
        for pos in range(min(look, len(seq))):
            qk(*seq[pos])
        for pos, item in enumerate(seq):
            if pos + look < len(seq):
                qk(*seq[pos + look])
            softmax_pv(*item)

    def slc_body(j, c):
        run([(SLC, 2 * j, None), (SLC, 2 * j + 1, None)])
        return c

    lax.fori_loop(0, i // 2, slc_body, 0)

    def tail(par, v):
        slc = [(SLC, i - 1, None)] * par + [(SLC, i, caus)]
        if v < nw:
            win = [(WIN, kt, None) for kt in range(v)] + [(WIN, i, caus)]
        else:
            win = ([(WIN, i - nw, lowb)] + [(WIN, i - nw + d, None) for d in range(1, nw)]
                   + [(WIN, i, caus)])
        run(slc + win)

    for v in range(nw):
        @pl.when(i == v)
        def _(v=v):
            tail(v % 2, v)

    for par in range(2):
        @pl.when((i >= nw) & (i % 2 == par))
        def _(par=par):
            tail(par, nw)

    for hh in range(NSA_HEADS):
        acc_s = acc_scr[0, hh]
        acc_w = acc_scr[1, hh]
        o_slc = acc_s[0:NSA_DH] * (1.0 / acc_s[NSA_DH:NSA_DH + 1])
        o_win = acc_w[0:NSA_DH] * (1.0 / acc_w[NSA_DH:NSA_DH + 1])
        o_cmp = ocmpt_ref[hh * NSA_DH:(hh + 1) * NSA_DH, :].astype(F32)
        gc = ZG_GNSA + 3 * hh
        yt_scr[hh * NSA_DH:(hh + 1) * NSA_DH, :] = (
            gates[gc:gc + 1] * o_cmp + gates[gc + 1:gc + 2] * o_slc + gates[gc + 2:gc + 3] * o_win)

    y_ref[...] = yt_scr[...].T.astype(BF16)


def _nsa(zt, zb3, selb, ocmpt, zgt, tq=256, look=5):
    b, _, s = zt.shape
    n_sel = s // SEL_BLOCK
    tk = tq
    den_rows = 16
    max_tiles = WINDOW // tk + 3
    assert WINDOW % tq == 0 and tq % SEL_BLOCK == 0 and SLAB_ROWS % (tk // SEL_BLOCK) == 0
    kern = functools.partial(_nsa_kernel, tq=tq, n_sel=n_sel, look=look)
    return pl.pallas_call(
        kern,
        grid=(b, s // tq),
        in_specs=[
            pl.BlockSpec((None, NSA_Q, tq), lambda bi, j: (bi, ZT_Q // NSA_Q, j)),
            pl.BlockSpec((None, s, NSA_KV), lambda bi, j: (bi, 0, ZB_KSLC // NSA_KV)),
            pl.BlockSpec((None, s, NSA_KV), lambda bi, j: (bi, 0, ZB_KWIN // NSA_KV)),
            pl.BlockSpec((None, NSA_KV, s), lambda bi, j: (bi, ZT_VSLC // NSA_KV, 0)),
            pl.BlockSpec((None, NSA_KV, s), lambda bi, j: (bi, ZT_VWIN // NSA_KV, 0)),
            pl.BlockSpec((None, NSA_GROUPS * n_sel, tq), lambda bi, j: (bi, 0, j)),
            pl.BlockSpec((None, NSA_Q, tq), lambda bi, j: (bi, 0, j)),
            pl.BlockSpec((None, ZG_WIDTH, tq), lambda bi, j: (bi, 0, j)),
        ],
        out_specs=pl.BlockSpec((None, tq, NSA_Q), lambda bi, j: (bi, j, 0)),
        out_shape=jax.ShapeDtypeStruct((b, s, NSA_Q), BF16),
        scratch_shapes=[
            pltpu.VMEM((NSA_Q, tq), F32),
            pltpu.VMEM((NSA_HEADS, 2 * NSA_DH, tq), BF16),
            pltpu.VMEM((2, NSA_HEADS, 1, tq), F32),
            pltpu.VMEM((2, NSA_HEADS, NSA_DH + den_rows, tq), F32),
            pltpu.VMEM((max_tiles * NSA_HEADS, tk, tq), F32),
            pltpu.VMEM((max_tiles * NSA_HEADS, 1, tq), F32),
            pltpu.VMEM((max_tiles, tk, tq), F32),
        ],
        compiler_params=_cparams("parallel", "arbitrary"),
        name="nsa",
    )(zt, zb3, zb3, zt, zt, selb, ocmpt, zgt)


def _mlstm_kernel(q_ref, k_ref, v_ref, o_ref, zg_ref, cw_ref, cb_ref, gb_ref, hg_ref, y_ref,
                  xbuf, c_st, n_st, m_st, *, nb):
    L = ML_CHUNK
    hist = xbuf.shape[1]
    @pl.when(pl.program_id(1) == 0)
    def _():
        xbuf[...] = jnp.zeros_like(xbuf)
        c_st[...] = jnp.zeros_like(c_st)
        n_st[...] = jnp.zeros_like(n_st)
        m_st[...] = jnp.zeros_like(m_st)

    row = lax.broadcasted_iota(jnp.int32, (L, L), 0)
    col = lax.broadcasted_iota(jnp.int32, (L, L), 1)
    tril = row >= col
    tril_f = jnp.where(tril, 1.0, 0.0)
    streams = [(bb, h) for bb in range(nb) for h in range(ML_HEADS)]

    row_l = lax.broadcasted_iota(jnp.int32, (L, ZG_WIDTH), 0)
    g = []
    for bb in range(nb):
        gates = zg_ref[bb] + gb_ref[...]
        bcum = jnp.dot(tril_f, _log_sigmoid(gates), preferred_element_type=F32,
                       precision=lax.Precision.HIGHEST)
        b_al = pltpu.roll(bcum, shift=ZG_WIDTH - ML_HEADS, axis=1)
        m_prev = m_st[bb, 0:1, :]
        b_end = b_al[L - 1:L, :]
        inter = b_al + m_prev
        wlog = b_end - b_al + gates
        m_new = jnp.maximum(b_end + m_prev, jnp.max(wlog, axis=0, keepdims=True))
        ws = jnp.exp(wlog - m_new)
        decay = jnp.exp(b_end + m_prev - m_new)
        m_st[bb, 0:1, :] = m_new
        r = gates - b_al
        cm = r
        d = 1
        while d < L:
            cm = jnp.maximum(cm, jnp.where(row_l >= d, pltpu.roll(cm, shift=d, axis=0), -jnp.inf))
            d *= 2
        m_t = jnp.maximum(inter, b_al + cm)
        iw = jnp.exp(inter - m_t)
        em = jnp.exp(-m_t)
        g.append((b_al, m_t, iw, em, ws, decay, r.T))

    sr = lax.broadcasted_iota(jnp.int32, (CONV_WIDTH * L, hist + L), 0)
    scol = lax.broadcasted_iota(jnp.int32, (CONV_WIDTH * L, hist + L), 1)
    shift = jnp.where(scol == hist - (CONV_WIDTH - 1) + sr % L + sr // L, 1.0, 0.0).astype(BF16)
    qks = []
    for bb in range(nb):
        halves = []
        for hf, x_ref in enumerate((q_ref, k_ref)):
            cur = x_ref[bb]
            ext = jnp.concatenate([xbuf[bb, :, hf * ML_W:(hf + 1) * ML_W], cur], axis=0)
            taps = jnp.dot(shift, ext, preferred_element_type=F32)
            conv = jnp.zeros((L, ML_W), F32) + cb_ref[:, hf * ML_W:(hf + 1) * ML_W]
            for j in range(CONV_WIDTH):
                conv = conv + taps[j * L:(j + 1) * L, :] * cw_ref[j:j + 1, hf * ML_W:(hf + 1) * ML_W]
            xbuf[bb, :, hf * ML_W:(hf + 1) * ML_W] = cur[L - hist:L, :]
            halves.append(_silu(conv))
        qks.append(jnp.concatenate(halves, axis=1))

    qs = {(bb, h): qks[bb][:, h * ML_DH:(h + 1) * ML_DH] for bb, h in streams}
    ks = {(bb, h): qks[bb][:, ML_W + h * ML_DH:ML_W + (h + 1) * ML_DH] * (ML_DH ** -0.5)
          for bb, h in streams}
    qbs = {s: qs[s].astype(BF16) for s in streams}
    vbs = {(bb, h): v_ref[bb, :, h * ML_DH:(h + 1) * ML_DH] for bb, h in streams}
    c_olds = {(bb, h): c_st[bb * ML_HEADS + h] for bb, h in streams}
    n_olds = {(bb, h): n_st[bb, h:h + 1, :] for bb, h in streams}
    s_qk = {s: lax.dot_general(qbs[s], ks[s].astype(BF16), _NT, preferred_element_type=F32)
            for s in streams}
    n_pad = jnp.zeros((ML_DH - 1, ML_DH), F32)
    cqs = {s: lax.dot_general(
        qbs[s], jnp.concatenate([c_olds[s], n_olds[s], n_pad], axis=0).astype(BF16), _NT,
        preferred_element_type=F32) for s in streams}
    ones_blk = jnp.ones((L, ML_DH), BF16)

    for bb, h in streams:
        c = ZG_IF + h
        ws, decay = g[bb][4], g[bb][5]
        kw = ks[bb, h] * ws[:, c:c + 1]
        upd = lax.dot_general(vbs[bb, h], kw.astype(BF16), _TN, preferred_element_type=F32)
        c_st[bb * ML_HEADS + h] = decay[:, c:c + 1] * c_olds[bb, h] + upd
        n_st[bb, h:h + 1, :] = decay[:, c:c + 1] * n_olds[bb, h] + jnp.sum(kw, axis=0, keepdims=True)

    nds = {}
    for bb, h in streams:
        c = ZG_IF + h
        b_al, m_t, _, _, _, _, r_t = g[bb]
        dlog = jnp.where(tril, b_al[:, c:c + 1] + r_t[c:c + 1, :], -jnp.inf)
        sqk = s_qk[bb, h] * jnp.exp(dlog - m_t[:, c:c + 1])
        v_aug = jnp.concatenate([vbs[bb, h], ones_blk], axis=1)
        nds[bb, h] = jnp.dot(sqk.astype(BF16), v_aug, preferred_element_type=F32)
    hss = {}
    for bb, h in streams:
        c = ZG_IF + h
        iw_c = g[bb][2][:, c:c + 1]
        nd, cq = nds[bb, h], cqs[bb, h]
        num = nd[:, 0:ML_DH] + iw_c * cq[:, 0:ML_DH]
        den = nd[:, ML_DH:ML_DH + 1] + iw_c * cq[:, ML_DH:ML_DH + 1]
        hss[bb, h] = num / jnp.maximum(jnp.abs(den), g[bb][3][:, c:c + 1])
    mss = {s: jnp.mean(hss[s] * hss[s], axis=-1, keepdims=True) for s in streams}
    for bb, h in streams:
        hn = hss[bb, h] * lax.rsqrt(mss[bb, h] + EPS) * hg_ref[:, h * ML_DH:(h + 1) * ML_DH]
        og = _sigmoid(o_ref[bb, :, h * ML_DH:(h + 1) * ML_DH].astype(F32))
        y_ref[bb, :, h * ML_DH:(h + 1) * ML_DH] = (og * hn).astype(BF16)


def _mlstm(zb3, zg3, conv_w, conv_b, gate_b, head_g, nb=2):
    b, s, _ = zb3.shape
    L = ML_CHUNK
    qb = ZB_QKVML // ML_W
    assert b % nb == 0

    def zspec(k):
        return pl.BlockSpec((nb, L, ML_W), lambda bi, c, k=k: (bi, c, k))

    return pl.pallas_call(
        functools.partial(_mlstm_kernel, nb=nb),
        grid=(b // nb, s // L),
        in_specs=[
            zspec(qb), zspec(qb + 1), zspec(qb + 2), zspec(ZB_OML // ML_W),
            pl.BlockSpec((nb, L, ZG_WIDTH), lambda bi, c: (bi, c, 0)),
            pl.BlockSpec((CONV_WIDTH, 2 * ML_W), lambda bi, c: (0, 0)),
            pl.BlockSpec((1, 2 * ML_W), lambda bi, c: (0, 0)),
            pl.BlockSpec((1, ZG_WIDTH), lambda bi, c: (0, 0)),
            pl.BlockSpec((1, ML_W), lambda bi, c: (0, 0)),
        ],
        out_specs=pl.BlockSpec((nb, L, ML_W), lambda bi, c: (bi, c, 0)),
        out_shape=jax.ShapeDtypeStruct((b, s, ML_W), BF16),
        scratch_shapes=[
            pltpu.VMEM((nb, 16, 2 * ML_W), BF16),
            pltpu.VMEM((nb * ML_HEADS, ML_DH, ML_DH), F32),
            pltpu.VMEM((nb, 8, ML_DH), F32),
            pltpu.VMEM((nb, 8, LANE), F32),
        ],
        compiler_params=_cparams("parallel", "arbitrary"),
        name="mlstm",
    )(zb3, zb3, zb3, zb3, zg3, conv_w, conv_b, gate_b, head_g)


def _memkv_kernel(mem_ref, g_ref, w_ref, kv_ref):
    hb = _rms(mem_ref[...], g_ref[...]).astype(BF16)
    kv_ref[...] = jnp.dot(hb, w_ref[...], preferred_element_type=F32).astype(BF16)


def _memkv(mem, g, wb):
    b, m, _ = mem.shape
    return pl.pallas_call(
        _memkv_kernel,
        grid=(b,),
        in_specs=[
            pl.BlockSpec((None, m, D_MODEL), lambda i: (i, 0, 0)),
            pl.BlockSpec((1, D_MODEL), lambda i: (0, 0)),
            pl.BlockSpec((D_MODEL, 2 * MEM_W), lambda i: (0, 0)),
        ],
        out_specs=pl.BlockSpec((None, m, 2 * MEM_W), lambda i: (i, 0, 0)),
        out_shape=jax.ShapeDtypeStruct((b, m, 2 * MEM_W), BF16),
        compiler_params=_cparams("parallel"),
        name="memkv",
    )(mem, g, wb)


def _memattn_kernel(q_ref, kv_ref, y_ref):
    c = MEM_SCALE * LOG2E
    scores = [lax.dot_general(q_ref[:, h * MEM_DH:(h + 1) * MEM_DH], kv_ref[:, h * MEM_DH:(h + 1) * MEM_DH],
                              _NT, preferred_element_type=F32) for h in range(MEM_HEADS)]
    for h in range(MEM_HEADS):
        s = scores[h]
        v = kv_ref[:, MEM_W + h * MEM_DH:MEM_W + (h + 1) * MEM_DH]
        e = jnp.exp2((s - jnp.max(s, axis=-1, keepdims=True)) * c)
        l = jnp.sum(e, axis=-1, keepdims=True)
        o = jnp.dot(e.astype(BF16), v, preferred_element_type=F32) * (1.0 / l)
        y_ref[:, h * MEM_DH:(h + 1) * MEM_DH] = o.astype(BF16)


def _memattn(zb3, kvm, tq=512):
    b, s, _ = zb3.shape
    m = kvm.shape[1]
    return pl.pallas_call(
        _memattn_kernel,
        grid=(b, s // tq),
        in_specs=[
            pl.BlockSpec((None, tq, MEM_W), lambda i, j: (i, j, ZB_QMEM // MEM_W)),
            pl.BlockSpec((None, m, 2 * MEM_W), lambda i, j: (i, 0, 0)),
        ],
        out_specs=pl.BlockSpec((None, tq, MEM_W), lambda i, j: (i, j, 0)),
        out_shape=jax.ShapeDtypeStruct((b, s, MEM_W), BF16),
        compiler_params=_cparams("parallel", "parallel"),
        name="memattn",
    )(zb3, kvm)


def _merge_kernel(x_ref, yn_ref, yl_ref, ym_ref, g0_ref, g1_ref, g2_ref,
                  wn_ref, wl_ref, wm_ref, wo_ref, gp_ref, out_ref):
    y = _sigmoid(g0_ref[...].astype(F32)) * jnp.dot(yn_ref[...], wn_ref[...], preferred_element_type=F32)
    y = y + _sigmoid(g1_ref[...].astype(F32)) * jnp.dot(yl_ref[...], wl_ref[...], preferred_element_type=F32)
    y = y + _sigmoid(g2_ref[...].astype(F32)) * jnp.dot(ym_ref[...], wm_ref[...], preferred_element_type=F32)
    u = jnp.dot(y.astype(BF16), wo_ref[...], preferred_element_type=F32)
    out_ref[...] = x_ref[...] + _rms(u, gp_ref[...])


def _merge(x2, yn, yl, ym, zb, wn, wl, wm, wo, gp, tm=512):
    t = x2.shape[0]
    gm = ZB_GMERGE // D_MODEL

    def const(shape):
        return pl.BlockSpec(shape, lambda i: (0, 0), pipeline_mode=pl.Buffered(1))

    return pl.pallas_call(
        _merge_kernel,
        grid=(t // tm,),
        in_specs=[
            pl.BlockSpec((tm, D_MODEL), lambda i: (i, 0)),
            pl.BlockSpec((tm, NSA_Q), lambda i: (i, 0)),
            pl.BlockSpec((tm, ML_W), lambda i: (i, 0)),
            pl.BlockSpec((tm, MEM_W), lambda i: (i, 0)),
            pl.BlockSpec((tm, D_MODEL), lambda i: (i, gm)),
            pl.BlockSpec((tm, D_MODEL), lambda i: (i, gm + 1)),
            pl.BlockSpec((tm, D_MODEL), lambda i: (i, gm + 2)),
            const((NSA_Q, D_MODEL)), const((ML_W, D_MODEL)), const((MEM_W, D_MODEL)),
            const((D_MODEL, D_MODEL)), const((1, D_MODEL)),
        ],
        out_specs=pl.BlockSpec((tm, D_MODEL), lambda i: (i, 0)),
        out_shape=jax.ShapeDtypeStruct((t, D_MODEL), F32),
        compiler_params=_cparams("parallel"),
        name="merge",
    )(x2, yn, yl, ym, zb, zb, zb, wn, wl, wm, wo, gp)


def _ffn_kernel(x_ref, gpre_ref, wg_ref, wu_ref, wd_ref, gpost_ref, out_ref, h_ref, acc_ref):
    j = pl.program_id(1)

    @pl.when(j == 0)
    def _():
        h_ref[...] = _rms(x_ref[...], gpre_ref[...]).astype(BF16)
        acc_ref[...] = jnp.zeros_like(acc_ref)

    h = h_ref[...]
    gate = jnp.dot(h, wg_ref[...], preferred_element_type=F32)
    up = jnp.dot(h, wu_ref[...], preferred_element_type=F32)
    act = (_silu(gate) * up).astype(BF16)
    acc_ref[...] += jnp.dot(act, wd_ref[...], preferred_element_type=F32)

    @pl.when(j == pl.num_programs(1) - 1)
    def _():
        out_ref[...] = x_ref[...] + _rms(acc_ref[...], gpost_ref[...])


def _ffn(x2, gpre, w_in, w_down, gpost, tm=512, nf=1):
    t = x2.shape[0]
    tf = D_FF // nf
    assert tf % LANE == 0
    wmode = dict(pipeline_mode=pl.Buffered(1)) if nf == 1 else {}
    return pl.pallas_call(
        _ffn_kernel,
        grid=(t // tm, nf),
        in_specs=[
            pl.BlockSpec((tm, D_MODEL), lambda i, j: (i, 0)),
            pl.BlockSpec((1, D_MODEL), lambda i, j: (0, 0)),
            pl.BlockSpec((D_MODEL, tf), lambda i, j: (0, j), **wmode),
            pl.BlockSpec((D_MODEL, tf), lambda i, j: (0, nf + j), **wmode),
            pl.BlockSpec((tf, D_MODEL), lambda i, j: (j, 0), **wmode),
            pl.BlockSpec((1, D_MODEL), lambda i, j: (0, 0)),
        ],
        out_specs=pl.BlockSpec((tm, D_MODEL), lambda i, j: (i, 0)),
        out_shape=jax.ShapeDtypeStruct((t, D_MODEL), F32),
        scratch_shapes=[pltpu.VMEM((tm, D_MODEL), BF16), pltpu.VMEM((tm, D_MODEL), F32)],
        compiler_params=_cparams("parallel", "arbitrary"),
        name="ffn",
    )(x2, gpre, w_in, w_in, w_down, gpost)


def _layer(x, mem, p):
    b, s, _ = x.shape
    t = b * s
    x2 = x.reshape(t, D_MODEL)

    wn, wt, wc = _wprep(p["w_in"].T)
    zt, zb, zg, zgt, kvc = _inproj(x2, p["g_pre_mix"].reshape(1, D_MODEL), wt, wn, wc, b, s)
    zb3 = zb.reshape(b, s, ZB_WIDTH)
    zg3 = zg.reshape(b, s, ZG_WIDTH)

    half = (CMP_BLOCK // 2) * NSA_DH
    pe = jnp.stack([p["cmp_pe_k"], p["cmp_pe_v"]]).reshape(2, 1, 2 * half)
    pe = jnp.pad(pe, ((0, 0), (0, 7), (0, 0))).astype(BF16)
    w1 = jnp.stack([p["cmp_w1_k"], p["cmp_w1_v"]])
    eye2 = jnp.eye(2, dtype=F32)
    wpa = jnp.einsum("ab,kldh->kladbh", eye2, w1[:, :CMP_STRIDE]).reshape(2, CMP_STRIDE, LANE, 2 * CMP_HIDDEN)
    wpb = jnp.einsum("ab,kldh->kladbh", eye2, w1[:, CMP_STRIDE:]).reshape(2, CMP_STRIDE, LANE, 2 * CMP_HIDDEN)
    wp = jnp.concatenate([wpa, wpb], axis=-1).astype(BF16)
    kc, vct = _compress(kvc.reshape(b, s, KVC_WIDTH), pe, w1.reshape(2, 2 * half, CMP_HIDDEN).astype(BF16), wp,
                        p["cmp_w2_k"].astype(BF16), p["cmp_w2_v"].T.astype(BF16))
    ocmpt, selb = _cmpsel(zt, kc, vct)
    y_nsa = _nsa(zt, zb3, selb, ocmpt, zgt)

    gate_b = jnp.zeros((1, ZG_WIDTH), F32).at[0, ZG_IF:ZG_IF + 2 * ML_HEADS].set(p["ml_gate_b"])
    y_ml = _mlstm(zb3, zg3, p["ml_conv_w"], p["ml_conv_b"].reshape(1, 2 * ML_W), gate_b,
                  p["ml_head_g"].reshape(1, ML_W))

    kvm = _memkv(mem, p["g_mem"].reshape(1, D_MODEL), p["w_mem_kv"].astype(BF16))
    y_mem = _memattn(zb3, kvm)

    x1 = _merge(x2, y_nsa.reshape(t, NSA_Q), y_ml.reshape(t, ML_W), y_mem.reshape(t, MEM_W), zb,
                p["w_proj_nsa"].astype(BF16), p["w_proj_ml"].astype(BF16),
                p["w_proj_mem"].astype(BF16), p["w_out"].astype(BF16),
                p["g_post_mix"].reshape(1, D_MODEL))
    x2o = _ffn(x1, p["g_pre_ffn"].reshape(1, D_MODEL), p["w_ffn_in"].astype(BF16),
               p["w_ffn_down"].astype(BF16), p["g_post_ffn"].reshape(1, D_MODEL))
    return x2o.reshape(b, s, D_MODEL)


def kernel(x, mem, g_pre_mix, w_in, cmp_pe_k, cmp_w1_k, cmp_w2_k, cmp_pe_v, cmp_w1_v, cmp_w2_v,
           ml_conv_w, ml_conv_b, ml_gate_b, ml_head_g, g_mem, w_mem_kv, w_proj_nsa, w_proj_ml,
           w_proj_mem, w_out, g_post_mix, g_pre_ffn, w_ffn_in, w_ffn_down, g_post_ffn):
    params = dict(
        g_pre_mix=g_pre_mix, w_in=w_in, cmp_pe_k=cmp_pe_k, cmp_w1_k=cmp_w1_k, cmp_w2_k=cmp_w2_k,
        cmp_pe_v=cmp_pe_v, cmp_w1_v=cmp_w1_v, cmp_w2_v=cmp_w2_v, ml_conv_w=ml_conv_w,
        ml_conv_b=ml_conv_b, ml_gate_b=ml_gate_b, ml_head_g=ml_head_g, g_mem=g_mem,
        w_mem_kv=w_mem_kv, w_proj_nsa=w_proj_nsa, w_proj_ml=w_proj_ml, w_proj_mem=w_proj_mem,
        w_out=w_out, g_post_mix=g_post_mix, g_pre_ffn=g_pre_ffn, w_ffn_in=w_ffn_in,
        w_ffn_down=w_ffn_down, g_post_ffn=g_post_ffn)
    depth = w_in.shape[0]
    for l in range(depth):
        x = _layer(x, mem, {k: v[l] for k, v in params.items()})
    return x
```

```python
import functools
import math

import jax
import jax.numpy as jnp
from jax import lax
from jax.experimental import pallas as pl
from jax.experimental.pallas import tpu as pltpu

F32 = jnp.float32
BF16 = jnp.bfloat16

D_MODEL = 1024
EPS = 1e-6
NEG_INF = -1e30
BIG = 1e30

NSA_HEADS = 16
NSA_GROUPS = 4
NSA_HPG = NSA_HEADS // NSA_GROUPS
NSA_DH = 64
NSA_SCALE = NSA_DH ** -0.5
CMP_BLOCK = 32
CMP_STRIDE = 16
CMP_HIDDEN = 128
SEL_BLOCK = 64
SEL_TOPK = 16
N_LOCAL_FORCED = 2
WINDOW = 512

ML_HEADS = 4
ML_DH = 128
ML_CHUNK = 128
CONV_WIDTH = 4

MEM_HEADS = 4
MEM_DH = 128
MEM_SCALE = MEM_DH ** -0.5

D_FF = -(-8 * D_MODEL // (3 * 256)) * 256

NSA_Q = NSA_HEADS * NSA_DH
NSA_KV = NSA_GROUPS * NSA_DH
ML_W = ML_HEADS * ML_DH
MEM_W = MEM_HEADS * MEM_DH

_OFF_Q = 0
_OFF_KV = _OFF_Q + NSA_Q
_OFF_GNSA = _OFF_KV + 6 * NSA_KV
_OFF_QKVML = _OFF_GNSA + 3 * NSA_HEADS
_OFF_IF = _OFF_QKVML + 3 * ML_W
_OFF_OML = _OFF_IF + 2 * ML_HEADS
_OFF_QMEM = _OFF_OML + ML_W
_OFF_GMERGE = _OFF_QMEM + MEM_W
_IN_WIDTH = _OFF_GMERGE + 3 * D_MODEL

ZB_GMERGE = 0
ZB_KSLC = ZB_GMERGE + 3 * D_MODEL
ZB_KWIN = ZB_KSLC + NSA_KV
ZB_QKVML = ZB_KWIN + NSA_KV
ZB_OML = ZB_QKVML + 3 * ML_W
ZB_QMEM = ZB_OML + ML_W
ZB_WIDTH = ZB_QMEM + MEM_W
KVC_WIDTH = 2 * NSA_KV
ZT_Q = 0
ZT_VSLC = ZT_Q + NSA_Q
ZT_VWIN = ZT_VSLC + NSA_KV
ZT_ROWS = ZT_VWIN + NSA_KV
ZG_WIDTH = 128
ZG_GNSA = 0
ZG_IF = 3 * NSA_HEADS

LANE = 128
SLAB_ROWS = 8
VMEM_LIMIT = 56 * 1024 * 1024
LOG2E = math.log2(math.e)

_NT = (((1,), (1,)), ((), ()))
_TN = (((0,), (0,)), ((), ()))


def _cparams(*sem):
    return pltpu.CompilerParams(dimension_semantics=sem, vmem_limit_bytes=VMEM_LIMIT)


def _rms(x, g):
    return x * lax.rsqrt(jnp.mean(x * x, axis=-1, keepdims=True) + EPS) * g


def _sigmoid(x):
    return 1.0 / (1.0 + jnp.exp(-x))


def _silu(x):
    return x * _sigmoid(x)


def _log_sigmoid(x):
    return jnp.minimum(x, 0.0) - jnp.log(1.0 + jnp.exp(-jnp.abs(x)))


def _wprep_kernel(w_ref, wn_ref, wt_ref, wc_ref):
    def rows(lo, hi):
        return w_ref[lo:hi, :]

    def kv_rows(k):
        return rows(_OFF_KV + k * NSA_KV, _OFF_KV + (k + 1) * NSA_KV)

    wn_ref[ZB_GMERGE:ZB_KSLC, :] = rows(_OFF_GMERGE, _IN_WIDTH).astype(BF16)
    wn_ref[ZB_KSLC:ZB_KWIN, :] = kv_rows(2).astype(BF16)
    wn_ref[ZB_KWIN:ZB_QKVML, :] = kv_rows(4).astype(BF16)
    wn_ref[ZB_QKVML:ZB_OML, :] = rows(_OFF_QKVML, _OFF_IF).astype(BF16)
    wn_ref[ZB_OML:ZB_WIDTH, :] = rows(_OFF_OML, _OFF_GMERGE).astype(BF16)
    wt_ref[ZT_Q:ZT_VSLC, :] = (rows(_OFF_Q, _OFF_KV) * (NSA_SCALE * LOG2E)).astype(BF16)
    wt_ref[ZT_VSLC:ZT_VWIN, :] = kv_rows(3).astype(BF16)
    wt_ref[ZT_VWIN:ZT_ROWS, :] = kv_rows(5).astype(BF16)
    n_g = 3 * NSA_HEADS
    n_if = 2 * ML_HEADS
    wt_ref[ZT_ROWS:ZT_ROWS + ZG_WIDTH, :] = jnp.zeros((ZG_WIDTH, wt_ref.shape[1]), BF16)
    wt_ref[ZT_ROWS + ZG_GNSA:ZT_ROWS + ZG_GNSA + n_g, :] = rows(_OFF_GNSA, _OFF_QKVML).astype(BF16)
    wt_ref[ZT_ROWS + ZG_IF:ZT_ROWS + ZG_IF + n_if, :] = rows(_OFF_IF, _OFF_OML).astype(BF16)
    wc_ref[...] = rows(_OFF_KV, _OFF_KV + KVC_WIDTH).astype(BF16)


def _wprep(w_t, tl=128):
    assert ZB_QMEM == ZB_OML + ML_W and _OFF_QMEM == _OFF_OML + ML_W
    return pl.pallas_call(
        _wprep_kernel,
        grid=(D_MODEL // tl,),
        in_specs=[pl.BlockSpec((_IN_WIDTH, tl), lambda i: (0, i))],
        out_specs=[
            pl.BlockSpec((ZB_WIDTH, tl), lambda i: (0, i)),
            pl.BlockSpec((ZT_ROWS + ZG_WIDTH, tl), lambda i: (0, i)),
            pl.BlockSpec((KVC_WIDTH, tl), lambda i: (0, i)),
        ],
        out_shape=[
            jax.ShapeDtypeStruct((ZB_WIDTH, D_MODEL), BF16),
            jax.ShapeDtypeStruct((ZT_ROWS + ZG_WIDTH, D_MODEL), BF16),
            jax.ShapeDtypeStruct((KVC_WIDTH, D_MODEL), BF16),
        ],
        compiler_params=_cparams("parallel"),
        name="wprep",
    )(w_t)


def _inproj_kernel(x_ref, g_ref, wt_ref, wn_ref, wc_ref,
                   zt_ref, zb_ref, zg_ref, zgt_ref, kvc_ref, h_ref, *, nt):
    j = pl.program_id(1)

    @pl.when(j == 0)
    def _():
        hb = _rms(x_ref[...], g_ref[...]).astype(BF16)
        h_ref[...] = hb
        kvc_ref[...] = lax.dot_general(hb, wc_ref[...], _NT, preferred_element_type=F32)
        zf = lax.dot_general(wt_ref[...], hb, _NT, preferred_element_type=F32)
        zt_ref[...] = zf[0:ZT_ROWS, :].astype(BF16)
        gt = zf[ZT_ROWS:ZT_ROWS + ZG_WIDTH, :]
        zgt_ref[...] = gt
        zg_ref[...] = gt.T

    @pl.when(j >= nt)
    def _():
        tn = zb_ref.shape[1]
        r0 = pl.multiple_of((j - nt) * tn, tn)
        zb_ref[...] = lax.dot_general(h_ref[...], wn_ref[pl.ds(r0, tn), :], _NT,
                                      preferred_element_type=F32).astype(BF16)


def _inproj(x2, g, wt, wn, wc, b, s, tm=1024, nn=3):
    t = x2.shape[0]
    nt = 1
    tn = ZB_WIDTH // nn
    assert tn * nn == ZB_WIDTH and tn % LANE == 0 and s % tm == 0
    spb = s // tm
    kern = functools.partial(_inproj_kernel, nt=nt)
    once = dict(pipeline_mode=pl.Buffered(1))
    return pl.pallas_call(
        kern,
        grid=(t // tm, nt + nn),
        in_specs=[
            pl.BlockSpec((tm, D_MODEL), lambda i, j: (i, 0)),
            pl.BlockSpec((1, D_MODEL), lambda i, j: (0, 0)),
            pl.BlockSpec((ZT_ROWS + ZG_WIDTH, D_MODEL), lambda i, j: (0, 0), **once),
            pl.BlockSpec((ZB_WIDTH, D_MODEL), lambda i, j: (0, 0), **once),
            pl.BlockSpec((KVC_WIDTH, D_MODEL), lambda i, j: (0, 0), **once),
        ],
        out_specs=[
            pl.BlockSpec((None, ZT_ROWS, tm), lambda i, j: (i // spb, 0, i % spb)),
            pl.BlockSpec((tm, tn), lambda i, j: (i, jnp.maximum(j - nt, 0))),
            pl.BlockSpec((tm, ZG_WIDTH), lambda i, j: (i, 0)),
            pl.BlockSpec((None, ZG_WIDTH, tm), lambda i, j: (i // spb, 0, i % spb)),
            pl.BlockSpec((tm, KVC_WIDTH), lambda i, j: (i, 0)),
        ],
        out_shape=[
            jax.ShapeDtypeStruct((b, ZT_ROWS, s), BF16),
            jax.ShapeDtypeStruct((t, ZB_WIDTH), BF16),
            jax.ShapeDtypeStruct((t, ZG_WIDTH), F32),
            jax.ShapeDtypeStruct((b, ZG_WIDTH, s), F32),
            jax.ShapeDtypeStruct((t, KVC_WIDTH), F32),
        ],
        scratch_shapes=[pltpu.VMEM((tm, D_MODEL), BF16)],
        compiler_params=_cparams("parallel", "arbitrary"),
        name="inproj",
    )(x2, g, wt, wn, wc)


def _compress_kernel(a0_ref, a1_ref, a2_ref, a3_ref, pe_ref, w1_ref, w2k_ref, w2vt_ref,
                     kc_ref, vct_ref):
    a_refs = (a0_ref, a1_ref, a2_ref, a3_ref)
    n_rows = a0_ref.shape[0] // CMP_STRIDE
    half = CMP_STRIDE * NSA_DH
    zero = jnp.zeros((NSA_DH, CMP_HIDDEN), BF16)
    for kind in range(2):
        pe = pe_ref[kind]
        c = jnp.dot(pe, w1_ref[kind], preferred_element_type=F32)[0:1, :]
        wps = []
        for l in range(CMP_STRIDE):
            wa = w1_ref[kind, l * NSA_DH:(l + 1) * NSA_DH, :]
            wb = w1_ref[kind, half + l * NSA_DH:half + (l + 1) * NSA_DH, :]
            wps.append(jnp.concatenate([jnp.concatenate([wa, zero, wb, zero], axis=1),
                                        jnp.concatenate([zero, wa, zero, wb], axis=1)], axis=0))
        for pair in range(NSA_GROUPS // 2):
            a_ref = a_refs[kind * 2 + pair]
            acc = jnp.zeros((n_rows, 4 * CMP_HIDDEN), F32)
            for l in range(CMP_STRIDE):
                rows = a_ref[pl.ds(l, n_rows, stride=CMP_STRIDE), :].astype(BF16)
                acc = acc + jnp.dot(rows, wps[l], preferred_element_type=F32)
            for gi in range(2):
                g = 2 * pair + gi
                p = acc[:, gi * CMP_HIDDEN:(gi + 1) * CMP_HIDDEN]
                q = acc[:, (2 + gi) * CMP_HIDDEN:(3 + gi) * CMP_HIDDEN]
                qs = pltpu.roll(q, shift=n_rows - 1, axis=0)
                hid = _silu(p + qs + c).astype(BF16)
                if kind == 0:
                    o = jnp.dot(hid, w2k_ref[...], preferred_element_type=F32)
                    kc_ref[:, g * NSA_DH:(g + 1) * NSA_DH] = o.astype(BF16)
                else:
                    ot = lax.dot_general(w2vt_ref[...], hid, _NT, preferred_element_type=F32)
                    vct_ref[g * NSA_DH:(g + 1) * NSA_DH, :] = ot.astype(BF16)


def _compress(kvc3, pe, w1, w2k, w2vt):
    b, s, _ = kvc3.shape
    n_rows = s // CMP_STRIDE
    half = (CMP_BLOCK // 2) * NSA_DH

    def col(k):
        return pl.BlockSpec((None, s, LANE), lambda i, k=k: (i, 0, k))

    return pl.pallas_call(
        _compress_kernel,
        grid=(b,),
        in_specs=[
            col(0), col(1), col(2), col(3),
            pl.BlockSpec((2, 8, 2 * half), lambda i: (0, 0, 0)),
            pl.BlockSpec((2, 2 * half, CMP_HIDDEN), lambda i: (0, 0, 0)),
            pl.BlockSpec((CMP_HIDDEN, NSA_DH), lambda i: (0, 0)),
            pl.BlockSpec((NSA_DH, CMP_HIDDEN), lambda i: (0, 0)),
        ],
        out_specs=[
            pl.BlockSpec((None, n_rows, NSA_KV), lambda i: (i, 0, 0)),
            pl.BlockSpec((None, NSA_KV, n_rows), lambda i: (i, 0, 0)),
        ],
        out_shape=[
            jax.ShapeDtypeStruct((b, n_rows, NSA_KV), BF16),
            jax.ShapeDtypeStruct((b, NSA_KV, n_rows), BF16),
        ],
        compiler_params=_cparams("parallel"),
        name="compress",
    )(kvc3, kvc3, kvc3, kvc3, pe, w1, w2k, w2vt)


def _pad_query(qt, g):
    z = jnp.zeros_like(qt)
    return jnp.concatenate([qt, z] if g % 2 == 0 else [z, qt], axis=0)


def _cmpsel_kernel(qt_ref, kc_ref, vct_ref, ocmpt_ref, selb_ref, *, tq, n_sel, topk):
    nc = kc_ref.shape[0]
    t0 = pl.program_id(1) * tq

    n_t = lax.broadcasted_iota(jnp.int32, (nc, tq), 0)
    t_t = t0 + lax.broadcasted_iota(jnp.int32, (nc, tq), 1)
    mask_t = (n_t * CMP_STRIDE + (CMP_BLOCK - 1)) <= t_t

    j_m = lax.broadcasted_iota(jnp.int32, (n_sel, nc), 0) * SEL_BLOCK
    c_m = lax.broadcasted_iota(jnp.int32, (n_sel, nc), 1) * CMP_STRIDE
    ov = jnp.minimum(c_m + CMP_BLOCK, j_m + SEL_BLOCK) - jnp.maximum(c_m, j_m)
    map_t = jnp.maximum(ov, 0).astype(F32) * (1.0 / CMP_BLOCK)

    j_s = lax.broadcasted_iota(jnp.int32, (n_sel, tq), 0)
    qblk = (t0 + lax.broadcasted_iota(jnp.int32, (n_sel, tq), 1)) // SEL_BLOCK
    rel = qblk - j_s
    causal = rel >= 0
    forced = causal & ((j_s == 0) | (rel < N_LOCAL_FORCED))

    def scores(g):
        kcp = kc_ref[:, (g // 2) * LANE:(g // 2 + 1) * LANE]
        return [jnp.dot(kcp, _pad_query(qt_ref[(g * NSA_HPG + h) * NSA_DH:(g * NSA_HPG + h + 1) * NSA_DH, :], g),
                        preferred_element_type=F32) for h in range(NSA_HPG)]

    sts_next = scores(0)
    for g in range(NSA_GROUPS):
        sts = sts_next
        if g + 1 < NSA_GROUPS:
            sts_next = scores(g + 1)
        vct = vct_ref[g * NSA_DH:(g + 1) * NSA_DH, :]
        masked = [jnp.where(mask_t, st, NEG_INF) for st in sts]
        ets = [jnp.where(mask_t, jnp.exp2(st - jnp.max(st, axis=0, keepdims=True)), 0.0) for st in masked]
        lts = [jnp.sum(et, axis=0, keepdims=True) for et in ets]
        pts = [et * (1.0 / jnp.where(lt > 0.0, lt, 1.0)) for et, lt in zip(ets, lts)]
        for h in range(NSA_HPG):
            hh = g * NSA_HPG + h
            ot = jnp.dot(vct, pts[h].astype(BF16), preferred_element_type=F32)
            ocmpt_ref[hh * NSA_DH:(hh + 1) * NSA_DH, :] = ot.astype(BF16)
        psum_t = (pts[0] + pts[1]) + (pts[2] + pts[3])
        imp_t = jnp.dot(map_t, psum_t, preferred_element_type=F32,
                        precision=lax.Precision.HIGHEST)
        score = jnp.where(forced, BIG, jnp.where(causal, imp_t, NEG_INF))
        rank = jnp.zeros((n_sel, tq), jnp.int32)
        for jp in range(n_sel):
            row = score[jp:jp + 1, :]
            before = (row > score) | ((row == score) & (j_s > jp))
            rank = rank + before.astype(jnp.int32)
        chosen = (rank < topk) & (score > 0.5 * NEG_INF)
        selb_ref[g * n_sel:(g + 1) * n_sel, :] = jnp.where(chosen, 0.0, NEG_INF)


def _cmpsel(zt, kc, vct, tq=256):
    b, _, s = zt.shape
    n_sel = s // SEL_BLOCK
    topk = min(SEL_TOPK, n_sel)
    nc = kc.shape[1]
    kern = functools.partial(_cmpsel_kernel, tq=tq, n_sel=n_sel, topk=topk)
    return pl.pallas_call(
        kern,
        grid=(b, s // tq),
        in_specs=[
            pl.BlockSpec((None, NSA_Q, tq), lambda i, j: (i, ZT_Q // NSA_Q, j)),
            pl.BlockSpec((None, nc, NSA_KV), lambda i, j: (i, 0, 0)),
            pl.BlockSpec((None, NSA_KV, nc), lambda i, j: (i, 0, 0)),
        ],
        out_specs=[
            pl.BlockSpec((None, NSA_Q, tq), lambda i, j: (i, 0, j)),
            pl.BlockSpec((None, NSA_GROUPS * n_sel, tq), lambda i, j: (i, 0, j)),
        ],
        out_shape=[
            jax.ShapeDtypeStruct((b, NSA_Q, s), BF16),
            jax.ShapeDtypeStruct((b, NSA_GROUPS * n_sel, s), F32),
        ],
        compiler_params=_cparams("parallel", "parallel"),
        name="cmpsel",
    )(zt, kc, vct)


def _nsa_kernel(qt_ref, ks_ref, kw_ref, vst_ref, vwt_ref, selb_ref, ocmpt_ref, gt_ref, y_ref,
                yt_scr, qp_scr, m_scr, acc_scr, st_scr, mx_scr, bias_scr, *, tq, n_sel, look):
    tk = tq
    nw = WINDOW // tk
    sel_per_tile = tk // SEL_BLOCK
    den_rows = 16
    i = pl.program_id(1)
    d0 = (lax.broadcasted_iota(jnp.int32, (tk, tq), 1)
          - lax.broadcasted_iota(jnp.int32, (tk, tq), 0))
    caus = jnp.where(d0 >= 0, 0.0, NEG_INF)
    lowb = jnp.where(d0 < 0, 0.0, NEG_INF)
    ones_rows = jnp.ones((den_rows, tk), BF16)
    ext = 16
    key_blk = lax.broadcasted_iota(jnp.int32, (tk, ext), 0) // SEL_BLOCK
    ext_col = lax.broadcasted_iota(jnp.int32, (tk, ext), 1)
    gates = _sigmoid(gt_ref[...])

    for hh in range(NSA_HEADS):
        qp_scr[hh] = _pad_query(qt_ref[hh * NSA_DH:(hh + 1) * NSA_DH, :], hh // NSA_HPG)
    m_scr[...] = jnp.full(m_scr.shape, NEG_INF, F32)
    acc_scr[...] = jnp.zeros(acc_scr.shape, F32)

    SLC = (0, ks_ref, vst_ref, True)
    WIN = (1, kw_ref, vwt_ref, False)

    def run(tiles):
        loaded = []
        for ti, ((br, k_ref, vt_ref, sel_on), kt, mask) in enumerate(tiles):
            r0 = pl.multiple_of(kt * tk, tk)
            ktiles = [k_ref[pl.ds(r0, tk), pair * LANE:(pair + 1) * LANE]
                      for pair in range(NSA_GROUPS // 2)]
            vt_augs = [jnp.concatenate([vt_ref[g * NSA_DH:(g + 1) * NSA_DH, pl.ds(r0, tk)], ones_rows],
                                       axis=0) for g in range(NSA_GROUPS)]
            slabs = None
            if sel_on:
                row0 = kt * sel_per_tile
                slab0 = pl.multiple_of((row0 // SLAB_ROWS) * SLAB_ROWS, SLAB_ROWS)
                hot = jnp.where(key_blk + row0 % SLAB_ROWS == ext_col, 1.0, 0.0).astype(BF16)
                ktiles = [jnp.concatenate([kp, hot], axis=1) for kp in ktiles]
                slabs = [jnp.concatenate(
                    [selb_ref[pl.ds(g * n_sel + slab0, SLAB_ROWS), :],
                     jnp.zeros((ext - SLAB_ROWS, tq), F32)], axis=0).astype(BF16)
                    for g in range(NSA_GROUPS)]
            loaded.append((ktiles, vt_augs, slabs))
            if mask is not None:
                bias_scr[ti] = mask

        def qk(ti, hh):
            g = hh // NSA_HPG
            ktiles, _, slabs = loaded[ti]
            q = qp_scr[hh]
            if slabs is not None:
                q = jnp.concatenate([q, slabs[g]], axis=0)
            st = jnp.dot(ktiles[g // 2], q, preferred_element_type=F32)
            if tiles[ti][2] is not None:
                st = st + bias_scr[ti]
            st_scr[ti * NSA_HEADS + hh] = st
            mx_scr[ti * NSA_HEADS + hh] = jnp.max(st, axis=0, keepdims=True)

        def softmax_pv(ti, hh):
            br = tiles[ti][0][0]
            vt_aug = loaded[ti][1][hh // NSA_HPG]
            m = m_scr[br, hh]
            m_new = jnp.maximum(m, mx_scr[ti * NSA_HEADS + hh])
            alpha = jnp.exp2(m - m_new)
            p = jnp.exp2(st_scr[ti * NSA_HEADS + hh] - m_new).astype(BF16)
            acc_scr[br, hh] = alpha * acc_scr[br, hh] + jnp.dot(vt_aug, p, preferred_element_type=F32)
            m_scr[br, hh] = m_new

        seq = [(ti, hh) for ti in range(len(tiles)) for hh in range(NSA_HEADS)]
        for pos in range(min(look, len(seq))):
            qk(*seq[pos])
        for pos, item in enumerate(seq):
            if pos + look < len(seq):
                qk(*seq[pos + look])
            softmax_pv(*item)

    def slc_body(j, c):
        run([(SLC, 2 * j, None), (SLC, 2 * j + 1, None)])
        return c

    lax.fori_loop(0, i // 2, slc_body, 0)

    def tail(par, v):
        slc = [(SLC, i - 1, None)] * par + [(SLC, i, caus)]
        if v < nw:
            win = [(WIN, kt, None) for kt in range(v)] + [(WIN, i, caus)]
        else:
            win = ([(WIN, i - nw, lowb)] + [(WIN, i - nw + d, None) for d in range(1, nw)]
                   + [(WIN, i, caus)])
        run(slc + win)

    for v in range(nw):
        @pl.when(i == v)
        def _(v=v):
            tail(v % 2, v)

    for par in range(2):
        @pl.when((i >= nw) & (i % 2 == par))
        def _(par=par):
            tail(par, nw)

    for hh in range(NSA_HEADS):
        acc_s = acc_scr[0, hh]
        acc_w = acc_scr[1, hh]
        o_slc = acc_s[0:NSA_DH] * (1.0 / acc_s[NSA_DH:NSA_DH + 1])
        o_win = acc_w[0:NSA_DH] * (1.0 / acc_w[NSA_DH:NSA_DH + 1])
        o_cmp = ocmpt_ref[hh * NSA_DH:(hh + 1) * NSA_DH, :].astype(F32)
        gc = ZG_GNSA + 3 * hh
        yt_scr[hh * NSA_DH:(hh + 1) * NSA_DH, :] = (
            gates[gc:gc + 1] * o_cmp + gates[gc + 1:gc + 2] * o_slc + gates[gc + 2:gc + 3] * o_win)

    y_ref[...] = yt_scr[...].T.astype(BF16)


def _nsa(zt, zb3, selb, ocmpt, zgt, tq=256, look=5):
    b, _, s = zt.shape
    n_sel = s // SEL_BLOCK
    tk = tq
    den_rows = 16
    max_tiles = WINDOW // tk + 3
    assert WINDOW % tq == 0 and tq % SEL_BLOCK == 0 and SLAB_ROWS % (tk // SEL_BLOCK) == 0
    kern = functools.partial(_nsa_kernel, tq=tq, n_sel=n_sel, look=look)
    return pl.pallas_call(
        kern,
        grid=(b, s // tq),
        in_specs=[
            pl.BlockSpec((None, NSA_Q, tq), lambda bi, j: (bi, ZT_Q // NSA_Q, j)),
            pl.BlockSpec((None, s, NSA_KV), lambda bi, j: (bi, 0, ZB_KSLC // NSA_KV)),
            pl.BlockSpec((None, s, NSA_KV), lambda bi, j: (bi, 0, ZB_KWIN // NSA_KV)),
            pl.BlockSpec((None, NSA_KV, s), lambda bi, j: (bi, ZT_VSLC // NSA_KV, 0)),
            pl.BlockSpec((None, NSA_KV, s), lambda bi, j: (bi, ZT_VWIN // NSA_KV, 0)),
            pl.BlockSpec((None, NSA_GROUPS * n_sel, tq), lambda bi, j: (bi, 0, j)),
            pl.BlockSpec((None, NSA_Q, tq), lambda bi, j: (bi, 0, j)),
            pl.BlockSpec((None, ZG_WIDTH, tq), lambda bi, j: (bi, 0, j)),
        ],
        out_specs=pl.BlockSpec((None, tq, NSA_Q), lambda bi, j: (bi, j, 0)),
        out_shape=jax.ShapeDtypeStruct((b, s, NSA_Q), BF16),
        scratch_shapes=[
            pltpu.VMEM((NSA_Q, tq), F32),
            pltpu.VMEM((NSA_HEADS, 2 * NSA_DH, tq), BF16),
            pltpu.VMEM((2, NSA_HEADS, 1, tq), F32),
            pltpu.VMEM((2, NSA_HEADS, NSA_DH + den_rows, tq), F32),
            pltpu.VMEM((max_tiles * NSA_HEADS, tk, tq), F32),
            pltpu.VMEM((max_tiles * NSA_HEADS, 1, tq), F32),
            pltpu.VMEM((max_tiles, tk, tq), F32),
        ],
        compiler_params=_cparams("parallel", "arbitrary"),
        name="nsa",
    )(zt, zb3, zb3, zt, zt, selb, ocmpt, zgt)


def _mlstm_kernel(q_ref, k_ref, v_ref, o_ref, zg_ref, cw_ref, cb_ref, gb_ref, hg_ref, y_ref,
                  xbuf, c_st, n_st, m_st, *, nb):
    L = ML_CHUNK
    hist = xbuf.shape[1]
    @pl.when(pl.program_id(1) == 0)
    def _():
        xbuf[...] = jnp.zeros_like(xbuf)
        c_st[...] = jnp.zeros_like(c_st)
        n_st[...] = jnp.zeros_like(n_st)
        m_st[...] = jnp.zeros_like(m_st)

    row = lax.broadcasted_iota(jnp.int32, (L, L), 0)
    col = lax.broadcasted_iota(jnp.int32, (L, L), 1)
    tril = row >= col
    tril_f = jnp.where(tril, 1.0, 0.0)
    streams = [(bb, h) for bb in range(nb) for h in range(ML_HEADS)]

    row_l = lax.broadcasted_iota(jnp.int32, (L, ZG_WIDTH), 0)
    g = []
    for bb in range(nb):
        gates = zg_ref[bb] + gb_ref[...]
        bcum = jnp.dot(tril_f, _log_sigmoid(gates), preferred_element_type=F32,
                       precision=lax.Precision.HIGHEST)
        b_al = pltpu.roll(bcum, shift=ZG_WIDTH - ML_HEADS, axis=1)
        m_prev = m_st[bb, 0:1, :]
        b_end = b_al[L - 1:L, :]
        inter = b_al + m_prev
        wlog = b_end - b_al + gates
        m_new = jnp.maximum(b_end + m_prev, jnp.max(wlog, axis=0, keepdims=True))
        ws = jnp.exp(wlog - m_new)
        decay = jnp.exp(b_end + m_prev - m_new)
        m_st[bb, 0:1, :] = m_new
        r = gates - b_al
        cm = r
        d = 1
        while d < L:
            cm = jnp.maximum(cm, jnp.where(row_l >= d, pltpu.roll(cm, shift=d, axis=0), -jnp.inf))
            d *= 2
        m_t = jnp.maximum(inter, b_al + cm)
        iw = jnp.exp(inter - m_t)
        em = jnp.exp(-m_t)
        g.append((b_al, m_t, iw, em, ws, decay, r.T))

    sr = lax.broadcasted_iota(jnp.int32, (CONV_WIDTH * L, hist + L), 0)
    scol = lax.broadcasted_iota(jnp.int32, (CONV_WIDTH * L, hist + L), 1)
    shift = jnp.where(scol == hist - (CONV_WIDTH - 1) + sr % L + sr // L, 1.0, 0.0).astype(BF16)
    qks = []
    for bb in range(nb):
        halves = []
        for hf, x_ref in enumerate((q_ref, k_ref)):
            cur = x_ref[bb]
            ext = jnp.concatenate([xbuf[bb, :, hf * ML_W:(hf + 1) * ML_W], cur], axis=0)
            taps = jnp.dot(shift, ext, preferred_element_type=F32)
            conv = jnp.zeros((L, ML_W), F32) + cb_ref[:, hf * ML_W:(hf + 1) * ML_W]
            for j in range(CONV_WIDTH):
                conv = conv + taps[j * L:(j + 1) * L, :] * cw_ref[j:j + 1, hf * ML_W:(hf + 1) * ML_W]
            xbuf[bb, :, hf * ML_W:(hf + 1) * ML_W] = cur[L - hist:L, :]
            halves.append(_silu(conv))
        qks.append(jnp.concatenate(halves, axis=1))

    qs = {(bb, h): qks[bb][:, h * ML_DH:(h + 1) * ML_DH] for bb, h in streams}
    ks = {(bb, h): qks[bb][:, ML_W + h * ML_DH:ML_W + (h + 1) * ML_DH] * (ML_DH ** -0.5)
          for bb, h in streams}
    qbs = {s: qs[s].astype(BF16) for s in streams}
    vbs = {(bb, h): v_ref[bb, :, h * ML_DH:(h + 1) * ML_DH] for bb, h in streams}
    c_olds = {(bb, h): c_st[bb * ML_HEADS + h] for bb, h in streams}
    n_olds = {(bb, h): n_st[bb, h:h + 1, :] for bb, h in streams}
    s_qk = {s: lax.dot_general(qbs[s], ks[s].astype(BF16), _NT, preferred_element_type=F32)
            for s in streams}
    n_pad = jnp.zeros((ML_DH - 1, ML_DH), F32)
    cqs = {s: lax.dot_general(
        qbs[s], jnp.concatenate([c_olds[s], n_olds[s], n_pad], axis=0).astype(BF16), _NT,
        preferred_element_type=F32) for s in streams}
    ones_blk = jnp.ones((L, ML_DH), BF16)

    for bb, h in streams:
        c = ZG_IF + h
        ws, decay = g[bb][4], g[bb][5]
        kw = ks[bb, h] * ws[:, c:c + 1]
        upd = lax.dot_general(vbs[bb, h], kw.astype(BF16), _TN, preferred_element_type=F32)
        c_st[bb * ML_HEADS + h] = decay[:, c:c + 1] * c_olds[bb, h] + upd
        n_st[bb, h:h + 1, :] = decay[:, c:c + 1] * n_olds[bb, h] + jnp.sum(kw, axis=0, keepdims=True)

    nds = {}
    for bb, h in streams:
        c = ZG_IF + h
        b_al, m_t, _, _, _, _, r_t = g[bb]
        dlog = jnp.where(tril, b_al[:, c:c + 1] + r_t[c:c + 1, :], -jnp.inf)
        sqk = s_qk[bb, h] * jnp.exp(dlog - m_t[:, c:c + 1])
        v_aug = jnp.concatenate([vbs[bb, h], ones_blk], axis=1)
        nds[bb, h] = jnp.dot(sqk.astype(BF16), v_aug, preferred_element_type=F32)
    hss = {}
    for bb, h in streams:
        c = ZG_IF + h
        iw_c = g[bb][2][:, c:c + 1]
        nd, cq = nds[bb, h], cqs[bb, h]
        num = nd[:, 0:ML_DH] + iw_c * cq[:, 0:ML_DH]
        den = nd[:, ML_DH:ML_DH + 1] + iw_c * cq[:, ML_DH:ML_DH + 1]
        hss[bb, h] = num / jnp.maximum(jnp.abs(den), g[bb][3][:, c:c + 1])
    mss = {s: jnp.mean(hss[s] * hss[s], axis=-1, keepdims=True) for s in streams}
    for bb, h in streams:
        hn = hss[bb, h] * lax.rsqrt(mss[bb, h] + EPS) * hg_ref[:, h * ML_DH:(h + 1) * ML_DH]
        og = _sigmoid(o_ref[bb, :, h * ML_DH:(h + 1) * ML_DH].astype(F32))
        y_ref[bb, :, h * ML_DH:(h + 1) * ML_DH] = (og * hn).astype(BF16)


def _mlstm(zb3, zg3, conv_w, conv_b, gate_b, head_g, nb=2):
    b, s, _ = zb3.shape
    L = ML_CHUNK
    qb = ZB_QKVML // ML_W
    assert b % nb == 0

    def zspec(k):
        return pl.BlockSpec((nb, L, ML_W), lambda bi, c, k=k: (bi, c, k))

    return pl.pallas_call(
        functools.partial(_mlstm_kernel, nb=nb),
        grid=(b // nb, s // L),
        in_specs=[
            zspec(qb), zspec(qb + 1), zspec(qb + 2), zspec(ZB_OML // ML_W),
            pl.BlockSpec((nb, L, ZG_WIDTH), lambda bi, c: (bi, c, 0)),
            pl.BlockSpec((CONV_WIDTH, 2 * ML_W), lambda bi, c: (0, 0)),
            pl.BlockSpec((1, 2 * ML_W), lambda bi, c: (0, 0)),
            pl.BlockSpec((1, ZG_WIDTH), lambda bi, c: (0, 0)),
            pl.BlockSpec((1, ML_W), lambda bi, c: (0, 0)),
        ],
        out_specs=pl.BlockSpec((nb, L, ML_W), lambda bi, c: (bi, c, 0)),
        out_shape=jax.ShapeDtypeStruct((b, s, ML_W), BF16),
        scratch_shapes=[
            pltpu.VMEM((nb, 16, 2 * ML_W), BF16),
            pltpu.VMEM((nb * ML_HEADS, ML_DH, ML_DH), F32),
            pltpu.VMEM((nb, 8, ML_DH), F32),
            pltpu.VMEM((nb, 8, LANE), F32),
        ],
        compiler_params=_cparams("parallel", "arbitrary"),
        name="mlstm",
    )(zb3, zb3, zb3, zb3, zg3, conv_w, conv_b, gate_b, head_g)


def _memkv_kernel(mem_ref, g_ref, w_ref, kv_ref):
    hb = _rms(mem_ref[...], g_ref[...]).astype(BF16)
    kv_ref[...] = jnp.dot(hb, w_ref[...], preferred_element_type=F32).astype(BF16)


def _memkv(mem, g, wb):
    b, m, _ = mem.shape
    return pl.pallas_call(
        _memkv_kernel,
        grid=(b,),
        in_specs=[
            pl.BlockSpec((None, m, D_MODEL), lambda i: (i, 0, 0)),
            pl.BlockSpec((1, D_MODEL), lambda i: (0, 0)),
            pl.BlockSpec((D_MODEL, 2 * MEM_W), lambda i: (0, 0)),
        ],
        out_specs=pl.BlockSpec((None, m, 2 * MEM_W), lambda i: (i, 0, 0)),
        out_shape=jax.ShapeDtypeStruct((b, m, 2 * MEM_W), BF16),
        compiler_params=_cparams("parallel"),
        name="memkv",
    )(mem, g, wb)


def _memattn_kernel(q_ref, kv_ref, y_ref):
    c = MEM_SCALE * LOG2E
    scores = [lax.dot_general(q_ref[:, h * MEM_DH:(h + 1) * MEM_DH], kv_ref[:, h * MEM_DH:(h + 1) * MEM_DH],
                              _NT, preferred_element_type=F32) for h in range(MEM_HEADS)]
    for h in range(MEM_HEADS):
        s = scores[h]
        v = kv_ref[:, MEM_W + h * MEM_DH:MEM_W + (h + 1) * MEM_DH]
        e = jnp.exp2((s - jnp.max(s, axis=-1, keepdims=True)) * c)
        l = jnp.sum(e, axis=-1, keepdims=True)
        o = jnp.dot(e.astype(BF16), v, preferred_element_type=F32) * (1.0 / l)
        y_ref[:, h * MEM_DH:(h + 1) * MEM_DH] = o.astype(BF16)


def _memattn(zb3, kvm, tq=1024):
    b, s, _ = zb3.shape
    m = kvm.shape[1]
    return pl.pallas_call(
        _memattn_kernel,
        grid=(b, s // tq),
        in_specs=[
            pl.BlockSpec((None, tq, MEM_W), lambda i, j: (i, j, ZB_QMEM // MEM_W)),
            pl.BlockSpec((None, m, 2 * MEM_W), lambda i, j: (i, 0, 0)),
        ],
        out_specs=pl.BlockSpec((None, tq, MEM_W), lambda i, j: (i, j, 0)),
        out_shape=jax.ShapeDtypeStruct((b, s, MEM_W), BF16),
        compiler_params=_cparams("parallel", "parallel"),
        name="memattn",
    )(zb3, kvm)


def _merge_kernel(x_ref, yn_ref, yl_ref, ym_ref, g0_ref, g1_ref, g2_ref,
                  wn_ref, wl_ref, wm_ref, wo_ref, gp_ref, out_ref):
    y = _sigmoid(g0_ref[...].astype(F32)) * jnp.dot(yn_ref[...], wn_ref[...], preferred_element_type=F32)
    y = y + _sigmoid(g1_ref[...].astype(F32)) * jnp.dot(yl_ref[...], wl_ref[...], preferred_element_type=F32)
    y = y + _sigmoid(g2_ref[...].astype(F32)) * jnp.dot(ym_ref[...], wm_ref[...], preferred_element_type=F32)
    u = jnp.dot(y.astype(BF16), wo_ref[...], preferred_element_type=F32)
    out_ref[...] = x_ref[...] + _rms(u, gp_ref[...])


def _merge(x2, yn, yl, ym, zb, wn, wl, wm, wo, gp, tm=1024):
    t = x2.shape[0]
    gm = ZB_GMERGE // D_MODEL

    def const(shape):
        return pl.BlockSpec(shape, lambda i: (0, 0), pipeline_mode=pl.Buffered(1))

    return pl.pallas_call(
        _merge_kernel,
        grid=(t // tm,),
        in_specs=[
            pl.BlockSpec((tm, D_MODEL), lambda i: (i, 0)),
            pl.BlockSpec((tm, NSA_Q), lambda i: (i, 0)),
            pl.BlockSpec((tm, ML_W), lambda i: (i, 0)),
            pl.BlockSpec((tm, MEM_W), lambda i: (i, 0)),
            pl.BlockSpec((tm, D_MODEL), lambda i: (i, gm)),
            pl.BlockSpec((tm, D_MODEL), lambda i: (i, gm + 1)),
            pl.BlockSpec((tm, D_MODEL), lambda i: (i, gm + 2)),
            const((NSA_Q, D_MODEL)), const((ML_W, D_MODEL)), const((MEM_W, D_MODEL)),
            const((D_MODEL, D_MODEL)), const((1, D_MODEL)),
        ],
        out_specs=pl.BlockSpec((tm, D_MODEL), lambda i: (i, 0)),
        out_shape=jax.ShapeDtypeStruct((t, D_MODEL), F32),
        compiler_params=_cparams("parallel"),
        name="merge",
    )(x2, yn, yl, ym, zb, zb, zb, wn, wl, wm, wo, gp)


def _ffn_kernel(x_ref, gpre_ref, wg_ref, wu_ref, wd_ref, gpost_ref, out_ref, h_ref, acc_ref):
    j = pl.program_id(1)

    @pl.when(j == 0)
    def _():
        h_ref[...] = _rms(x_ref[...], gpre_ref[...]).astype(BF16)
        acc_ref[...] = jnp.zeros_like(acc_ref)

    h = h_ref[...]
    gate = jnp.dot(h, wg_ref[...], preferred_element_type=F32)
    up = jnp.dot(h, wu_ref[...], preferred_element_type=F32)
    act = (_silu(gate) * up).astype(BF16)
    acc_ref[...] += jnp.dot(act, wd_ref[...], preferred_element_type=F32)

    @pl.when(j == pl.num_programs(1) - 1)
    def _():
        out_ref[...] = x_ref[...] + _rms(acc_ref[...], gpost_ref[...])


def _ffn(x2, gpre, w_in, w_down, gpost, tm=512, nf=1):
    t = x2.shape[0]
    tf = D_FF // nf
    assert tf % LANE == 0
    wmode = dict(pipeline_mode=pl.Buffered(1)) if nf == 1 else {}
    return pl.pallas_call(
        _ffn_kernel,
        grid=(t // tm, nf),
        in_specs=[
            pl.BlockSpec((tm, D_MODEL), lambda i, j: (i, 0)),
            pl.BlockSpec((1, D_MODEL), lambda i, j: (0, 0)),
            pl.BlockSpec((D_MODEL, tf), lambda i, j: (0, j), **wmode),
            pl.BlockSpec((D_MODEL, tf), lambda i, j: (0, nf + j), **wmode),
            pl.BlockSpec((tf, D_MODEL), lambda i, j: (j, 0), **wmode),
            pl.BlockSpec((1, D_MODEL), lambda i, j: (0, 0)),
        ],
        out_specs=pl.BlockSpec((tm, D_MODEL), lambda i, j: (i, 0)),
        out_shape=jax.ShapeDtypeStruct((t, D_MODEL), F32),
        scratch_shapes=[pltpu.VMEM((tm, D_MODEL), BF16), pltpu.VMEM((tm, D_MODEL), F32)],
        compiler_params=_cparams("parallel", "arbitrary"),
        name="ffn",
    )(x2, gpre, w_in, w_in, w_down, gpost)


def _layer(x, mem, p):
    b, s, _ = x.shape
    t = b * s
    x2 = x.reshape(t, D_MODEL)

    wn, wt, wc = _wprep(p["w_in"].T)
    zt, zb, zg, zgt, kvc = _inproj(x2, p["g_pre_mix"].reshape(1, D_MODEL), wt, wn, wc, b, s)
    zb3 = zb.reshape(b, s, ZB_WIDTH)
    zg3 = zg.reshape(b, s, ZG_WIDTH)

    half = (CMP_BLOCK // 2) * NSA_DH
    pe = jnp.stack([p["cmp_pe_k"], p["cmp_pe_v"]]).reshape(2, 1, 2 * half)
    pe = jnp.pad(pe, ((0, 0), (0, 7), (0, 0))).astype(BF16)
    w1 = jnp.stack([p["cmp_w1_k"], p["cmp_w1_v"]]).reshape(2, 2 * half, CMP_HIDDEN).astype(BF16)
    kc, vct = _compress(kvc.reshape(b, s, KVC_WIDTH), pe, w1,
                        p["cmp_w2_k"].astype(BF16), p["cmp_w2_v"].T.astype(BF16))
    ocmpt, selb = _cmpsel(zt, kc, vct)
    y_nsa = _nsa(zt, zb3, selb, ocmpt, zgt)

    gate_b = jnp.zeros((1, ZG_WIDTH), F32).at[0, ZG_IF:ZG_IF + 2 * ML_HEADS].set(p["ml_gate_b"])
    y_ml = _mlstm(zb3, zg3, p["ml_conv_w"], p["ml_conv_b"].reshape(1, 2 * ML_W), gate_b,
                  p["ml_head_g"].reshape(1, ML_W))

    kvm = _memkv(mem, p["g_mem"].reshape(1, D_MODEL), p["w_mem_kv"].astype(BF16))
    y_mem = _memattn(zb3, kvm)

    x1 = _merge(x2, y_nsa.reshape(t, NSA_Q), y_ml.reshape(t, ML_W), y_mem.reshape(t, MEM_W), zb,
                p["w_proj_nsa"].astype(BF16), p["w_proj_ml"].astype(BF16),
                p["w_proj_mem"].astype(BF16), p["w_out"].astype(BF16),
                p["g_post_mix"].reshape(1, D_MODEL))
    x2o = _ffn(x1, p["g_pre_ffn"].reshape(1, D_MODEL), p["w_ffn_in"].astype(BF16),
               p["w_ffn_down"].astype(BF16), p["g_post_ffn"].reshape(1, D_MODEL))
    return x2o.reshape(b, s, D_MODEL)


def kernel(x, mem, g_pre_mix, w_in, cmp_pe_k, cmp_w1_k, cmp_w2_k, cmp_pe_v, cmp_w1_v, cmp_w2_v,
           ml_conv_w, ml_conv_b, ml_gate_b, ml_head_g, g_mem, w_mem_kv, w_proj_nsa, w_proj_ml,
           w_proj_mem, w_out, g_post_mix, g_pre_ffn, w_ffn_in, w_ffn_down, g_post_ffn):
    params = dict(
        g_pre_mix=g_pre_mix, w_in=w_in, cmp_pe_k=cmp_pe_k, cmp_w1_k=cmp_w1_k, cmp_w2_k=cmp_w2_k,
        cmp_pe_v=cmp_pe_v, cmp_w1_v=cmp_w1_v, cmp_w2_v=cmp_w2_v, ml_conv_w=ml_conv_w,
        ml_conv_b=ml_conv_b, ml_gate_b=ml_gate_b, ml_head_g=ml_head_g, g_mem=g_mem,
        w_mem_kv=w_mem_kv, w_proj_nsa=w_proj_nsa, w_proj_ml=w_proj_ml, w_proj_mem=w_proj_mem,
        w_out=w_out, g_post_mix=g_post_mix, g_pre_ffn=g_pre_ffn, w_ffn_in=w_ffn_in,
        w_ffn_down=w_ffn_down, g_post_ffn=g_post_ffn)
    depth = w_in.shape[0]
    for l in range(depth):
        x = _layer(x, mem, {k: v[l] for k, v in params.items()})
    return x
```

```python
import functools
import math

import jax
import jax.numpy as jnp
from jax import lax
from jax.experimental import pallas as pl
from jax.experimental.pallas import tpu as pltpu

F32 = jnp.float32
BF16 = jnp.bfloat16

D_MODEL = 1024
EPS = 1e-6
NEG_INF = -1e30
BIG = 1e30

NSA_HEADS = 16
NSA_GROUPS = 4
NSA_HPG = NSA_HEADS // NSA_GROUPS
NSA_DH = 64
NSA_SCALE = NSA_DH ** -0.5
CMP_BLOCK = 32
CMP_STRIDE = 16
CMP_HIDDEN = 128
SEL_BLOCK = 64
SEL_TOPK = 16
N_LOCAL_FORCED = 2
WINDOW = 512

ML_HEADS = 4
ML_DH = 128
ML_CHUNK = 128
CONV_WIDTH = 4

MEM_HEADS = 4
MEM_DH = 128
MEM_SCALE = MEM_DH ** -0.5

D_FF = -(-8 * D_MODEL // (3 * 256)) * 256

NSA_Q = NSA_HEADS * NSA_DH
NSA_KV = NSA_GROUPS * NSA_DH
ML_W = ML_HEADS * ML_DH
MEM_W = MEM_HEADS * MEM_DH

_OFF_Q = 0
_OFF_KV = _OFF_Q + NSA_Q
_OFF_GNSA = _OFF_KV + 6 * NSA_KV
_OFF_QKVML = _OFF_GNSA + 3 * NSA_HEADS
_OFF_IF = _OFF_QKVML + 3 * ML_W
_OFF_OML = _OFF_IF + 2 * ML_HEADS
_OFF_QMEM = _OFF_OML + ML_W
_OFF_GMERGE = _OFF_QMEM + MEM_W
_IN_WIDTH = _OFF_GMERGE + 3 * D_MODEL

ZB_GMERGE = 0
ZB_KSLC = ZB_GMERGE + 3 * D_MODEL
ZB_KWIN = ZB_KSLC + NSA_KV
ZB_QKVML = ZB_KWIN + NSA_KV
ZB_OML = ZB_QKVML + 3 * ML_W
ZB_QMEM = ZB_OML + ML_W
ZB_WIDTH = ZB_QMEM + MEM_W
KVC_WIDTH = 2 * NSA_KV
ZT_Q = 0
ZT_VSLC = ZT_Q + NSA_Q
ZT_VWIN = ZT_VSLC + NSA_KV
ZT_ROWS = ZT_VWIN + NSA_KV
ZG_WIDTH = 128
ZG_GNSA = 0
ZG_IF = 3 * NSA_HEADS

LANE = 128
SLAB_ROWS = 8
VMEM_LIMIT = 56 * 1024 * 1024
LOG2E = math.log2(math.e)

_NT = (((1,), (1,)), ((), ()))
_TN = (((0,), (0,)), ((), ()))


def _cparams(*sem):
    return pltpu.CompilerParams(dimension_semantics=sem, vmem_limit_bytes=VMEM_LIMIT)


def _rms(x, g):
    return x * lax.rsqrt(jnp.mean(x * x, axis=-1, keepdims=True) + EPS) * g


def _sigmoid(x):
    return 1.0 / (1.0 + jnp.exp(-x))


def _silu(x):
    return x * _sigmoid(x)


def _log_sigmoid(x):
    return jnp.minimum(x, 0.0) - jnp.log(1.0 + jnp.exp(-jnp.abs(x)))


def _wprep_kernel(w_ref, wn_ref, wt_ref, wc_ref):
    def rows(lo, hi):
        return w_ref[lo:hi, :]

    def kv_rows(k):
        return rows(_OFF_KV + k * NSA_KV, _OFF_KV + (k + 1) * NSA_KV)

    wn_ref[ZB_GMERGE:ZB_KSLC, :] = rows(_OFF_GMERGE, _IN_WIDTH).astype(BF16)
    wn_ref[ZB_KSLC:ZB_KWIN, :] = kv_rows(2).astype(BF16)
    wn_ref[ZB_KWIN:ZB_QKVML, :] = kv_rows(4).astype(BF16)
    wn_ref[ZB_QKVML:ZB_OML, :] = rows(_OFF_QKVML, _OFF_IF).astype(BF16)
    wn_ref[ZB_OML:ZB_WIDTH, :] = rows(_OFF_OML, _OFF_GMERGE).astype(BF16)
    wt_ref[ZT_Q:ZT_VSLC, :] = (rows(_OFF_Q, _OFF_KV) * (NSA_SCALE * LOG2E)).astype(BF16)
    wt_ref[ZT_VSLC:ZT_VWIN, :] = kv_rows(3).astype(BF16)
    wt_ref[ZT_VWIN:ZT_ROWS, :] = kv_rows(5).astype(BF16)
    n_g = 3 * NSA_HEADS
    n_if = 2 * ML_HEADS
    wt_ref[ZT_ROWS:ZT_ROWS + ZG_WIDTH, :] = jnp.zeros((ZG_WIDTH, wt_ref.shape[1]), BF16)
    wt_ref[ZT_ROWS + ZG_GNSA:ZT_ROWS + ZG_GNSA + n_g, :] = rows(_OFF_GNSA, _OFF_QKVML).astype(BF16)
    wt_ref[ZT_ROWS + ZG_IF:ZT_ROWS + ZG_IF + n_if, :] = rows(_OFF_IF, _OFF_OML).astype(BF16)
    wc_ref[...] = rows(_OFF_KV, _OFF_KV + KVC_WIDTH).astype(BF16)


def _wprep(w_t, tl=128):
    assert ZB_QMEM == ZB_OML + ML_W and _OFF_QMEM == _OFF_OML + ML_W
    return pl.pallas_call(
        _wprep_kernel,
        grid=(D_MODEL // tl,),
        in_specs=[pl.BlockSpec((_IN_WIDTH, tl), lambda i: (0, i))],
        out_specs=[
            pl.BlockSpec((ZB_WIDTH, tl), lambda i: (0, i)),
            pl.BlockSpec((ZT_ROWS + ZG_WIDTH, tl), lambda i: (0, i)),
            pl.BlockSpec((KVC_WIDTH, tl), lambda i: (0, i)),
        ],
        out_shape=[
            jax.ShapeDtypeStruct((ZB_WIDTH, D_MODEL), BF16),
            jax.ShapeDtypeStruct((ZT_ROWS + ZG_WIDTH, D_MODEL), BF16),
            jax.ShapeDtypeStruct((KVC_WIDTH, D_MODEL), BF16),
        ],
        compiler_params=_cparams("parallel"),
        name="wprep",
    )(w_t)


def _inproj_kernel(x_ref, g_ref, wt_ref, wn_ref, wc_ref,
                   zt_ref, zb_ref, zg_ref, zgt_ref, kvc_ref, h_ref, *, nt):
    j = pl.program_id(1)

    @pl.when(j == 0)
    def _():
        hb = _rms(x_ref[...], g_ref[...]).astype(BF16)
        h_ref[...] = hb
        kvc_ref[...] = lax.dot_general(hb, wc_ref[...], _NT, preferred_element_type=F32)
        zf = lax.dot_general(wt_ref[...], hb, _NT, preferred_element_type=F32)
        zt_ref[...] = zf[0:ZT_ROWS, :].astype(BF16)
        gt = zf[ZT_ROWS:ZT_ROWS + ZG_WIDTH, :]
        zgt_ref[...] = gt
        zg_ref[...] = gt.T

    @pl.when(j >= nt)
    def _():
        tn = zb_ref.shape[1]
        r0 = pl.multiple_of((j - nt) * tn, tn)
        zb_ref[...] = lax.dot_general(h_ref[...], wn_ref[pl.ds(r0, tn), :], _NT,
                                      preferred_element_type=F32).astype(BF16)


def _inproj(x2, g, wt, wn, wc, b, s, tm=1024, nn=3):
    t = x2.shape[0]
    nt = 1
    tn = ZB_WIDTH // nn
    assert tn * nn == ZB_WIDTH and tn % LANE == 0 and s % tm == 0
    spb = s // tm
    kern = functools.partial(_inproj_kernel, nt=nt)
    once = dict(pipeline_mode=pl.Buffered(1))
    return pl.pallas_call(
        kern,
        grid=(t // tm, nt + nn),
        in_specs=[
            pl.BlockSpec((tm, D_MODEL), lambda i, j: (i, 0)),
            pl.BlockSpec((1, D_MODEL), lambda i, j: (0, 0)),
            pl.BlockSpec((ZT_ROWS + ZG_WIDTH, D_MODEL), lambda i, j: (0, 0), **once),
            pl.BlockSpec((ZB_WIDTH, D_MODEL), lambda i, j: (0, 0), **once),
            pl.BlockSpec((KVC_WIDTH, D_MODEL), lambda i, j: (0, 0), **once),
        ],
        out_specs=[
            pl.BlockSpec((None, ZT_ROWS, tm), lambda i, j: (i // spb, 0, i % spb)),
            pl.BlockSpec((tm, tn), lambda i, j: (i, jnp.maximum(j - nt, 0))),
            pl.BlockSpec((tm, ZG_WIDTH), lambda i, j: (i, 0)),
            pl.BlockSpec((None, ZG_WIDTH, tm), lambda i, j: (i // spb, 0, i % spb)),
            pl.BlockSpec((tm, KVC_WIDTH), lambda i, j: (i, 0)),
        ],
        out_shape=[
            jax.ShapeDtypeStruct((b, ZT_ROWS, s), BF16),
            jax.ShapeDtypeStruct((t, ZB_WIDTH), BF16),
            jax.ShapeDtypeStruct((t, ZG_WIDTH), F32),
            jax.ShapeDtypeStruct((b, ZG_WIDTH, s), F32),
            jax.ShapeDtypeStruct((t, KVC_WIDTH), F32),
        ],
        scratch_shapes=[pltpu.VMEM((tm, D_MODEL), BF16)],
        compiler_params=_cparams("parallel", "arbitrary"),
        name="inproj",
    )(x2, g, wt, wn, wc)


def _compress_kernel(a0_ref, a1_ref, a2_ref, a3_ref, pe_ref, w1_ref, w2k_ref, w2vt_ref,
                     kc_ref, vct_ref):
    a_refs = (a0_ref, a1_ref, a2_ref, a3_ref)
    n_rows = a0_ref.shape[0] // CMP_STRIDE
    half = CMP_STRIDE * NSA_DH
    zero = jnp.zeros((NSA_DH, CMP_HIDDEN), BF16)
    for kind in range(2):
        pe = pe_ref[kind]
        c = jnp.dot(pe, w1_ref[kind], preferred_element_type=F32)[0:1, :]
        wps = []
        for l in range(CMP_STRIDE):
            wa = w1_ref[kind, l * NSA_DH:(l + 1) * NSA_DH, :]
            wb = w1_ref[kind, half + l * NSA_DH:half + (l + 1) * NSA_DH, :]
            wps.append(jnp.concatenate([jnp.concatenate([wa, zero, wb, zero], axis=1),
                                        jnp.concatenate([zero, wa, zero, wb], axis=1)], axis=0))
        for pair in range(NSA_GROUPS // 2):
            a_ref = a_refs[kind * 2 + pair]
            acc = jnp.zeros((n_rows, 4 * CMP_HIDDEN), F32)
            for l in range(CMP_STRIDE):
                rows = a_ref[pl.ds(l, n_rows, stride=CMP_STRIDE), :].astype(BF16)
                acc = acc + jnp.dot(rows, wps[l], preferred_element_type=F32)
            for gi in range(2):
                g = 2 * pair + gi
                p = acc[:, gi * CMP_HIDDEN:(gi + 1) * CMP_HIDDEN]
                q = acc[:, (2 + gi) * CMP_HIDDEN:(3 + gi) * CMP_HIDDEN]
                qs = pltpu.roll(q, shift=n_rows - 1, axis=0)
                hid = _silu(p + qs + c).astype(BF16)
                if kind == 0:
                    o = jnp.dot(hid, w2k_ref[...], preferred_element_type=F32)
                    kc_ref[:, g * NSA_DH:(g + 1) * NSA_DH] = o.astype(BF16)
                else:
                    ot = lax.dot_general(w2vt_ref[...], hid, _NT, preferred_element_type=F32)
                    vct_ref[g * NSA_DH:(g + 1) * NSA_DH, :] = ot.astype(BF16)


def _compress(kvc3, pe, w1, w2k, w2vt):
    b, s, _ = kvc3.shape
    n_rows = s // CMP_STRIDE
    half = (CMP_BLOCK // 2) * NSA_DH

    def col(k):
        return pl.BlockSpec((None, s, LANE), lambda i, k=k: (i, 0, k))

    return pl.pallas_call(
        _compress_kernel,
        grid=(b,),
        in_specs=[
            col(0), col(1), col(2), col(3),
            pl.BlockSpec((2, 8, 2 * half), lambda i: (0, 0, 0)),
            pl.BlockSpec((2, 2 * half, CMP_HIDDEN), lambda i: (0, 0, 0)),
            pl.BlockSpec((CMP_HIDDEN, NSA_DH), lambda i: (0, 0)),
            pl.BlockSpec((NSA_DH, CMP_HIDDEN), lambda i: (0, 0)),
        ],
        out_specs=[
            pl.BlockSpec((None, n_rows, NSA_KV), lambda i: (i, 0, 0)),
            pl.BlockSpec((None, NSA_KV, n_rows), lambda i: (i, 0, 0)),
        ],
        out_shape=[
            jax.ShapeDtypeStruct((b, n_rows, NSA_KV), BF16),
            jax.ShapeDtypeStruct((b, NSA_KV, n_rows), BF16),
        ],
        compiler_params=_cparams("parallel"),
        name="compress",
    )(kvc3, kvc3, kvc3, kvc3, pe, w1, w2k, w2vt)


def _pad_query(qt, g):
    z = jnp.zeros_like(qt)
    return jnp.concatenate([qt, z] if g % 2 == 0 else [z, qt], axis=0)


def _cmpsel_kernel(qt_ref, kc_ref, vct_ref, ocmpt_ref, selb_ref, *, tq, n_sel, topk):
    nc = kc_ref.shape[0]
    t0 = pl.program_id(1) * tq

    n_t = lax.broadcasted_iota(jnp.int32, (nc, tq), 0)
    t_t = t0 + lax.broadcasted_iota(jnp.int32, (nc, tq), 1)
    mask_t = (n_t * CMP_STRIDE + (CMP_BLOCK - 1)) <= t_t

    j_m = lax.broadcasted_iota(jnp.int32, (n_sel, nc), 0) * SEL_BLOCK
    c_m = lax.broadcasted_iota(jnp.int32, (n_sel, nc), 1) * CMP_STRIDE
    ov = jnp.minimum(c_m + CMP_BLOCK, j_m + SEL_BLOCK) - jnp.maximum(c_m, j_m)
    map_t = jnp.maximum(ov, 0).astype(F32) * (1.0 / CMP_BLOCK)

    j_s = lax.broadcasted_iota(jnp.int32, (n_sel, tq), 0)
    qblk = (t0 + lax.broadcasted_iota(jnp.int32, (n_sel, tq), 1)) // SEL_BLOCK
    rel = qblk - j_s
    causal = rel >= 0
    forced = causal & ((j_s == 0) | (rel < N_LOCAL_FORCED))

    def scores(g):
        kcp = kc_ref[:, (g // 2) * LANE:(g // 2 + 1) * LANE]
        return [jnp.dot(kcp, _pad_query(qt_ref[(g * NSA_HPG + h) * NSA_DH:(g * NSA_HPG + h + 1) * NSA_DH, :], g),
                        preferred_element_type=F32) for h in range(NSA_HPG)]

    sts_next = scores(0)
    for g in range(NSA_GROUPS):
        sts = sts_next
        if g + 1 < NSA_GROUPS:
            sts_next = scores(g + 1)
        vct = vct_ref[g * NSA_DH:(g + 1) * NSA_DH, :]
        masked = [jnp.where(mask_t, st, NEG_INF) for st in sts]
        ets = [jnp.where(mask_t, jnp.exp2(st - jnp.max(st, axis=0, keepdims=True)), 0.0) for st in masked]
        lts = [jnp.sum(et, axis=0, keepdims=True) for et in ets]
        pts = [et * (1.0 / jnp.where(lt > 0.0, lt, 1.0)) for et, lt in zip(ets, lts)]
        for h in range(NSA_HPG):
            hh = g * NSA_HPG + h
            ot = jnp.dot(vct, pts[h].astype(BF16), preferred_element_type=F32)
            ocmpt_ref[hh * NSA_DH:(hh + 1) * NSA_DH, :] = ot.astype(BF16)
        psum_t = (pts[0] + pts[1]) + (pts[2] + pts[3])
        imp_t = jnp.dot(map_t, psum_t, preferred_element_type=F32,
                        precision=lax.Precision.HIGHEST)
        score = jnp.where(forced, BIG, jnp.where(causal, imp_t, NEG_INF))
        rank = jnp.zeros((n_sel, tq), jnp.int32)
        for jp in range(n_sel):
            row = score[jp:jp + 1, :]
            before = (row > score) | ((row == score) & (j_s > jp))
            rank = rank + before.astype(jnp.int32)
        chosen = (rank < topk) & (score > 0.5 * NEG_INF)
        selb_ref[g * n_sel:(g + 1) * n_sel, :] = jnp.where(chosen, 0.0, NEG_INF)


def _cmpsel(zt, kc, vct, tq=256):
    b, _, s = zt.shape
    n_sel = s // SEL_BLOCK
    topk = min(SEL_TOPK, n_sel)
    nc = kc.shape[1]
    kern = functools.partial(_cmpsel_kernel, tq=tq, n_sel=n_sel, topk=topk)
    return pl.pallas_call(
        kern,
        grid=(b, s // tq),
        in_specs=[
            pl.BlockSpec((None, NSA_Q, tq), lambda i, j: (i, ZT_Q // NSA_Q, j)),
            pl.BlockSpec((None, nc, NSA_KV), lambda i, j: (i, 0, 0)),
            pl.BlockSpec((None, NSA_KV, nc), lambda i, j: (i, 0, 0)),
        ],
        out_specs=[
            pl.BlockSpec((None, NSA_Q, tq), lambda i, j: (i, 0, j)),
            pl.BlockSpec((None, NSA_GROUPS * n_sel, tq), lambda i, j: (i, 0, j)),
        ],
        out_shape=[
            jax.ShapeDtypeStruct((b, NSA_Q, s), BF16),
            jax.ShapeDtypeStruct((b, NSA_GROUPS * n_sel, s), F32),
        ],
        compiler_params=_cparams("parallel", "parallel"),
        name="cmpsel",
    )(zt, kc, vct)


def _nsa_kernel(qt_ref, ks_ref, kw_ref, vst_ref, vwt_ref, selb_ref, ocmpt_ref, gt_ref, y_ref,
                yt_scr, qp_scr, m_scr, acc_scr, st_scr, mx_scr, bias_scr, *, tq, n_sel, look):
    tk = tq
    nw = WINDOW // tk
    sel_per_tile = tk // SEL_BLOCK
    den_rows = 16
    i = pl.program_id(1)
    d0 = (lax.broadcasted_iota(jnp.int32, (tk, tq), 1)
          - lax.broadcasted_iota(jnp.int32, (tk, tq), 0))
    caus = jnp.where(d0 >= 0, 0.0, NEG_INF)
    lowb = jnp.where(d0 < 0, 0.0, NEG_INF)
    ones_rows = jnp.ones((den_rows, tk), BF16)
    ext = 16
    key_blk = lax.broadcasted_iota(jnp.int32, (tk, ext), 0) // SEL_BLOCK
    ext_col = lax.broadcasted_iota(jnp.int32, (tk, ext), 1)
    gates = _sigmoid(gt_ref[...])

    for hh in range(NSA_HEADS):
        qp_scr[hh] = _pad_query(qt_ref[hh * NSA_DH:(hh + 1) * NSA_DH, :], hh // NSA_HPG)
    m_scr[...] = jnp.full(m_scr.shape, NEG_INF, F32)
    acc_scr[...] = jnp.zeros(acc_scr.shape, F32)

    SLC = (0, ks_ref, vst_ref, True)
    WIN = (1, kw_ref, vwt_ref, False)

    def run(tiles):
        loaded = []
        for ti, ((br, k_ref, vt_ref, sel_on), kt, mask) in enumerate(tiles):
            r0 = pl.multiple_of(kt * tk, tk)
            ktiles = [k_ref[pl.ds(r0, tk), pair * LANE:(pair + 1) * LANE]
                      for pair in range(NSA_GROUPS // 2)]
            vt_augs = [jnp.concatenate([vt_ref[g * NSA_DH:(g + 1) * NSA_DH, pl.ds(r0, tk)], ones_rows],
                                       axis=0) for g in range(NSA_GROUPS)]
            slabs = None
            if sel_on:
                row0 = kt * sel_per_tile
                slab0 = pl.multiple_of((row0 // SLAB_ROWS) * SLAB_ROWS, SLAB_ROWS)
                hot = jnp.where(key_blk + row0 % SLAB_ROWS == ext_col, 1.0, 0.0).astype(BF16)
                ktiles = [jnp.concatenate([kp, hot], axis=1) for kp in ktiles]
                slabs = [jnp.concatenate(
                    [selb_ref[pl.ds(g * n_sel + slab0, SLAB_ROWS), :],
                     jnp.zeros((ext - SLAB_ROWS, tq), F32)], axis=0).astype(BF16)
                    for g in range(NSA_GROUPS)]
            loaded.append((ktiles, vt_augs, slabs))
            if mask is not None:
                bias_scr[ti] = mask

        def qk(ti, hh):
            g = hh // NSA_HPG
            ktiles, _, slabs = loaded[ti]
            q = qp_scr[hh]
            if slabs is not None:
                q = jnp.concatenate([q, slabs[g]], axis=0)
            st = jnp.dot(ktiles[g // 2], q, preferred_element_type=F32)
            if tiles[ti][2] is not None:
                st = st + bias_scr[ti]
            st_scr[ti * NSA_HEADS + hh] = st
            mx_scr[ti * NSA_HEADS + hh] = jnp.max(st, axis=0, keepdims=True)

        def softmax_pv(ti, hh):
            br = tiles[ti][0][0]
            vt_aug = loaded[ti][1][hh // NSA_HPG]
            m = m_scr[br, hh]
            m_new = jnp.maximum(m, mx_scr[ti * NSA_HEADS + hh])
            alpha = jnp.exp2(m - m_new)
            p = jnp.exp2(st_scr[ti * NSA_HEADS + hh] - m_new).astype(BF16)
            acc_scr[br, hh] = alpha * acc_scr[br, hh] + jnp.dot(vt_aug, p, preferred_element_type=F32)
            m_scr[br, hh] = m_new

        seq = [(ti, hh) for ti in range(len(tiles)) for hh in range(NSA_HEADS)]
        for pos in range(min(look, len(seq))):
            qk(*seq[pos])
        for pos, item in enumerate(seq):
            if pos + look < len(seq):
                qk(*seq[pos + look])
            softmax_pv(*item)

    def slc_body(j, c):
        run([(SLC, 2 * j, None), (SLC, 2 * j + 1, None)])
        return c

    lax.fori_loop(0, i // 2, slc_body, 0)

    def tail(par, v):
        slc = [(SLC, i - 1, None)] * par + [(SLC, i, caus)]
        if v < nw:
            win = [(WIN, kt, None) for kt in range(v)] + [(WIN, i, caus)]
        else:
            win = ([(WIN, i - nw, lowb)] + [(WIN, i - nw + d, None) for d in range(1, nw)]
                   + [(WIN, i, caus)])
        run(slc + win)

    for v in range(nw):
        @pl.when(i == v)
        def _(v=v):
            tail(v % 2, v)

    for par in range(2):
        @pl.when((i >= nw) & (i % 2 == par))
        def _(par=par):
            tail(par, nw)

    for hh in range(NSA_HEADS):
        acc_s = acc_scr[0, hh]
        acc_w = acc_scr[1, hh]
        o_slc = acc_s[0:NSA_DH] * (1.0 / acc_s[NSA_DH:NSA_DH + 1])
        o_win = acc_w[0:NSA_DH] * (1.0 / acc_w[NSA_DH:NSA_DH + 1])
        o_cmp = ocmpt_ref[hh * NSA_DH:(hh + 1) * NSA_DH, :].astype(F32)
        gc = ZG_GNSA + 3 * hh
        yt_scr[hh * NSA_DH:(hh + 1) * NSA_DH, :] = (
            gates[gc:gc + 1] * o_cmp + gates[gc + 1:gc + 2] * o_slc + gates[gc + 2:gc + 3] * o_win)

    y_ref[...] = yt_scr[...].T.astype(BF16)


def _nsa(zt, zb3, selb, ocmpt, zgt, tq=256, look=5):
    b, _, s = zt.shape
    n_sel = s // SEL_BLOCK
    tk = tq
    den_rows = 16
    max_tiles = WINDOW // tk + 3
    assert WINDOW % tq == 0 and tq % SEL_BLOCK == 0 and SLAB_ROWS % (tk // SEL_BLOCK) == 0
    kern = functools.partial(_nsa_kernel, tq=tq, n_sel=n_sel, look=look)
    return pl.pallas_call(
        kern,
        grid=(b, s // tq),
        in_specs=[
            pl.BlockSpec((None, NSA_Q, tq), lambda bi, j: (bi, ZT_Q // NSA_Q, j)),
            pl.BlockSpec((None, s, NSA_KV), lambda bi, j: (bi, 0, ZB_KSLC // NSA_KV)),
            pl.BlockSpec((None, s, NSA_KV), lambda bi, j: (bi, 0, ZB_KWIN // NSA_KV)),
            pl.BlockSpec((None, NSA_KV, s), lambda bi, j: (bi, ZT_VSLC // NSA_KV, 0)),
            pl.BlockSpec((None, NSA_KV, s), lambda bi, j: (bi, ZT_VWIN // NSA_KV, 0)),
            pl.BlockSpec((None, NSA_GROUPS * n_sel, tq), lambda bi, j: (bi, 0, j)),
            pl.BlockSpec((None, NSA_Q, tq), lambda bi, j: (bi, 0, j)),
            pl.BlockSpec((None, ZG_WIDTH, tq), lambda bi, j: (bi, 0, j)),
        ],
        out_specs=pl.BlockSpec((None, tq, NSA_Q), lambda bi, j: (bi, j, 0)),
        out_shape=jax.ShapeDtypeStruct((b, s, NSA_Q), BF16),
        scratch_shapes=[
            pltpu.VMEM((NSA_Q, tq), F32),
            pltpu.VMEM((NSA_HEADS, 2 * NSA_DH, tq), BF16),
            pltpu.VMEM((2, NSA_HEADS, 1, tq), F32),
            pltpu.VMEM((2, NSA_HEADS, NSA_DH + den_rows, tq), F32),
            pltpu.VMEM((max_tiles * NSA_HEADS, tk, tq), F32),
            pltpu.VMEM((max_tiles * NSA_HEADS, 1, tq), F32),
            pltpu.VMEM((max_tiles, tk, tq), F32),
        ],
        compiler_params=_cparams("parallel", "arbitrary"),
        name="nsa",
    )(zt, zb3, zb3, zt, zt, selb, ocmpt, zgt)


def _mlstm_kernel(q_ref, k_ref, v_ref, o_ref, zg_ref, cw_ref, cb_ref, gb_ref, hg_ref, y_ref,
                  xbuf, c_st, n_st, m_st, *, nb):
    L = ML_CHUNK
    hist = xbuf.shape[1]
    @pl.when(pl.program_id(1) == 0)
    def _():
        xbuf[...] = jnp.zeros_like(xbuf)
        c_st[...] = jnp.zeros_like(c_st)
        n_st[...] = jnp.zeros_like(n_st)
        m_st[...] = jnp.zeros_like(m_st)

    row = lax.broadcasted_iota(jnp.int32, (L, L), 0)
    col = lax.broadcasted_iota(jnp.int32, (L, L), 1)
    tril = row >= col
    tril_f = jnp.where(tril, 1.0, 0.0)
    streams = [(bb, h) for bb in range(nb) for h in range(ML_HEADS)]

    row_l = lax.broadcasted_iota(jnp.int32, (L, ZG_WIDTH), 0)
    g = []
    for bb in range(nb):
        gates = zg_ref[bb] + gb_ref[...]
        bcum = jnp.dot(tril_f, _log_sigmoid(gates), preferred_element_type=F32,
                       precision=lax.Precision.HIGHEST)
        b_al = pltpu.roll(bcum, shift=ZG_WIDTH - ML_HEADS, axis=1)
        m_prev = m_st[bb, 0:1, :]
        b_end = b_al[L - 1:L, :]
        inter = b_al + m_prev
        wlog = b_end - b_al + gates
        m_new = jnp.maximum(b_end + m_prev, jnp.max(wlog, axis=0, keepdims=True))
        ws = jnp.exp(wlog - m_new)
        decay = jnp.exp(b_end + m_prev - m_new)
        m_st[bb, 0:1, :] = m_new
        r = gates - b_al
        cm = r
        d = 1
        while d < L:
            cm = jnp.maximum(cm, jnp.where(row_l >= d, pltpu.roll(cm, shift=d, axis=0), -jnp.inf))
            d *= 2
        m_t = jnp.maximum(inter, b_al + cm)
        iw = jnp.exp(inter - m_t)
        em = jnp.exp(-m_t)
        g.append((b_al, m_t, iw, em, ws, decay, r.T))

    sr = lax.broadcasted_iota(jnp.int32, (CONV_WIDTH * L, hist + L), 0)
    scol = lax.broadcasted_iota(jnp.int32, (CONV_WIDTH * L, hist + L), 1)
    shift = jnp.where(scol == hist - (CONV_WIDTH - 1) + sr % L + sr // L, 1.0, 0.0).astype(BF16)
    qks = []
    for bb in range(nb):
        halves = []
        for hf, x_ref in enumerate((q_ref, k_ref)):
            cur = x_ref[bb]
            ext = jnp.concatenate([xbuf[bb, :, hf * ML_W:(hf + 1) * ML_W], cur], axis=0)
            taps = jnp.dot(shift, ext, preferred_element_type=F32)
            conv = jnp.zeros((L, ML_W), F32) + cb_ref[:, hf * ML_W:(hf + 1) * ML_W]
            for j in range(CONV_WIDTH):
                conv = conv + taps[j * L:(j + 1) * L, :] * cw_ref[j:j + 1, hf * ML_W:(hf + 1) * ML_W]
            xbuf[bb, :, hf * ML_W:(hf + 1) * ML_W] = cur[L - hist:L, :]
            halves.append(_silu(conv))
        qks.append(jnp.concatenate(halves, axis=1))

    qs = {(bb, h): qks[bb][:, h * ML_DH:(h + 1) * ML_DH] for bb, h in streams}
    ks = {(bb, h): qks[bb][:, ML_W + h * ML_DH:ML_W + (h + 1) * ML_DH] * (ML_DH ** -0.5)
          for bb, h in streams}
    qbs = {s: qs[s].astype(BF16) for s in streams}
    vbs = {(bb, h): v_ref[bb, :, h * ML_DH:(h + 1) * ML_DH] for bb, h in streams}
    c_olds = {(bb, h): c_st[bb * ML_HEADS + h] for bb, h in streams}
    n_olds = {(bb, h): n_st[bb, h:h + 1, :] for bb, h in streams}
    s_qk = {s: lax.dot_general(qbs[s], ks[s].astype(BF16), _NT, preferred_element_type=F32)
            for s in streams}
    n_pad = jnp.zeros((ML_DH - 1, ML_DH), F32)
    cqs = {s: lax.dot_general(
        qbs[s], jnp.concatenate([c_olds[s], n_olds[s], n_pad], axis=0).astype(BF16), _NT,
        preferred_element_type=F32) for s in streams}
    ones_blk = jnp.ones((L, ML_DH), BF16)

    for bb, h in streams:
        c = ZG_IF + h
        ws, decay = g[bb][4], g[bb][5]
        kw = ks[bb, h] * ws[:, c:c + 1]
        upd = lax.dot_general(vbs[bb, h], kw.astype(BF16), _TN, preferred_element_type=F32)
        c_st[bb * ML_HEADS + h] = decay[:, c:c + 1] * c_olds[bb, h] + upd
        n_st[bb, h:h + 1, :] = decay[:, c:c + 1] * n_olds[bb, h] + jnp.sum(kw, axis=0, keepdims=True)

    nds = {}
    for bb, h in streams:
        c = ZG_IF + h
        b_al, m_t, _, _, _, _, r_t = g[bb]
        dlog = jnp.where(tril, b_al[:, c:c + 1] + r_t[c:c + 1, :], -jnp.inf)
        sqk = s_qk[bb, h] * jnp.exp(dlog - m_t[:, c:c + 1])
        v_aug = jnp.concatenate([vbs[bb, h], ones_blk], axis=1)
        nds[bb, h] = jnp.dot(sqk.astype(BF16), v_aug, preferred_element_type=F32)
    hss = {}
    for bb, h in streams:
        c = ZG_IF + h
        iw_c = g[bb][2][:, c:c + 1]
        nd, cq = nds[bb, h], cqs[bb, h]
        num = nd[:, 0:ML_DH] + iw_c * cq[:, 0:ML_DH]
        den = nd[:, ML_DH:ML_DH + 1] + iw_c * cq[:, ML_DH:ML_DH + 1]
        hss[bb, h] = num / jnp.maximum(jnp.abs(den), g[bb][3][:, c:c + 1])
    mss = {s: jnp.mean(hss[s] * hss[s], axis=-1, keepdims=True) for s in streams}
    for bb, h in streams:
        hn = hss[bb, h] * lax.rsqrt(mss[bb, h] + EPS) * hg_ref[:, h * ML_DH:(h + 1) * ML_DH]
        og = _sigmoid(o_ref[bb, :, h * ML_DH:(h + 1) * ML_DH].astype(F32))
        y_ref[bb, :, h * ML_DH:(h + 1) * ML_DH] = (og * hn).astype(BF16)


def _mlstm(zb3, zg3, conv_w, conv_b, gate_b, head_g, nb=2):
    b, s, _ = zb3.shape
    L = ML_CHUNK
    qb = ZB_QKVML // ML_W
    assert b % nb == 0

    def zspec(k):
        return pl.BlockSpec((nb, L, ML_W), lambda bi, c, k=k: (bi, c, k))

    return pl.pallas_call(
        functools.partial(_mlstm_kernel, nb=nb),
        grid=(b // nb, s // L),
        in_specs=[
            zspec(qb), zspec(qb + 1), zspec(qb + 2), zspec(ZB_OML // ML_W),
            pl.BlockSpec((nb, L, ZG_WIDTH), lambda bi, c: (bi, c, 0)),
            pl.BlockSpec((CONV_WIDTH, 2 * ML_W), lambda bi, c: (0, 0)),
            pl.BlockSpec((1, 2 * ML_W), lambda bi, c: (0, 0)),
            pl.BlockSpec((1, ZG_WIDTH), lambda bi, c: (0, 0)),
            pl.BlockSpec((1, ML_W), lambda bi, c: (0, 0)),
        ],
        out_specs=pl.BlockSpec((nb, L, ML_W), lambda bi, c: (bi, c, 0)),
        out_shape=jax.ShapeDtypeStruct((b, s, ML_W), BF16),
        scratch_shapes=[
            pltpu.VMEM((nb, 16, 2 * ML_W), BF16),
            pltpu.VMEM((nb * ML_HEADS, ML_DH, ML_DH), F32),
            pltpu.VMEM((nb, 8, ML_DH), F32),
            pltpu.VMEM((nb, 8, LANE), F32),
        ],
        compiler_params=_cparams("parallel", "arbitrary"),
        name="mlstm",
    )(zb3, zb3, zb3, zb3, zg3, conv_w, conv_b, gate_b, head_g)


def _memattn_kernel(q_ref, mem_ref, g_ref, w_ref, y_ref, kv_ref):
    @pl.when(pl.program_id(1) == 0)
    def _():
        hb = _rms(mem_ref[...], g_ref[...]).astype(BF16)
        kv_ref[...] = jnp.dot(hb, w_ref[...], preferred_element_type=F32).astype(BF16)

    c = MEM_SCALE * LOG2E
    scores = [lax.dot_general(q_ref[:, h * MEM_DH:(h + 1) * MEM_DH], kv_ref[:, h * MEM_DH:(h + 1) * MEM_DH],
                              _NT, preferred_element_type=F32) for h in range(MEM_HEADS)]
    for h in range(MEM_HEADS):
        s = scores[h]
        v = kv_ref[:, MEM_W + h * MEM_DH:MEM_W + (h + 1) * MEM_DH]
        e = jnp.exp2((s - jnp.max(s, axis=-1, keepdims=True)) * c)
        l = jnp.sum(e, axis=-1, keepdims=True)
        o = jnp.dot(e.astype(BF16), v, preferred_element_type=F32) * (1.0 / l)
        y_ref[:, h * MEM_DH:(h + 1) * MEM_DH] = o.astype(BF16)


def _memattn(zb3, mem, g, wb, tq=1024):
    b, s, _ = zb3.shape
    m = mem.shape[1]
    return pl.pallas_call(
        _memattn_kernel,
        grid=(b, s // tq),
        in_specs=[
            pl.BlockSpec((None, tq, MEM_W), lambda i, j: (i, j, ZB_QMEM // MEM_W)),
            pl.BlockSpec((None, m, D_MODEL), lambda i, j: (i, 0, 0)),
            pl.BlockSpec((1, D_MODEL), lambda i, j: (0, 0)),
            pl.BlockSpec((D_MODEL, 2 * MEM_W), lambda i, j: (0, 0), pipeline_mode=pl.Buffered(1)),
        ],
        out_specs=pl.BlockSpec((None, tq, MEM_W), lambda i, j: (i, j, 0)),
        out_shape=jax.ShapeDtypeStruct((b, s, MEM_W), BF16),
        scratch_shapes=[pltpu.VMEM((m, 2 * MEM_W), BF16)],
        compiler_params=_cparams("parallel", "arbitrary"),
        name="memattn",
    )(zb3, mem, g, wb)


def _merge_kernel(x_ref, yn_ref, yl_ref, ym_ref, g0_ref, g1_ref, g2_ref,
                  wn_ref, wl_ref, wm_ref, wo_ref, gp_ref, out_ref):
    y = _sigmoid(g0_ref[...].astype(F32)) * jnp.dot(yn_ref[...], wn_ref[...], preferred_element_type=F32)
    y = y + _sigmoid(g1_ref[...].astype(F32)) * jnp.dot(yl_ref[...], wl_ref[...], preferred_element_type=F32)
    y = y + _sigmoid(g2_ref[...].astype(F32)) * jnp.dot(ym_ref[...], wm_ref[...], preferred_element_type=F32)
    u = jnp.dot(y.astype(BF16), wo_ref[...], preferred_element_type=F32)
    out_ref[...] = x_ref[...] + _rms(u, gp_ref[...])


def _merge(x2, yn, yl, ym, zb, wn, wl, wm, wo, gp, tm=1024):
    t = x2.shape[0]
    gm = ZB_GMERGE // D_MODEL

    def const(shape):
        return pl.BlockSpec(shape, lambda i: (0, 0), pipeline_mode=pl.Buffered(1))

    return pl.pallas_call(
        _merge_kernel,
        grid=(t // tm,),
        in_specs=[
            pl.BlockSpec((tm, D_MODEL), lambda i: (i, 0)),
            pl.BlockSpec((tm, NSA_Q), lambda i: (i, 0)),
            pl.BlockSpec((tm, ML_W), lambda i: (i, 0)),
            pl.BlockSpec((tm, MEM_W), lambda i: (i, 0)),
            pl.BlockSpec((tm, D_MODEL), lambda i: (i, gm)),
            pl.BlockSpec((tm, D_MODEL), lambda i: (i, gm + 1)),
            pl.BlockSpec((tm, D_MODEL), lambda i: (i, gm + 2)),
            const((NSA_Q, D_MODEL)), const((ML_W, D_MODEL)), const((MEM_W, D_MODEL)),
            const((D_MODEL, D_MODEL)), const((1, D_MODEL)),
        ],
        out_specs=pl.BlockSpec((tm, D_MODEL), lambda i: (i, 0)),
        out_shape=jax.ShapeDtypeStruct((t, D_MODEL), F32),
        compiler_params=_cparams("parallel"),
        name="merge",
    )(x2, yn, yl, ym, zb, zb, zb, wn, wl, wm, wo, gp)


def _ffn_kernel(x_ref, gpre_ref, wg_ref, wu_ref, wd_ref, gpost_ref, out_ref, h_ref, acc_ref):
    j = pl.program_id(1)

    @pl.when(j == 0)
    def _():
        h_ref[...] = _rms(x_ref[...], gpre_ref[...]).astype(BF16)
        acc_ref[...] = jnp.zeros_like(acc_ref)

    h = h_ref[...]
    gate = jnp.dot(h, wg_ref[...], preferred_element_type=F32)
    up = jnp.dot(h, wu_ref[...], preferred_element_type=F32)
    act = (_silu(gate) * up).astype(BF16)
    acc_ref[...] += jnp.dot(act, wd_ref[...], preferred_element_type=F32)

    @pl.when(j == pl.num_programs(1) - 1)
    def _():
        out_ref[...] = x_ref[...] + _rms(acc_ref[...], gpost_ref[...])


def _ffn(x2, gpre, w_in, w_down, gpost, tm=512, nf=1):
    t = x2.shape[0]
    tf = D_FF // nf
    assert tf % LANE == 0
    wmode = dict(pipeline_mode=pl.Buffered(1)) if nf == 1 else {}
    return pl.pallas_call(
        _ffn_kernel,
        grid=(t // tm, nf),
        in_specs=[
            pl.BlockSpec((tm, D_MODEL), lambda i, j: (i, 0)),
            pl.BlockSpec((1, D_MODEL), lambda i, j: (0, 0)),
            pl.BlockSpec((D_MODEL, tf), lambda i, j: (0, j), **wmode),
            pl.BlockSpec((D_MODEL, tf), lambda i, j: (0, nf + j), **wmode),
            pl.BlockSpec((tf, D_MODEL), lambda i, j: (j, 0), **wmode),
            pl.BlockSpec((1, D_MODEL), lambda i, j: (0, 0)),
        ],
        out_specs=pl.BlockSpec((tm, D_MODEL), lambda i, j: (i, 0)),
        out_shape=jax.ShapeDtypeStruct((t, D_MODEL), F32),
        scratch_shapes=[pltpu.VMEM((tm, D_MODEL), BF16), pltpu.VMEM((tm, D_MODEL), F32)],
        compiler_params=_cparams("parallel", "arbitrary"),
        name="ffn",
    )(x2, gpre, w_in, w_in, w_down, gpost)


def _layer(x, mem, p):
    b, s, _ = x.shape
    t = b * s
    x2 = x.reshape(t, D_MODEL)

    wn, wt, wc = _wprep(p["w_in"].T)
    zt, zb, zg, zgt, kvc = _inproj(x2, p["g_pre_mix"].reshape(1, D_MODEL), wt, wn, wc, b, s)
    zb3 = zb.reshape(b, s, ZB_WIDTH)
    zg3 = zg.reshape(b, s, ZG_WIDTH)

    half = (CMP_BLOCK // 2) * NSA_DH
    pe = jnp.stack([p["cmp_pe_k"], p["cmp_pe_v"]]).reshape(2, 1, 2 * half)
    pe = jnp.pad(pe, ((0, 0), (0, 7), (0, 0))).astype(BF16)
    w1 = jnp.stack([p["cmp_w1_k"], p["cmp_w1_v"]]).reshape(2, 2 * half, CMP_HIDDEN).astype(BF16)
    kc, vct = _compress(kvc.reshape(b, s, KVC_WIDTH), pe, w1,
                        p["cmp_w2_k"].astype(BF16), p["cmp_w2_v"].T.astype(BF16))
    ocmpt, selb = _cmpsel(zt, kc, vct)
    y_nsa = _nsa(zt, zb3, selb, ocmpt, zgt)

    gate_b = jnp.zeros((1, ZG_WIDTH), F32).at[0, ZG_IF:ZG_IF + 2 * ML_HEADS].set(p["ml_gate_b"])
    y_ml = _mlstm(zb3, zg3, p["ml_conv_w"], p["ml_conv_b"].reshape(1, 2 * ML_W), gate_b,
                  p["ml_head_g"].reshape(1, ML_W))

    y_mem = _memattn(zb3, mem, p["g_mem"].reshape(1, D_MODEL), p["w_mem_kv"].astype(BF16))

    x1 = _merge(x2, y_nsa.reshape(t, NSA_Q), y_ml.reshape(t, ML_W), y_mem.reshape(t, MEM_W), zb,
                p["w_proj_nsa"].astype(BF16), p["w_proj_ml"].astype(BF16),
                p["w_proj_mem"].astype(BF16), p["w_out"].astype(BF16),
                p["g_post_mix"].reshape(1, D_MODEL))
    x2o = _ffn(x1, p["g_pre_ffn"].reshape(1, D_MODEL), p["w_ffn_in"].astype(BF16),
               p["w_ffn_down"].astype(BF16), p["g_post_ffn"].reshape(1, D_MODEL))
    return x2o.reshape(b, s, D_MODEL)


def kernel(x, mem, g_pre_mix, w_in, cmp_pe_k, cmp_w1_k, cmp_w2_k, cmp_pe_v, cmp_w1_v, cmp_w2_v,
           ml_conv_w, ml_conv_b, ml_gate_b, ml_head_g, g_mem, w_mem_kv, w_proj_nsa, w_proj_ml,
           w_proj_mem, w_out, g_post_mix, g_pre_ffn, w_ffn_in, w_ffn_down, g_post_ffn):
    params = dict(
        g_pre_mix=g_pre_mix, w_in=w_in, cmp_pe_k=cmp_pe_k, cmp_w1_k=cmp_w1_k, cmp_w2_k=cmp_w2_k,
        cmp_pe_v=cmp_pe_v, cmp_w1_v=cmp_w1_v, cmp_w2_v=cmp_w2_v, ml_conv_w=ml_conv_w,
        ml_conv_b=ml_conv_b, ml_gate_b=ml_gate_b, ml_head_g=ml_head_g, g_mem=g_mem,
        w_mem_kv=w_mem_kv, w_proj_nsa=w_proj_nsa, w_proj_ml=w_proj_ml, w_proj_mem=w_proj_mem,
        w_out=w_out, g_post_mix=g_post_mix, g_pre_ffn=g_pre_ffn, w_ffn_in=w_ffn_in,
        w_ffn_down=w_ffn_down, g_post_ffn=g_post_ffn)
    depth = w_in.shape[0]
    for l in range(depth):
        x = _layer(x, mem, {k: v[l] for k, v in params.items()})
    return x
```

```python
import functools
import math

import jax
import jax.numpy as jnp
from jax import lax
from jax.experimental import pallas as pl
from jax.experimental.pallas import tpu as pltpu

F32 = jnp.float32
BF16 = jnp.bfloat16

D_MODEL = 1024
EPS = 1e-6
NEG_INF = -1e30
BIG = 1e30

NSA_HEADS = 16
NSA_GROUPS = 4
NSA_HPG = NSA_HEADS // NSA_GROUPS
NSA_DH = 64
NSA_SCALE = NSA_DH ** -0.5
CMP_BLOCK = 32
CMP_STRIDE = 16
CMP_HIDDEN = 128
SEL_BLOCK = 64
SEL_TOPK = 16
N_LOCAL_FORCED = 2
WINDOW = 512

ML_HEADS = 4
ML_DH = 128
ML_CHUNK = 128
CONV_WIDTH = 4

MEM_HEADS = 4
MEM_DH = 128
MEM_SCALE = MEM_DH ** -0.5

D_FF = -(-8 * D_MODEL // (3 * 256)) * 256

NSA_Q = NSA_HEADS * NSA_DH
NSA_KV = NSA_GROUPS * NSA_DH
ML_W = ML_HEADS * ML_DH
MEM_W = MEM_HEADS * MEM_DH

_OFF_Q = 0
_OFF_KV = _OFF_Q + NSA_Q
_OFF_GNSA = _OFF_KV + 6 * NSA_KV
_OFF_QKVML = _OFF_GNSA + 3 * NSA_HEADS
_OFF_IF = _OFF_QKVML + 3 * ML_W
_OFF_OML = _OFF_IF + 2 * ML_HEADS
_OFF_QMEM = _OFF_OML + ML_W
_OFF_GMERGE = _OFF_QMEM + MEM_W
_IN_WIDTH = _OFF_GMERGE + 3 * D_MODEL

ZB_GMERGE = 0
ZB_KSLC = ZB_GMERGE + 3 * D_MODEL
ZB_KWIN = ZB_KSLC + NSA_KV
ZB_QKVML = ZB_KWIN + NSA_KV
ZB_OML = ZB_QKVML + 3 * ML_W
ZB_QMEM = ZB_OML + ML_W
ZB_WIDTH = ZB_QMEM + MEM_W
KVC_WIDTH = 2 * NSA_KV
ZT_Q = 0
ZT_VSLC = ZT_Q + NSA_Q
ZT_VWIN = ZT_VSLC + NSA_KV
ZT_ROWS = ZT_VWIN + NSA_KV
ZG_WIDTH = 128
ZG_GNSA = 0
ZG_IF = 3 * NSA_HEADS

LANE = 128
SLAB_ROWS = 8
VMEM_LIMIT = 56 * 1024 * 1024
LOG2E = math.log2(math.e)

_NT = (((1,), (1,)), ((), ()))
_TN = (((0,), (0,)), ((), ()))


def _cparams(*sem):
    return pltpu.CompilerParams(dimension_semantics=sem, vmem_limit_bytes=VMEM_LIMIT)


def _rms(x, g):
    return x * lax.rsqrt(jnp.mean(x * x, axis=-1, keepdims=True) + EPS) * g


def _sigmoid(x):
    return 1.0 / (1.0 + jnp.exp(-x))


def _silu(x):
    return x * _sigmoid(x)


def _log_sigmoid(x):
    return jnp.minimum(x, 0.0) - jnp.log(1.0 + jnp.exp(-jnp.abs(x)))


def _dot_exact_lhs(a_bf16, x):
    hi = x.astype(BF16)
    r1 = x - hi.astype(F32)
    mid = r1.astype(BF16)
    lo = (r1 - mid.astype(F32)).astype(BF16)
    return (jnp.dot(a_bf16, hi, preferred_element_type=F32)
            + jnp.dot(a_bf16, mid, preferred_element_type=F32)
            + jnp.dot(a_bf16, lo, preferred_element_type=F32))


def _wprep_kernel(w_ref, wn_ref, wt_ref, wc_ref):
    def rows(lo, hi):
        return w_ref[lo:hi, :]

    def kv_rows(k):
        return rows(_OFF_KV + k * NSA_KV, _OFF_KV + (k + 1) * NSA_KV)

    wn_ref[ZB_GMERGE:ZB_KSLC, :] = rows(_OFF_GMERGE, _IN_WIDTH).astype(BF16)
    wn_ref[ZB_KSLC:ZB_KWIN, :] = kv_rows(2).astype(BF16)
    wn_ref[ZB_KWIN:ZB_QKVML, :] = kv_rows(4).astype(BF16)
    wn_ref[ZB_QKVML:ZB_OML, :] = rows(_OFF_QKVML, _OFF_IF).astype(BF16)
    wn_ref[ZB_OML:ZB_WIDTH, :] = rows(_OFF_OML, _OFF_GMERGE).astype(BF16)
    wt_ref[ZT_Q:ZT_VSLC, :] = (rows(_OFF_Q, _OFF_KV) * (NSA_SCALE * LOG2E)).astype(BF16)
    wt_ref[ZT_VSLC:ZT_VWIN, :] = kv_rows(3).astype(BF16)
    wt_ref[ZT_VWIN:ZT_ROWS, :] = kv_rows(5).astype(BF16)
    n_g = 3 * NSA_HEADS
    n_if = 2 * ML_HEADS
    wt_ref[ZT_ROWS:ZT_ROWS + ZG_WIDTH, :] = jnp.zeros((ZG_WIDTH, wt_ref.shape[1]), BF16)
    wt_ref[ZT_ROWS + ZG_GNSA:ZT_ROWS + ZG_GNSA + n_g, :] = rows(_OFF_GNSA, _OFF_QKVML).astype(BF16)
    wt_ref[ZT_ROWS + ZG_IF:ZT_ROWS + ZG_IF + n_if, :] = rows(_OFF_IF, _OFF_OML).astype(BF16)
    wc_ref[...] = rows(_OFF_KV, _OFF_KV + KVC_WIDTH).astype(BF16)


def _wprep(w_t, tl=128):
    assert ZB_QMEM == ZB_OML + ML_W and _OFF_QMEM == _OFF_OML + ML_W
    return pl.pallas_call(
        _wprep_kernel,
        grid=(D_MODEL // tl,),
        in_specs=[pl.BlockSpec((_IN_WIDTH, tl), lambda i: (0, i))],
        out_specs=[
            pl.BlockSpec((ZB_WIDTH, tl), lambda i: (0, i)),
            pl.BlockSpec((ZT_ROWS + ZG_WIDTH, tl), lambda i: (0, i)),
            pl.BlockSpec((KVC_WIDTH, tl), lambda i: (0, i)),
        ],
        out_shape=[
            jax.ShapeDtypeStruct((ZB_WIDTH, D_MODEL), BF16),
            jax.ShapeDtypeStruct((ZT_ROWS + ZG_WIDTH, D_MODEL), BF16),
            jax.ShapeDtypeStruct((KVC_WIDTH, D_MODEL), BF16),
        ],
        compiler_params=_cparams("parallel"),
        name="wprep",
    )(w_t)


def _inproj_kernel(x_ref, g_ref, wt_ref, wn_ref, wc_ref,
                   zt_ref, zb_ref, zg_ref, zgt_ref, kvc_ref, h_ref, *, nt):
    j = pl.program_id(1)

    @pl.when(j == 0)
    def _():
        hb = _rms(x_ref[...], g_ref[...]).astype(BF16)
        h_ref[...] = hb
        kvc_ref[...] = lax.dot_general(hb, wc_ref[...], _NT, preferred_element_type=F32)
        zf = lax.dot_general(wt_ref[...], hb, _NT, preferred_element_type=F32)
        zt_ref[...] = zf[0:ZT_ROWS, :].astype(BF16)
        gt = zf[ZT_ROWS:ZT_ROWS + ZG_WIDTH, :]
        zgt_ref[...] = gt
        zg_ref[...] = gt.T

    @pl.when(j >= nt)
    def _():
        tn = zb_ref.shape[1]
        r0 = pl.multiple_of((j - nt) * tn, tn)
        zb_ref[...] = lax.dot_general(h_ref[...], wn_ref[pl.ds(r0, tn), :], _NT,
                                      preferred_element_type=F32).astype(BF16)


def _inproj(x2, g, wt, wn, wc, b, s, tm=1024, nn=3):
    t = x2.shape[0]
    nt = 1
    tn = ZB_WIDTH // nn
    assert tn * nn == ZB_WIDTH and tn % LANE == 0 and s % tm == 0
    spb = s // tm
    kern = functools.partial(_inproj_kernel, nt=nt)
    once = dict(pipeline_mode=pl.Buffered(1))
    return pl.pallas_call(
        kern,
        grid=(t // tm, nt + nn),
        in_specs=[
            pl.BlockSpec((tm, D_MODEL), lambda i, j: (i, 0)),
            pl.BlockSpec((1, D_MODEL), lambda i, j: (0, 0)),
            pl.BlockSpec((ZT_ROWS + ZG_WIDTH, D_MODEL), lambda i, j: (0, 0), **once),
            pl.BlockSpec((ZB_WIDTH, D_MODEL), lambda i, j: (0, 0), **once),
            pl.BlockSpec((KVC_WIDTH, D_MODEL), lambda i, j: (0, 0), **once),
        ],
        out_specs=[
            pl.BlockSpec((None, ZT_ROWS, tm), lambda i, j: (i // spb, 0, i % spb)),
            pl.BlockSpec((tm, tn), lambda i, j: (i, jnp.maximum(j - nt, 0))),
            pl.BlockSpec((tm, ZG_WIDTH), lambda i, j: (i, 0)),
            pl.BlockSpec((None, ZG_WIDTH, tm), lambda i, j: (i // spb, 0, i % spb)),
            pl.BlockSpec((tm, KVC_WIDTH), lambda i, j: (i, 0)),
        ],
        out_shape=[
            jax.ShapeDtypeStruct((b, ZT_ROWS, s), BF16),
            jax.ShapeDtypeStruct((t, ZB_WIDTH), BF16),
            jax.ShapeDtypeStruct((t, ZG_WIDTH), F32),
            jax.ShapeDtypeStruct((b, ZG_WIDTH, s), F32),
            jax.ShapeDtypeStruct((t, KVC_WIDTH), F32),
        ],
        scratch_shapes=[pltpu.VMEM((tm, D_MODEL), BF16)],
        compiler_params=_cparams("parallel", "arbitrary"),
        name="inproj",
    )(x2, g, wt, wn, wc)


def _compress_kernel(a0_ref, a1_ref, a2_ref, a3_ref, pe_ref, w1_ref, w2k_ref, w2vt_ref,
                     kc_ref, vct_ref):
    a_refs = (a0_ref, a1_ref, a2_ref, a3_ref)
    n_rows = a0_ref.shape[0] // CMP_STRIDE
    half = CMP_STRIDE * NSA_DH
    zero = jnp.zeros((NSA_DH, CMP_HIDDEN), BF16)
    for kind in range(2):
        pe = pe_ref[kind]
        c = jnp.dot(pe, w1_ref[kind], preferred_element_type=F32)[0:1, :]
        wps = []
        for l in range(CMP_STRIDE):
            wa = w1_ref[kind, l * NSA_DH:(l + 1) * NSA_DH, :]
            wb = w1_ref[kind, half + l * NSA_DH:half + (l + 1) * NSA_DH, :]
            wps.append(jnp.concatenate([jnp.concatenate([wa, zero, wb, zero], axis=1),
                                        jnp.concatenate([zero, wa, zero, wb], axis=1)], axis=0))
        for pair in range(NSA_GROUPS // 2):
            a_ref = a_refs[kind * 2 + pair]
            acc = jnp.zeros((n_rows, 4 * CMP_HIDDEN), F32)
            for l in range(CMP_STRIDE):
                rows = a_ref[pl.ds(l, n_rows, stride=CMP_STRIDE), :].astype(BF16)
                acc = acc + jnp.dot(rows, wps[l], preferred_element_type=F32)
            for gi in range(2):
                g = 2 * pair + gi
                p = acc[:, gi * CMP_HIDDEN:(gi + 1) * CMP_HIDDEN]
                q = acc[:, (2 + gi) * CMP_HIDDEN:(3 + gi) * CMP_HIDDEN]
                qs = pltpu.roll(q, shift=n_rows - 1, axis=0)
                hid = _silu(p + qs + c).astype(BF16)
                if kind == 0:
                    o = jnp.dot(hid, w2k_ref[...], preferred_element_type=F32)
                    kc_ref[:, g * NSA_DH:(g + 1) * NSA_DH] = o.astype(BF16)
                else:
                    ot = lax.dot_general(w2vt_ref[...], hid, _NT, preferred_element_type=F32)
                    vct_ref[g * NSA_DH:(g + 1) * NSA_DH, :] = ot.astype(BF16)


def _compress(kvc3, pe, w1, w2k, w2vt):
    b, s, _ = kvc3.shape
    n_rows = s // CMP_STRIDE
    half = (CMP_BLOCK // 2) * NSA_DH

    def col(k):
        return pl.BlockSpec((None, s, LANE), lambda i, k=k: (i, 0, k))

    return pl.pallas_call(
        _compress_kernel,
        grid=(b,),
        in_specs=[
            col(0), col(1), col(2), col(3),
            pl.BlockSpec((2, 8, 2 * half), lambda i: (0, 0, 0)),
            pl.BlockSpec((2, 2 * half, CMP_HIDDEN), lambda i: (0, 0, 0)),
            pl.BlockSpec((CMP_HIDDEN, NSA_DH), lambda i: (0, 0)),
            pl.BlockSpec((NSA_DH, CMP_HIDDEN), lambda i: (0, 0)),
        ],
        out_specs=[
            pl.BlockSpec((None, n_rows, NSA_KV), lambda i: (i, 0, 0)),
            pl.BlockSpec((None, NSA_KV, n_rows), lambda i: (i, 0, 0)),
        ],
        out_shape=[
            jax.ShapeDtypeStruct((b, n_rows, NSA_KV), BF16),
            jax.ShapeDtypeStruct((b, NSA_KV, n_rows), BF16),
        ],
        compiler_params=_cparams("parallel"),
        name="compress",
    )(kvc3, kvc3, kvc3, kvc3, pe, w1, w2k, w2vt)


def _pad_query(qt, g):
    z = jnp.zeros_like(qt)
    return jnp.concatenate([qt, z] if g % 2 == 0 else [z, qt], axis=0)


def _cmpsel_kernel(qt_ref, kc_ref, vct_ref, ocmpt_ref, selb_ref, *, tq, n_sel, topk):
    nc = kc_ref.shape[0]
    t0 = pl.program_id(1) * tq

    n_t = lax.broadcasted_iota(jnp.int32, (nc, tq), 0)
    t_t = t0 + lax.broadcasted_iota(jnp.int32, (nc, tq), 1)
    mask_t = (n_t * CMP_STRIDE + (CMP_BLOCK - 1)) <= t_t

    j_m = lax.broadcasted_iota(jnp.int32, (n_sel, nc), 0) * SEL_BLOCK
    c_m = lax.broadcasted_iota(jnp.int32, (n_sel, nc), 1) * CMP_STRIDE
    ov = jnp.minimum(c_m + CMP_BLOCK, j_m + SEL_BLOCK) - jnp.maximum(c_m, j_m)
    map_b = (jnp.maximum(ov, 0).astype(F32) * (1.0 / CMP_BLOCK)).astype(BF16)

    j_s = lax.broadcasted_iota(jnp.int32, (n_sel, tq), 0)
    qblk = (t0 + lax.broadcasted_iota(jnp.int32, (n_sel, tq), 1)) // SEL_BLOCK
    rel = qblk - j_s
    causal = rel >= 0
    forced = causal & ((j_s == 0) | (rel < N_LOCAL_FORCED))

    def scores(g):
        kcp = kc_ref[:, (g // 2) * LANE:(g // 2 + 1) * LANE]
        return [jnp.dot(kcp, _pad_query(qt_ref[(g * NSA_HPG + h) * NSA_DH:(g * NSA_HPG + h + 1) * NSA_DH, :], g),
                        preferred_element_type=F32) for h in range(NSA_HPG)]

    sts_next = scores(0)
    for g in range(NSA_GROUPS):
        sts = sts_next
        if g + 1 < NSA_GROUPS:
            sts_next = scores(g + 1)
        vct = vct_ref[g * NSA_DH:(g + 1) * NSA_DH, :]
        masked = [jnp.where(mask_t, st, NEG_INF) for st in sts]
        ets = [jnp.where(mask_t, jnp.exp2(st - jnp.max(st, axis=0, keepdims=True)), 0.0) for st in masked]
        lts = [jnp.sum(et, axis=0, keepdims=True) for et in ets]
        pts = [et * (1.0 / jnp.where(lt > 0.0, lt, 1.0)) for et, lt in zip(ets, lts)]
        for h in range(NSA_HPG):
            hh = g * NSA_HPG + h
            ot = jnp.dot(vct, pts[h].astype(BF16), preferred_element_type=F32)
            ocmpt_ref[hh * NSA_DH:(hh + 1) * NSA_DH, :] = ot.astype(BF16)
        psum_t = (pts[0] + pts[1]) + (pts[2] + pts[3])
        imp_t = _dot_exact_lhs(map_b, psum_t)
        score = jnp.where(forced, BIG, jnp.where(causal, imp_t, NEG_INF))
        rank = jnp.zeros((n_sel, tq), jnp.int32)
        for jp in range(n_sel):
            row = score[jp:jp + 1, :]
            before = (row > score) | ((row == score) & (j_s > jp))
            rank = rank + before.astype(jnp.int32)
        chosen = (rank < topk) & (score > 0.5 * NEG_INF)
        selb_ref[g * n_sel:(g + 1) * n_sel, :] = jnp.where(chosen, 0.0, NEG_INF)


def _cmpsel(zt, kc, vct, tq=256):
    b, _, s = zt.shape
    n_sel = s // SEL_BLOCK
    topk = min(SEL_TOPK, n_sel)
    nc = kc.shape[1]
    kern = functools.partial(_cmpsel_kernel, tq=tq, n_sel=n_sel, topk=topk)
    return pl.pallas_call(
        kern,
        grid=(b, s // tq),
        in_specs=[
            pl.BlockSpec((None, NSA_Q, tq), lambda i, j: (i, ZT_Q // NSA_Q, j)),
            pl.BlockSpec((None, nc, NSA_KV), lambda i, j: (i, 0, 0)),
            pl.BlockSpec((None, NSA_KV, nc), lambda i, j: (i, 0, 0)),
        ],
        out_specs=[
            pl.BlockSpec((None, NSA_Q, tq), lambda i, j: (i, 0, j)),
            pl.BlockSpec((None, NSA_GROUPS * n_sel, tq), lambda i, j: (i, 0, j)),
        ],
        out_shape=[
            jax.ShapeDtypeStruct((b, NSA_Q, s), BF16),
            jax.ShapeDtypeStruct((b, NSA_GROUPS * n_sel, s), F32),
        ],
        compiler_params=_cparams("parallel", "parallel"),
        name="cmpsel",
    )(zt, kc, vct)


def _nsa_kernel(qt_ref, ks_ref, kw_ref, vst_ref, vwt_ref, selb_ref, ocmpt_ref, gt_ref, y_ref,
                yt_scr, qp_scr, m_scr, acc_scr, st_scr, mx_scr, bias_scr, *, tq, n_sel, look):
    tk = tq
    nw = WINDOW // tk
    sel_per_tile = tk // SEL_BLOCK
    den_rows = 16
    i = pl.program_id(1)
    d0 = (lax.broadcasted_iota(jnp.int32, (tk, tq), 1)
          - lax.broadcasted_iota(jnp.int32, (tk, tq), 0))
    caus = jnp.where(d0 >= 0, 0.0, NEG_INF)
    lowb = jnp.where(d0 < 0, 0.0, NEG_INF)
    ones_rows = jnp.ones((den_rows, tk), BF16)
    ext = 16
    key_blk = lax.broadcasted_iota(jnp.int32, (tk, ext), 0) // SEL_BLOCK
    ext_col = lax.broadcasted_iota(jnp.int32, (tk, ext), 1)
    gates = _sigmoid(gt_ref[...])

    for hh in range(NSA_HEADS):
        qp_scr[hh] = _pad_query(qt_ref[hh * NSA_DH:(hh + 1) * NSA_DH, :], hh // NSA_HPG)
    m_scr[...] = jnp.full(m_scr.shape, NEG_INF, F32)
    acc_scr[...] = jnp.zeros(acc_scr.shape, F32)

    SLC = (0, ks_ref, vst_ref, True)
    WIN = (1, kw_ref, vwt_ref, False)

    def run(tiles):
        loaded = []
        for ti, ((br, k_ref, vt_ref, sel_on), kt, mask) in enumerate(tiles):
            r0 = pl.multiple_of(kt * tk, tk)
            ktiles = [k_ref[pl.ds(r0, tk), pair * LANE:(pair + 1) * LANE]
                      for pair in range(NSA_GROUPS // 2)]
            vt_augs = [jnp.concatenate([vt_ref[g * NSA_DH:(g + 1) * NSA_DH, pl.ds(r0, tk)], ones_rows],
                                       axis=0) for g in range(NSA_GROUPS)]
            slabs = None
            if sel_on:
                row0 = kt * sel_per_tile
                slab0 = pl.multiple_of((row0 // SLAB_ROWS) * SLAB_ROWS, SLAB_ROWS)
                hot = jnp.where(key_blk + row0 % SLAB_ROWS == ext_col, 1.0, 0.0).astype(BF16)
                ktiles = [jnp.concatenate([kp, hot], axis=1) for kp in ktiles]
                slabs = [jnp.concatenate(
                    [selb_ref[pl.ds(g * n_sel + slab0, SLAB_ROWS), :],
                     jnp.zeros((ext - SLAB_ROWS, tq), F32)], axis=0).astype(BF16)
                    for g in range(NSA_GROUPS)]
            loaded.append((ktiles, vt_augs, slabs))
            if mask is not None:
                bias_scr[ti] = mask

        def qk(ti, hh):
            g = hh // NSA_HPG
            ktiles, _, slabs = loaded[ti]
            q = qp_scr[hh]
            if slabs is not None:
                q = jnp.concatenate([q, slabs[g]], axis=0)
            st = jnp.dot(ktiles[g // 2], q, preferred_element_type=F32)
            if tiles[ti][2] is not None:
                st = st + bias_scr[ti]
            st_scr[ti * NSA_HEADS + hh] = st
            mx_scr[ti * NSA_HEADS + hh] = jnp.max(st, axis=0, keepdims=True)

        def softmax_pv(ti, hh):
            br = tiles[ti][0][0]
            vt_aug = loaded[ti][1][hh // NSA_HPG]
            m = m_scr[br, hh]
            m_new = jnp.maximum(m, mx_scr[ti * NSA_HEADS + hh])
            alpha = jnp.exp2(m - m_new)
            p = jnp.exp2(st_scr[ti * NSA_HEADS + hh] - m_new).astype(BF16)
            acc_scr[br, hh] = alpha * acc_scr[br, hh] + jnp.dot(vt_aug, p, preferred_element_type=F32)
            m_scr[br, hh] = m_new

        seq = [(ti, hh) for ti in range(len(tiles)) for hh in range(NSA_HEADS)]
        for pos in range(min(look, len(seq))):
            qk(*seq[pos])
        for pos, item in enumerate(seq):
            if pos + look < len(seq):
                qk(*seq[pos + look])
            softmax_pv(*item)

    def slc_body(j, c):
        run([(SLC, 2 * j, None), (SLC, 2 * j + 1, None)])
        return c

    lax.fori_loop(0, i // 2, slc_body, 0)

    def tail(par, v):
        slc = [(SLC, i - 1, None)] * par + [(SLC, i, caus)]
        if v < nw:
            win = [(WIN, kt, None) for kt in range(v)] + [(WIN, i, caus)]
        else:
            win = ([(WIN, i - nw, lowb)] + [(WIN, i - nw + d, None) for d in range(1, nw)]
                   + [(WIN, i, caus)])
        run(slc + win)

    for v in range(nw):
        @pl.when(i == v)
        def _(v=v):
            tail(v % 2, v)

    for par in range(2):
        @pl.when((i >= nw) & (i % 2 == par))
        def _(par=par):
            tail(par, nw)

    for hh in range(NSA_HEADS):
        acc_s = acc_scr[0, hh]
        acc_w = acc_scr[1, hh]
        o_slc = acc_s[0:NSA_DH] * (1.0 / acc_s[NSA_DH:NSA_DH + 1])
        o_win = acc_w[0:NSA_DH] * (1.0 / acc_w[NSA_DH:NSA_DH + 1])
        o_cmp = ocmpt_ref[hh * NSA_DH:(hh + 1) * NSA_DH, :].astype(F32)
        gc = ZG_GNSA + 3 * hh
        yt_scr[hh * NSA_DH:(hh + 1) * NSA_DH, :] = (
            gates[gc:gc + 1] * o_cmp + gates[gc + 1:gc + 2] * o_slc + gates[gc + 2:gc + 3] * o_win)

    y_ref[...] = yt_scr[...].T.astype(BF16)


def _nsa(zt, zb3, selb, ocmpt, zgt, tq=256, look=5):
    b, _, s = zt.shape
    n_sel = s // SEL_BLOCK
    tk = tq
    den_rows = 16
    max_tiles = WINDOW // tk + 3
    assert WINDOW % tq == 0 and tq % SEL_BLOCK == 0 and SLAB_ROWS % (tk // SEL_BLOCK) == 0
    kern = functools.partial(_nsa_kernel, tq=tq, n_sel=n_sel, look=look)
    return pl.pallas_call(
        kern,
        grid=(b, s // tq),
        in_specs=[
            pl.BlockSpec((None, NSA_Q, tq), lambda bi, j: (bi, ZT_Q // NSA_Q, j)),
            pl.BlockSpec((None, s, NSA_KV), lambda bi, j: (bi, 0, ZB_KSLC // NSA_KV)),
            pl.BlockSpec((None, s, NSA_KV), lambda bi, j: (bi, 0, ZB_KWIN // NSA_KV)),
            pl.BlockSpec((None, NSA_KV, s), lambda bi, j: (bi, ZT_VSLC // NSA_KV, 0)),
            pl.BlockSpec((None, NSA_KV, s), lambda bi, j: (bi, ZT_VWIN // NSA_KV, 0)),
            pl.BlockSpec((None, NSA_GROUPS * n_sel, tq), lambda bi, j: (bi, 0, j)),
            pl.BlockSpec((None, NSA_Q, tq), lambda bi, j: (bi, 0, j)),
            pl.BlockSpec((None, ZG_WIDTH, tq), lambda bi, j: (bi, 0, j)),
        ],
        out_specs=pl.BlockSpec((None, tq, NSA_Q), lambda bi, j: (bi, j, 0)),
        out_shape=jax.ShapeDtypeStruct((b, s, NSA_Q), BF16),
        scratch_shapes=[
            pltpu.VMEM((NSA_Q, tq), F32),
            pltpu.VMEM((NSA_HEADS, 2 * NSA_DH, tq), BF16),
            pltpu.VMEM((2, NSA_HEADS, 1, tq), F32),
            pltpu.VMEM((2, NSA_HEADS, NSA_DH + den_rows, tq), F32),
            pltpu.VMEM((max_tiles * NSA_HEADS, tk, tq), F32),
            pltpu.VMEM((max_tiles * NSA_HEADS, 1, tq), F32),
            pltpu.VMEM((max_tiles, tk, tq), F32),
        ],
        compiler_params=_cparams("parallel", "arbitrary"),
        name="nsa",
    )(zt, zb3, zb3, zt, zt, selb, ocmpt, zgt)


def _mlstm_kernel(q_ref, k_ref, v_ref, o_ref, zg_ref, cw_ref, cb_ref, gb_ref, hg_ref, y_ref,
                  xbuf, c_st, n_st, m_st, *, nb):
    L = ML_CHUNK
    hist = xbuf.shape[1]
    @pl.when(pl.program_id(1) == 0)
    def _():
        xbuf[...] = jnp.zeros_like(xbuf)
        c_st[...] = jnp.zeros_like(c_st)
        n_st[...] = jnp.zeros_like(n_st)
        m_st[...] = jnp.zeros_like(m_st)

    row = lax.broadcasted_iota(jnp.int32, (L, L), 0)
    col = lax.broadcasted_iota(jnp.int32, (L, L), 1)
    tril = row >= col
    tril_b = jnp.where(tril, 1.0, 0.0).astype(BF16)
    streams = [(bb, h) for bb in range(nb) for h in range(ML_HEADS)]

    row_l = lax.broadcasted_iota(jnp.int32, (L, ZG_WIDTH), 0)
    g = []
    for bb in range(nb):
        gates = zg_ref[bb] + gb_ref[...]
        bcum = _dot_exact_lhs(tril_b, _log_sigmoid(gates))
        b_al = pltpu.roll(bcum, shift=ZG_WIDTH - ML_HEADS, axis=1)
        m_prev = m_st[bb, 0:1, :]
        b_end = b_al[L - 1:L, :]
        inter = b_al + m_prev
        wlog = b_end - b_al + gates
        m_new = jnp.maximum(b_end + m_prev, jnp.max(wlog, axis=0, keepdims=True))
        ws = jnp.exp(wlog - m_new)
        decay = jnp.exp(b_end + m_prev - m_new)
        m_st[bb, 0:1, :] = m_new
        r = gates - b_al
        cm = r
        d = 1
        while d < L:
            cm = jnp.maximum(cm, jnp.where(row_l >= d, pltpu.roll(cm, shift=d, axis=0), -jnp.inf))
            d *= 2
        m_t = jnp.maximum(inter, b_al + cm)
        iw = jnp.exp(inter - m_t)
        em = jnp.exp(-m_t)
        g.append((b_al, m_t, iw, em, ws, decay, r.T))

    sr = lax.broadcasted_iota(jnp.int32, (CONV_WIDTH * L, hist + L), 0)
    scol = lax.broadcasted_iota(jnp.int32, (CONV_WIDTH * L, hist + L), 1)
    shift = jnp.where(scol == hist - (CONV_WIDTH - 1) + sr % L + sr // L, 1.0, 0.0).astype(BF16)
    qks = []
    for bb in range(nb):
        halves = []
        for hf, x_ref in enumerate((q_ref, k_ref)):
            cur = x_ref[bb]
            ext = jnp.concatenate([xbuf[bb, :, hf * ML_W:(hf + 1) * ML_W], cur], axis=0)
            taps = jnp.dot(shift, ext, preferred_element_type=F32)
            conv = jnp.zeros((L, ML_W), F32) + cb_ref[:, hf * ML_W:(hf + 1) * ML_W]
            for j in range(CONV_WIDTH):
                conv = conv + taps[j * L:(j + 1) * L, :] * cw_ref[j:j + 1, hf * ML_W:(hf + 1) * ML_W]
            xbuf[bb, :, hf * ML_W:(hf + 1) * ML_W] = cur[L - hist:L, :]
            halves.append(_silu(conv))
        qks.append(jnp.concatenate(halves, axis=1))

    qs = {(bb, h): qks[bb][:, h * ML_DH:(h + 1) * ML_DH] for bb, h in streams}
    ks = {(bb, h): qks[bb][:, ML_W + h * ML_DH:ML_W + (h + 1) * ML_DH] * (ML_DH ** -0.5)
          for bb, h in streams}
    qbs = {s: qs[s].astype(BF16) for s in streams}
    vbs = {(bb, h): v_ref[bb, :, h * ML_DH:(h + 1) * ML_DH] for bb, h in streams}
    c_olds = {(bb, h): c_st[bb * ML_HEADS + h] for bb, h in streams}
    n_olds = {(bb, h): n_st[bb, h:h + 1, :] for bb, h in streams}
    s_qk = {s: lax.dot_general(qbs[s], ks[s].astype(BF16), _NT, preferred_element_type=F32)
            for s in streams}
    n_pad = jnp.zeros((ML_DH - 1, ML_DH), F32)
    cqs = {s: lax.dot_general(
        qbs[s], jnp.concatenate([c_olds[s], n_olds[s], n_pad], axis=0).astype(BF16), _NT,
        preferred_element_type=F32) for s in streams}
    ones_blk = jnp.ones((L, ML_DH), BF16)

    for bb, h in streams:
        c = ZG_IF + h
        ws, decay = g[bb][4], g[bb][5]
        kw = ks[bb, h] * ws[:, c:c + 1]
        upd = lax.dot_general(vbs[bb, h], kw.astype(BF16), _TN, preferred_element_type=F32)
        c_st[bb * ML_HEADS + h] = decay[:, c:c + 1] * c_olds[bb, h] + upd
        n_st[bb, h:h + 1, :] = decay[:, c:c + 1] * n_olds[bb, h] + jnp.sum(kw, axis=0, keepdims=True)

    nds = {}
    for bb, h in streams:
        c = ZG_IF + h
        b_al, m_t, _, _, _, _, r_t = g[bb]
        dlog = jnp.where(tril, b_al[:, c:c + 1] + r_t[c:c + 1, :], -jnp.inf)
        sqk = s_qk[bb, h] * jnp.exp(dlog - m_t[:, c:c + 1])
        v_aug = jnp.concatenate([vbs[bb, h], ones_blk], axis=1)
        nds[bb, h] = jnp.dot(sqk.astype(BF16), v_aug, preferred_element_type=F32)
    hss = {}
    for bb, h in streams:
        c = ZG_IF + h
        iw_c = g[bb][2][:, c:c + 1]
        nd, cq = nds[bb, h], cqs[bb, h]
        num = nd[:, 0:ML_DH] + iw_c * cq[:, 0:ML_DH]
        den = nd[:, ML_DH:ML_DH + 1] + iw_c * cq[:, ML_DH:ML_DH + 1]
        hss[bb, h] = num / jnp.maximum(jnp.abs(den), g[bb][3][:, c:c + 1])
    mss = {s: jnp.mean(hss[s] * hss[s], axis=-1, keepdims=True) for s in streams}
    for bb, h in streams:
        hn = hss[bb, h] * lax.rsqrt(mss[bb, h] + EPS) * hg_ref[:, h * ML_DH:(h + 1) * ML_DH]
        og = _sigmoid(o_ref[bb, :, h * ML_DH:(h + 1) * ML_DH].astype(F32))
        y_ref[bb, :, h * ML_DH:(h + 1) * ML_DH] = (og * hn).astype(BF16)


def _mlstm(zb3, zg3, conv_w, conv_b, gate_b, head_g, nb=2):
    b, s, _ = zb3.shape
    L = ML_CHUNK
    qb = ZB_QKVML // ML_W
    assert b % nb == 0

    def zspec(k):
        return pl.BlockSpec((nb, L, ML_W), lambda bi, c, k=k: (bi, c, k))

    return pl.pallas_call(
        functools.partial(_mlstm_kernel, nb=nb),
        grid=(b // nb, s // L),
        in_specs=[
            zspec(qb), zspec(qb + 1), zspec(qb + 2), zspec(ZB_OML // ML_W),
            pl.BlockSpec((nb, L, ZG_WIDTH), lambda bi, c: (bi, c, 0)),
            pl.BlockSpec((CONV_WIDTH, 2 * ML_W), lambda bi, c: (0, 0)),
            pl.BlockSpec((1, 2 * ML_W), lambda bi, c: (0, 0)),
            pl.BlockSpec((1, ZG_WIDTH), lambda bi, c: (0, 0)),
            pl.BlockSpec((1, ML_W), lambda bi, c: (0, 0)),
        ],
        out_specs=pl.BlockSpec((nb, L, ML_W), lambda bi, c: (bi, c, 0)),
        out_shape=jax.ShapeDtypeStruct((b, s, ML_W), BF16),
        scratch_shapes=[
            pltpu.VMEM((nb, 16, 2 * ML_W), BF16),
            pltpu.VMEM((nb * ML_HEADS, ML_DH, ML_DH), F32),
            pltpu.VMEM((nb, 8, ML_DH), F32),
            pltpu.VMEM((nb, 8, LANE), F32),
        ],
        compiler_params=_cparams("parallel", "arbitrary"),
        name="mlstm",
    )(zb3, zb3, zb3, zb3, zg3, conv_w, conv_b, gate_b, head_g)


def _memattn_kernel(q_ref, mem_ref, g_ref, w_ref, y_ref, kv_ref):
    @pl.when(pl.program_id(1) == 0)
    def _():
        hb = _rms(mem_ref[...], g_ref[...]).astype(BF16)
        kv_ref[...] = jnp.dot(hb, w_ref[...], preferred_element_type=F32).astype(BF16)

    c = MEM_SCALE * LOG2E
    scores = [lax.dot_general(q_ref[:, h * MEM_DH:(h + 1) * MEM_DH], kv_ref[:, h * MEM_DH:(h + 1) * MEM_DH],
                              _NT, preferred_element_type=F32) for h in range(MEM_HEADS)]
    for h in range(MEM_HEADS):
        s = scores[h]
        v = kv_ref[:, MEM_W + h * MEM_DH:MEM_W + (h + 1) * MEM_DH]
        e = jnp.exp2((s - jnp.max(s, axis=-1, keepdims=True)) * c)
        l = jnp.sum(e, axis=-1, keepdims=True)
        o = jnp.dot(e.astype(BF16), v, preferred_element_type=F32) * (1.0 / l)
        y_ref[:, h * MEM_DH:(h + 1) * MEM_DH] = o.astype(BF16)


def _memattn(zb3, mem, g, wb, tq=1024):
    b, s, _ = zb3.shape
    m = mem.shape[1]
    return pl.pallas_call(
        _memattn_kernel,
        grid=(b, s // tq),
        in_specs=[
            pl.BlockSpec((None, tq, MEM_W), lambda i, j: (i, j, ZB_QMEM // MEM_W)),
            pl.BlockSpec((None, m, D_MODEL), lambda i, j: (i, 0, 0)),
            pl.BlockSpec((1, D_MODEL), lambda i, j: (0, 0)),
            pl.BlockSpec((D_MODEL, 2 * MEM_W), lambda i, j: (0, 0), pipeline_mode=pl.Buffered(1)),
        ],
        out_specs=pl.BlockSpec((None, tq, MEM_W), lambda i, j: (i, j, 0)),
        out_shape=jax.ShapeDtypeStruct((b, s, MEM_W), BF16),
        scratch_shapes=[pltpu.VMEM((m, 2 * MEM_W), BF16)],
        compiler_params=_cparams("parallel", "arbitrary"),
        name="memattn",
    )(zb3, mem, g, wb)


def _merge_kernel(x_ref, yn_ref, yl_ref, ym_ref, g0_ref, g1_ref, g2_ref,
                  wn_ref, wl_ref, wm_ref, wo_ref, gp_ref, out_ref):
    y = _sigmoid(g0_ref[...].astype(F32)) * jnp.dot(yn_ref[...], wn_ref[...], preferred_element_type=F32)
    y = y + _sigmoid(g1_ref[...].astype(F32)) * jnp.dot(yl_ref[...], wl_ref[...], preferred_element_type=F32)
    y = y + _sigmoid(g2_ref[...].astype(F32)) * jnp.dot(ym_ref[...], wm_ref[...], preferred_element_type=F32)
    u = jnp.dot(y.astype(BF16), wo_ref[...], preferred_element_type=F32)
    out_ref[...] = x_ref[...] + _rms(u, gp_ref[...])


def _merge(x2, yn, yl, ym, zb, wn, wl, wm, wo, gp, tm=1024):
    t = x2.shape[0]
    gm = ZB_GMERGE // D_MODEL

    def const(shape):
        return pl.BlockSpec(shape, lambda i: (0, 0), pipeline_mode=pl.Buffered(1))

    return pl.pallas_call(
        _merge_kernel,
        grid=(t // tm,),
        in_specs=[
            pl.BlockSpec((tm, D_MODEL), lambda i: (i, 0)),
            pl.BlockSpec((tm, NSA_Q), lambda i: (i, 0)),
            pl.BlockSpec((tm, ML_W), lambda i: (i, 0)),
            pl.BlockSpec((tm, MEM_W), lambda i: (i, 0)),
            pl.BlockSpec((tm, D_MODEL), lambda i: (i, gm)),
            pl.BlockSpec((tm, D_MODEL), lambda i: (i, gm + 1)),
            pl.BlockSpec((tm, D_MODEL), lambda i: (i, gm + 2)),
            const((NSA_Q, D_MODEL)), const((ML_W, D_MODEL)), const((MEM_W, D_MODEL)),
            const((D_MODEL, D_MODEL)), const((1, D_MODEL)),
        ],
        out_specs=pl.BlockSpec((tm, D_MODEL), lambda i: (i, 0)),
        out_shape=jax.ShapeDtypeStruct((t, D_MODEL), F32),
        compiler_params=_cparams("parallel"),
        name="merge",
    )(x2, yn, yl, ym, zb, zb, zb, wn, wl, wm, wo, gp)


def _ffn_kernel(x_ref, gpre_ref, wg_ref, wu_ref, wd_ref, gpost_ref, out_ref, h_ref, acc_ref):
    j = pl.program_id(1)

    @pl.when(j == 0)
    def _():
        h_ref[...] = _rms(x_ref[...], gpre_ref[...]).astype(BF16)
        acc_ref[...] = jnp.zeros_like(acc_ref)

    h = h_ref[...]
    gate = jnp.dot(h, wg_ref[...], preferred_element_type=F32)
    up = jnp.dot(h, wu_ref[...], preferred_element_type=F32)
    act = (_silu(gate) * up).astype(BF16)
    acc_ref[...] += jnp.dot(act, wd_ref[...], preferred_element_type=F32)

    @pl.when(j == pl.num_programs(1) - 1)
    def _():
        out_ref[...] = x_ref[...] + _rms(acc_ref[...], gpost_ref[...])


def _ffn(x2, gpre, w_in, w_down, gpost, tm=512, nf=1):
    t = x2.shape[0]
    tf = D_FF // nf
    assert tf % LANE == 0
    wmode = dict(pipeline_mode=pl.Buffered(1)) if nf == 1 else {}
    return pl.pallas_call(
        _ffn_kernel,
        grid=(t // tm, nf),
        in_specs=[
            pl.BlockSpec((tm, D_MODEL), lambda i, j: (i, 0)),
            pl.BlockSpec((1, D_MODEL), lambda i, j: (0, 0)),
            pl.BlockSpec((D_MODEL, tf), lambda i, j: (0, j), **wmode),
            pl.BlockSpec((D_MODEL, tf), lambda i, j: (0, nf + j), **wmode),
            pl.BlockSpec((tf, D_MODEL), lambda i, j: (j, 0), **wmode),
            pl.BlockSpec((1, D_MODEL), lambda i, j: (0, 0)),
        ],
        out_specs=pl.BlockSpec((tm, D_MODEL), lambda i, j: (i, 0)),
        out_shape=jax.ShapeDtypeStruct((t, D_MODEL), F32),
        scratch_shapes=[pltpu.VMEM((tm, D_MODEL), BF16), pltpu.VMEM((tm, D_MODEL), F32)],
        compiler_params=_cparams("parallel", "arbitrary"),
        name="ffn",
    )(x2, gpre, w_in, w_in, w_down, gpost)


def _layer(x, mem, p):
    b, s, _ = x.shape
    t = b * s
    x2 = x.reshape(t, D_MODEL)

    wn, wt, wc = _wprep(p["w_in"].T)
    zt, zb, zg, zgt, kvc = _inproj(x2, p["g_pre_mix"].reshape(1, D_MODEL), wt, wn, wc, b, s)
    zb3 = zb.reshape(b, s, ZB_WIDTH)
    zg3 = zg.reshape(b, s, ZG_WIDTH)

    half = (CMP_BLOCK // 2) * NSA_DH
    pe = jnp.stack([p["cmp_pe_k"], p["cmp_pe_v"]]).reshape(2, 1, 2 * half)
    pe = jnp.pad(pe, ((0, 0), (0, 7), (0, 0))).astype(BF16)
    w1 = jnp.stack([p["cmp_w1_k"], p["cmp_w1_v"]]).reshape(2, 2 * half, CMP_HIDDEN).astype(BF16)
    kc, vct = _compress(kvc.reshape(b, s, KVC_WIDTH), pe, w1,
                        p["cmp_w2_k"].astype(BF16), p["cmp_w2_v"].T.astype(BF16))
    ocmpt, selb = _cmpsel(zt, kc, vct)
    y_nsa = _nsa(zt, zb3, selb, ocmpt, zgt)

    gate_b = jnp.zeros((1, ZG_WIDTH), F32).at[0, ZG_IF:ZG_IF + 2 * ML_HEADS].set(p["ml_gate_b"])
    y_ml = _mlstm(zb3, zg3, p["ml_conv_w"], p["ml_conv_b"].reshape(1, 2 * ML_W), gate_b,
                  p["ml_head_g"].reshape(1, ML_W))

    y_mem = _memattn(zb3, mem, p["g_mem"].reshape(1, D_MODEL), p["w_mem_kv"].astype(BF16))

    x1 = _merge(x2, y_nsa.reshape(t, NSA_Q), y_ml.reshape(t, ML_W), y_mem.reshape(t, MEM_W), zb,
                p["w_proj_nsa"].astype(BF16), p["w_proj_ml"].astype(BF16),
                p["w_proj_mem"].astype(BF16), p["w_out"].astype(BF16),
                p["g_post_mix"].reshape(1, D_MODEL))
    x2o = _ffn(x1, p["g_pre_ffn"].reshape(1, D_MODEL), p["w_ffn_in"].astype(BF16),
               p["w_ffn_down"].astype(BF16), p["g_post_ffn"].reshape(1, D_MODEL))
    return x2o.reshape(b, s, D_MODEL)


def kernel(x, mem, g_pre_mix, w_in, cmp_pe_k, cmp_w1_k, cmp_w2_k, cmp_pe_v, cmp_w1_v, cmp_w2_v,
           ml_conv_w, ml_conv_b, ml_gate_b, ml_head_g, g_mem, w_mem_kv, w_proj_nsa, w_proj_ml,
           w_proj_mem, w_out, g_post_mix, g_pre_ffn, w_ffn_in, w_ffn_down, g_post_ffn):
    params = dict(
        g_pre_mix=g_pre_mix, w_in=w_in, cmp_pe_k=cmp_pe_k, cmp_w1_k=cmp_w1_k, cmp_w2_k=cmp_w2_k,
        cmp_pe_v=cmp_pe_v, cmp_w1_v=cmp_w1_v, cmp_w2_v=cmp_w2_v, ml_conv_w=ml_conv_w,
        ml_conv_b=ml_conv_b, ml_gate_b=ml_gate_b, ml_head_g=ml_head_g, g_mem=g_mem,
        w_mem_kv=w_mem_kv, w_proj_nsa=w_proj_nsa, w_proj_ml=w_proj_ml, w_proj_mem=w_proj_mem,
        w_out=w_out, g_post_mix=g_post_mix, g_pre_ffn=g_pre_ffn, w_ffn_in=w_ffn_in,
        w_ffn_down=w_ffn_down, g_post_ffn=g_post_ffn)
    depth = w_in.shape[0]
    for l in range(depth):
        x = _layer(x, mem, {k: v[l] for k, v in params.items()})
    return x
```
